```python
import jax, jax.numpy as jnp
from jax import lax
import numpy as np

D_MODEL = 1024
BATCH = 8
SEQ = 2048
DEPTH = 4

N_MIXERS = 3
HEAD_DIM = 64
ROPE_THETA = 10000.0
RMS_EPS = 1e-6
NEG_INF = -1e30
GRID_W = 64

A_HEADS = D_MODEL // HEAD_DIM
A_GROUPS = ((128, 1), (512, 4), (2048, 16))
A_QBLOCK = 64
A_WIDTH = A_HEADS * HEAD_DIM

B_HEADS = D_MODEL // HEAD_DIM
NA_KH = 8
NA_KW = 16
NA_QCOLS = 16
NA_KCOLS = 32
B_WIDTH = B_HEADS * HEAD_DIM

C_Q_HEADS = D_MODEL // HEAD_DIM
C_KV_HEADS = 4
C_QBLOCK = 128
C_WIDTH = C_Q_HEADS * HEAD_DIM

D_FF = 4 * D_MODEL

kernel_name = "interleaved_dilated_neighbourhood_gqa_encoder"


def rms_norm(x, g):
    xf = x.astype(jnp.float32)
    y = xf * lax.rsqrt(jnp.mean(xf * xf, axis=-1, keepdims=True) + RMS_EPS)
    return (y * g.astype(jnp.float32)).astype(x.dtype)


def rope_angles(pos, dim):
    inv = 1.0 / (ROPE_THETA ** (jnp.arange(0, dim, 2, dtype=jnp.float32) / dim))
    return pos.astype(jnp.float32)[:, None] * inv[None, :]


def apply_rope(x, cos, sin):
    xf = x.astype(jnp.float32)
    x1, x2 = jnp.split(xf, 2, axis=-1)
    c = cos[:, None, :]
    s = sin[:, None, :]
    return jnp.concatenate([x1 * c - x2 * s, x2 * c + x1 * s], axis=-1).astype(x.dtype)


def dilated_group(q, k, v, dilation, half):
    B, S, H, Dh = q.shape
    L = S // dilation
    nb = -(-L // A_QBLOCK)
    Lp = nb * A_QBLOCK
    span_blocks = 1 + (2 * half) // A_QBLOCK
    span = span_blocks * A_QBLOCK
    qs = q.reshape(B, L, dilation, H, Dh)
    ks = k.reshape(B, L, dilation, H, Dh)
    vs = v.reshape(B, L, dilation, H, Dh)
    pad_q = ((0, 0), (0, Lp - L), (0, 0), (0, 0), (0, 0))
    pad_k = ((0, 0), (half, Lp - L + half), (0, 0), (0, 0), (0, 0))
    qb = jnp.pad(qs, pad_q).reshape(B, nb, A_QBLOCK, dilation, H, Dh)
    kb = jnp.pad(ks, pad_k).reshape(B, nb + span_blocks - 1, A_QBLOCK, dilation, H, Dh)
    vb = jnp.pad(vs, pad_k).reshape(B, nb + span_blocks - 1, A_QBLOCK, dilation, H, Dh)
    kw = jnp.concatenate([kb[:, s:s + nb] for s in range(span_blocks)], axis=2)
    vw = jnp.concatenate([vb[:, s:s + nb] for s in range(span_blocks)], axis=2)
    qi = np.arange(nb)[:, None, None] * A_QBLOCK + np.arange(A_QBLOCK)[None, :, None]
    kj = np.arange(nb)[:, None, None] * A_QBLOCK + np.arange(span)[None, None, :] - half
    mask = (kj >= 0) & (kj < L) & (np.abs(qi - kj) <= half)
    scale = Dh ** -0.5
    s = jnp.einsum('bnqrhe,bnkrhe->brhnqk', qb, kw, preferred_element_type=jnp.float32) * scale
    s = jnp.where(mask, s, NEG_INF)
    m = jnp.max(s, axis=-1, keepdims=True)
    p = jnp.exp(s - m)
    den = jnp.sum(p, axis=-1)
    o = jnp.einsum('brhnqk,bnkrhe->bnqrhe', p.astype(v.dtype), vw, preferred_element_type=jnp.float32)
    o = o / den.transpose(0, 3, 4, 1, 2)[..., None]
    lse = (m[..., 0] + jnp.log(den)).transpose(0, 3, 4, 1, 2)
    o = o.reshape(B, Lp, dilation, H, Dh)[:, :L].reshape(B, S, H, Dh)
    lse = lse.reshape(B, Lp, dilation, H)[:, :L].reshape(B, S, H)
    return o, lse


def mixer_a(h, w_in, w_out, cos, sin):
    B, S, _ = h.shape
    proj = (h @ w_in).reshape(B, S, len(A_GROUPS), 3, A_HEADS, HEAD_DIM)
    outs, lses = [], []
    for g, (window, dilation) in enumerate(A_GROUPS):
        q = apply_rope(proj[:, :, g, 0], cos, sin)
        k = apply_rope(proj[:, :, g, 1], cos, sin)
        v = proj[:, :, g, 2]
        o, lse = dilated_group(q, k, v, dilation, window // (2 * dilation))
        outs.append(o)
        lses.append(lse)
    wts = jax.nn.softmax(jnp.stack(lses, axis=0), axis=0)
    o = jnp.sum(wts[..., None] * jnp.stack(outs, axis=0), axis=0)
    return o.reshape(B, S, A_WIDTH).astype(h.dtype) @ w_out


def mixer_b(h, w_in, rpb, w_out):
    B, S, _ = h.shape
    rows = S // GRID_W
    kh = min(NA_KH, rows)
    qkv = (h @ w_in).reshape(B, rows, GRID_W, 3, B_HEADS, HEAD_DIM)
    q, k, v = qkv[:, :, :, 0], qkv[:, :, :, 1], qkv[:, :, :, 2]
    nqb = GRID_W // NA_QCOLS
    qc = np.arange(GRID_W).reshape(nqb, NA_QCOLS)
    kstart = np.clip(np.arange(nqb) * NA_QCOLS - (NA_KCOLS - NA_QCOLS) // 2, 0, GRID_W - NA_KCOLS)
    kc = kstart[:, None] + np.arange(NA_KCOLS)[None, :]
    cs = np.clip(qc - NA_KW // 2, 0, GRID_W - NA_KW)
    col_mask = (kc[:, None, :] >= cs[:, :, None]) & (kc[:, None, :] < cs[:, :, None] + NA_KW)
    dcol_idx = np.clip(kc[:, None, :] - qc[:, :, None] + NA_KW - 1, 0, 2 * NA_KW - 2)
    col_bias = rpb[:, :, dcol_idx]
    scale = HEAD_DIM ** -0.5

    def row_fn(r):
        rs = jnp.clip(r - kh // 2, 0, rows - kh)
        k_rows = lax.dynamic_slice_in_dim(k, rs, kh, axis=1)
        v_rows = lax.dynamic_slice_in_dim(v, rs, kh, axis=1)
        q_row = lax.dynamic_index_in_dim(q, r, axis=1, keepdims=False)
        q_cb = q_row.reshape(B, nqb, NA_QCOLS, B_HEADS, HEAD_DIM)
        k_cb = jnp.stack([k_rows[:, :, int(s0):int(s0) + NA_KCOLS] for s0 in kstart], axis=1)
        v_cb = jnp.stack([v_rows[:, :, int(s0):int(s0) + NA_KCOLS] for s0 in kstart], axis=1)
        s = jnp.einsum('bjqhd,bjakhd->bhjqak', q_cb, k_cb, preferred_element_type=jnp.float32) * scale
        drow_idx = rs + jnp.arange(kh) - r + NA_KH - 1
        bias = col_bias[:, drow_idx].transpose(0, 2, 3, 1, 4)
        s = s + bias[None].astype(jnp.float32)
        s = jnp.where(col_mask[:, :, None, :], s, NEG_INF)
        p = jax.nn.softmax(s.reshape(B, B_HEADS, nqb, NA_QCOLS, kh * NA_KCOLS), axis=-1)
        o = jnp.einsum('bhjqn,bjnhd->bjqhd', p.astype(v.dtype),
                       v_cb.reshape(B, nqb, kh * NA_KCOLS, B_HEADS, HEAD_DIM))
        return o.reshape(B, GRID_W, B_HEADS, HEAD_DIM)

    out = lax.map(row_fn, jnp.arange(rows, dtype=jnp.int32))
    out = out.transpose(1, 0, 2, 3, 4).reshape(B, S, B_WIDTH)
    return out @ w_out


def mixer_c(h, w_in, q_norm, k_norm, w_out, cos, sin):
    B, S, _ = h.shape
    G = C_Q_HEADS // C_KV_HEADS
    proj = h @ w_in
    nq = C_Q_HEADS * HEAD_DIM
    nk = C_KV_HEADS * HEAD_DIM
    q = proj[..., :nq].reshape(B, S, C_Q_HEADS, HEAD_DIM)
    k = proj[..., nq:nq + nk].reshape(B, S, C_KV_HEADS, HEAD_DIM)
    v = proj[..., nq + nk:].reshape(B, S, C_KV_HEADS, HEAD_DIM)
    q = apply_rope(rms_norm(q, q_norm), cos, sin)
    k = apply_rope(rms_norm(k, k_norm), cos, sin)
    nqb = S // C_QBLOCK
    qb = q.reshape(B, nqb, C_QBLOCK, C_KV_HEADS, G, HEAD_DIM).transpose(1, 0, 2, 3, 4, 5)
    scale = HEAD_DIM ** -0.5

    def block_fn(qblk):
        s = jnp.einsum('bqkgd,bskd->bkgqs', qblk, k, preferred_element_type=jnp.float32) * scale
        p = jax.nn.softmax(s, axis=-1)
        return jnp.einsum('bkgqs,bskd->bqkgd', p.astype(v.dtype), v)

    out = lax.map(block_fn, qb)
    out = out.transpose(1, 0, 2, 3, 4, 5).reshape(B, S, C_WIDTH)
    return out @ w_out


def squared_relu_mlp(h, w_up, w_down):
    u = h @ w_up
    return jnp.square(jax.nn.relu(u)) @ w_down


def setup_inputs(seed: int = 0) -> dict:
    key = jax.random.key(seed)
    ks = iter(jax.random.split(key, 10 * DEPTH + 4))

    def nrm(shape, scale):
        return jax.random.normal(next(ks), shape, jnp.float32) * scale

    def gain(n):
        return 1.0 + nrm((n,), 0.05)

    p = {"x": nrm((BATCH, SEQ, D_MODEL), 1.0)}
    for i in range(DEPTH):
        kind = i % N_MIXERS
        p[f"l{i}_attn_norm"] = gain(D_MODEL)
        if kind == 0:
            p[f"l{i}_w_in"] = nrm((D_MODEL, len(A_GROUPS) * 3 * A_WIDTH), D_MODEL ** -0.5)
            p[f"l{i}_w_out"] = nrm((A_WIDTH, D_MODEL), A_WIDTH ** -0.5)
        elif kind == 1:
            p[f"l{i}_w_in"] = nrm((D_MODEL, 3 * B_WIDTH), D_MODEL ** -0.5)
            p[f"l{i}_rpb"] = nrm((B_HEADS, 2 * NA_KH - 1, 2 * NA_KW - 1), 0.1)
            p[f"l{i}_w_out"] = nrm((B_WIDTH, D_MODEL), B_WIDTH ** -0.5)
        else:
            p[f"l{i}_w_in"] = nrm((D_MODEL, (C_Q_HEADS + 2 * C_KV_HEADS) * HEAD_DIM), D_MODEL ** -0.5)
            p[f"l{i}_q_norm"] = gain(HEAD_DIM)
            p[f"l{i}_k_norm"] = gain(HEAD_DIM)
            p[f"l{i}_w_out"] = nrm((C_WIDTH, D_MODEL), C_WIDTH ** -0.5)
        p[f"l{i}_mlp_norm"] = gain(D_MODEL)
        p[f"l{i}_w_up"] = nrm((D_MODEL, D_FF), D_MODEL ** -0.5)
        p[f"l{i}_w_down"] = nrm((D_FF, D_MODEL), D_FF ** -0.5)
    p["final_norm"] = gain(D_MODEL)
    return p


def reference(x,
              l0_attn_norm, l0_w_in, l0_w_out, l0_mlp_norm, l0_w_up, l0_w_down,
              l1_attn_norm, l1_w_in, l1_rpb, l1_w_out, l1_mlp_norm, l1_w_up, l1_w_down,
              l2_attn_norm, l2_w_in, l2_q_norm, l2_k_norm, l2_w_out, l2_mlp_norm, l2_w_up, l2_w_down,
              l3_attn_norm, l3_w_in, l3_w_out, l3_mlp_norm, l3_w_up, l3_w_down,
              final_norm):
    S = x.shape[1]
    t = jnp.arange(S, dtype=jnp.int32)
    ang_a = rope_angles(t, HEAD_DIM)
    ang_c = jnp.concatenate([rope_angles(t // GRID_W, HEAD_DIM // 2),
                             rope_angles(t % GRID_W, HEAD_DIM // 2)], axis=-1)
    cos_a, sin_a = jnp.cos(ang_a), jnp.sin(ang_a)
    cos_c, sin_c = jnp.cos(ang_c), jnp.sin(ang_c)

    attn_norms = (l0_attn_norm, l1_attn_norm, l2_attn_norm, l3_attn_norm)
    mixer_params = ((l0_w_in, l0_w_out),
                    (l1_w_in, l1_rpb, l1_w_out),
                    (l2_w_in, l2_q_norm, l2_k_norm, l2_w_out),
                    (l3_w_in, l3_w_out))
    mlp_norms = (l0_mlp_norm, l1_mlp_norm, l2_mlp_norm, l3_mlp_norm)
    w_ups = (l0_w_up, l1_w_up, l2_w_up, l3_w_up)
    w_downs = (l0_w_down, l1_w_down, l2_w_down, l3_w_down)

    for i in range(DEPTH):
        kind = i % N_MIXERS
        h = rms_norm(x, attn_norms[i])
        if kind == 0:
            y = mixer_a(h, *mixer_params[i], cos_a, sin_a)
        elif kind == 1:
            y = mixer_b(h, *mixer_params[i])
        else:
            y = mixer_c(h, *mixer_params[i], cos_c, sin_c)
        x = x + y
        x = x + squared_relu_mlp(rms_norm(x, mlp_norms[i]), w_ups[i], w_downs[i])
    return rms_norm(x, final_norm)
```

```python
import functools

import jax
import jax.numpy as jnp
import numpy as np
from jax import lax
from jax.experimental import pallas as pl
from jax.experimental.pallas import tpu as pltpu

D_MODEL = 1024
HEAD_DIM = 64
N_HEADS = 16
D_FF = 4 * D_MODEL
ROPE_THETA = 10000.0
RMS_EPS = 1e-6
NEG_INF = -1e30
GRID_W = 64
A_GROUPS = ((128, 1), (512, 4), (2048, 16))
C_KV_HEADS = 4
QK_SCALE = HEAD_DIM ** -0.5

LANES = 128
HALF = HEAD_DIM // 2
VMEM_LIMIT = 56 * 1024 * 1024

NT_DIMS = (((1,), (1,)), ((), ()))


def _cparams(sem):
    return pltpu.CompilerParams(dimension_semantics=sem, vmem_limit_bytes=VMEM_LIMIT)


def _rms(x, g):
    ms = jnp.mean(x * x, axis=-1, keepdims=True)
    return (x * lax.rsqrt(ms + RMS_EPS)) * g


def _lane_is_first_head():
    return lax.broadcasted_iota(jnp.int32, (1, LANES), 1) < HEAD_DIM


def _rope128(y, cos, sin_signed):
    lane = lax.broadcasted_iota(jnp.int32, (1, LANES), 1)
    first_half = (lane % HEAD_DIM) < HALF
    partner = jnp.where(first_half, pltpu.roll(y, LANES - HALF, 1), pltpu.roll(y, HALF, 1))
    return y * cos + partner * sin_signed


def _proj_kernel(x_ref, g_ref, w_ref, cos_ref, sin_ref, o_ref, h_ref, *, rope, sections):
    n = pl.program_id(1)

    @pl.when(n == 0)
    def _():
        h_ref[...] = _rms(x_ref[...], g_ref[...]).astype(jnp.bfloat16)

    y = jnp.dot(h_ref[...], w_ref[...], preferred_element_type=jnp.float32)
    tn = y.shape[1]
    sec = n % sections
    scale = jnp.where(sec == 0, QK_SCALE, 1.0).astype(jnp.float32)

    if rope:
        @pl.when(sec == 2)
        def _():
            o_ref[...] = y.astype(o_ref.dtype)

        @pl.when(sec != 2)
        def _():
            cos = cos_ref[...]
            sin = sin_ref[...]
            for c in range(tn // LANES):
                sl = slice(c * LANES, (c + 1) * LANES)
                o_ref[:, sl] = (_rope128(y[:, sl], cos, sin) * scale).astype(o_ref.dtype)
    else:
        o_ref[...] = (y * scale).astype(o_ref.dtype)


def _proj(x2, g, w, cos, sin, *, rope, tm=1024, tn=1024):
    T, D = x2.shape
    N = w.shape[1]
    S = cos.shape[0]
    nsb = S // tm
    kern = functools.partial(_proj_kernel, rope=rope, sections=3)
    return pl.pallas_call(
        kern,
        grid=(T // tm, N // tn),
        in_specs=[
            pl.BlockSpec((tm, D), lambda i, n: (i, 0)),
            pl.BlockSpec((1, D), lambda i, n: (0, 0)),
            pl.BlockSpec((D, tn), lambda i, n: (0, n)),
            pl.BlockSpec((tm, LANES), lambda i, n: (i % nsb, 0)),
            pl.BlockSpec((tm, LANES), lambda i, n: (i % nsb, 0)),
        ],
        out_specs=pl.BlockSpec((tm, tn), lambda i, n: (i, n)),
        out_shape=jax.ShapeDtypeStruct((T, N), jnp.bfloat16),
        scratch_shapes=[pltpu.VMEM((tm, D), jnp.bfloat16)],
        compiler_params=_cparams(("parallel", "arbitrary")),
        name="proj",
    )(x2, g, w, cos, sin)


def _proj_c_kernel(x_ref, g_ref, w_ref, cos_ref, sin_ref, qg_ref, kg_ref, bd_ref,
                   q_ref, k_ref, v_ref):
    h = _rms(x_ref[...], g_ref[...]).astype(jnp.bfloat16)
    y = jnp.dot(h, w_ref[...], preferred_element_type=jnp.float32)
    cos = cos_ref[...]
    sin = sin_ref[...]
    bd = bd_ref[...]
    is_a = _lane_is_first_head()

    def head_norm(c, gain):
        yc = y[:, c * LANES:(c + 1) * LANES]
        sq = yc * yc
        hi = sq.astype(jnp.bfloat16)
        lo = (sq - hi.astype(jnp.float32)).astype(jnp.bfloat16)
        ms = (jnp.dot(hi, bd, preferred_element_type=jnp.float32)
              + jnp.dot(lo, bd, preferred_element_type=jnp.float32))
        return (yc * lax.rsqrt(ms + RMS_EPS)) * gain

    nq = D_MODEL // LANES
    for c in range(nq):
        qn = head_norm(c, qg_ref[...])
        q_ref[:, c * LANES:(c + 1) * LANES] = (_rope128(qn, cos, sin) * QK_SCALE).astype(q_ref.dtype)
    nkv = C_KV_HEADS * HEAD_DIM // LANES
    for c in range(nkv):
        kn = _rope128(head_norm(nq + c, kg_ref[...]), cos, sin)
        ksw = pltpu.roll(kn, HEAD_DIM, 1)
        k_ref[:, (2 * c) * LANES:(2 * c + 1) * LANES] = jnp.where(is_a, kn, ksw).astype(k_ref.dtype)
        k_ref[:, (2 * c + 1) * LANES:(2 * c + 2) * LANES] = jnp.where(is_a, ksw, kn).astype(k_ref.dtype)
        vc = y[:, (nq + nkv + c) * LANES:(nq + nkv + c + 1) * LANES]
        vsw = pltpu.roll(vc, HEAD_DIM, 1)
        v_ref[:, (2 * c) * LANES:(2 * c + 1) * LANES] = jnp.where(is_a, vc, vsw).astype(v_ref.dtype)
        v_ref[:, (2 * c + 1) * LANES:(2 * c + 2) * LANES] = jnp.where(is_a, vsw, vc).astype(v_ref.dtype)


def _proj_c(x2, g, w, cos, sin, qg, kg, bd, *, tm=512):
    T, D = x2.shape
    N = w.shape[1]
    S = cos.shape[0]
    nsb = S // tm
    kvw = 2 * C_KV_HEADS * HEAD_DIM
    full = lambda shape: pl.BlockSpec(shape, lambda i: (0, 0))
    return pl.pallas_call(
        _proj_c_kernel,
        grid=(T // tm,),
        in_specs=[
            pl.BlockSpec((tm, D), lambda i: (i, 0)),
            full((1, D)),
            full((D, N)),
            pl.BlockSpec((tm, LANES), lambda i: (i % nsb, 0)),
            pl.BlockSpec((tm, LANES), lambda i: (i % nsb, 0)),
            full((1, LANES)),
            full((1, LANES)),
            full((LANES, LANES)),
        ],
        out_specs=[
            pl.BlockSpec((tm, D_MODEL), lambda i: (i, 0)),
            pl.BlockSpec((tm, kvw), lambda i: (i, 0)),
            pl.BlockSpec((tm, kvw), lambda i: (i, 0)),
        ],
        out_shape=[
            jax.ShapeDtypeStruct((T, D_MODEL), jnp.bfloat16),
            jax.ShapeDtypeStruct((T, kvw), jnp.bfloat16),
            jax.ShapeDtypeStruct((T, kvw), jnp.bfloat16),
        ],
        compiler_params=_cparams(("parallel",)),
        name="proj_c",
    )(x2, g, w, cos, sin, qg, kg, bd)


def _stack_heads(q2, is_a):
    zero = jnp.zeros_like(q2)
    return jnp.concatenate([jnp.where(is_a, q2, zero), jnp.where(is_a, zero, q2)], axis=0)


def _softmax_pv(s, vw):
    m = jnp.max(s, axis=-1, keepdims=True)
    p = jnp.exp(s - m)
    den = jnp.sum(p, axis=-1, keepdims=True)
    pv = jnp.dot(p.astype(vw.dtype), vw, preferred_element_type=jnp.float32)
    return pv / den, m + jnp.log(den)


def _attn_a_kernel(q_ref, k_ref, v_ref, o_ref, lse_ref, *, L, qb, W, half):
    hc = q_ref.shape[2]
    is_a = _lane_is_first_head()
    row = lax.broadcasted_iota(jnp.int32, (2 * qb, W), 0) % qb
    col = lax.broadcasted_iota(jnp.int32, (2 * qb, W), 1)
    rel = row - col

    for p in range(hc // LANES):
        cols = slice(p * LANES, (p + 1) * LANES)

        def body(iq, carry):
            qs = pl.multiple_of(iq * qb, qb)
            ks = pl.multiple_of(jnp.clip(qs - half, 0, L - W), half)
            qst = _stack_heads(q_ref[0, pl.ds(qs, qb), cols], is_a)
            kw = k_ref[0, pl.ds(ks, W), cols]
            vw = v_ref[0, pl.ds(ks, W), cols]
            s = lax.dot_general(qst, kw, NT_DIMS, preferred_element_type=jnp.float32)
            s = jnp.where(jnp.abs(rel + (qs - ks)) <= half, s, NEG_INF)
            o, lse = _softmax_pv(s, vw)
            o_ref[0, pl.ds(qs, qb), cols] = jnp.where(is_a, o[:qb], o[qb:]).astype(o_ref.dtype)
            lse_ref[0, pl.ds(qs, qb), cols] = jnp.where(is_a, lse[:qb], lse[qb:])
            return carry

        lax.fori_loop(0, L // qb, body, 0)


def _attn_a_group(qkv, g, *, B, S, hc):
    window, d = A_GROUPS[g]
    half = window // (2 * d)
    L = S // d
    qb = min(128, L)
    W = min(L, qb + 2 * half)
    ncol = qkv.shape[2]
    nch = D_MODEL // hc
    view = qkv.reshape(B, L, d * ncol)
    base = g * 3 * D_MODEL

    def in_spec(j):
        off = (base + j * D_MODEL) // hc
        return pl.BlockSpec((1, L, hc), lambda b, r, c: (b, 0, r * (ncol // hc) + off + c))

    out_spec = pl.BlockSpec((1, L, hc), lambda b, r, c: (b, 0, r * nch + c))
    kern = functools.partial(_attn_a_kernel, L=L, qb=qb, W=W, half=half)
    o, lse = pl.pallas_call(
        kern,
        grid=(B, d, nch),
        in_specs=[in_spec(0), in_spec(1), in_spec(2)],
        out_specs=[out_spec, out_spec],
        out_shape=[
            jax.ShapeDtypeStruct((B, L, d * D_MODEL), jnp.bfloat16),
            jax.ShapeDtypeStruct((B, L, d * D_MODEL), jnp.float32),
        ],
        compiler_params=_cparams(("parallel", "parallel", "parallel")),
        name=f"attn_a{g}",
    )(view, view, view)
    return o.reshape(B * S, D_MODEL), lse.reshape(B * S, D_MODEL)


def _merge_out_kernel(x_ref, o0_ref, o1_ref, o2_ref, l0_ref, l1_ref, l2_ref, w_ref, y_ref):
    l0, l1, l2 = l0_ref[...], l1_ref[...], l2_ref[...]
    m = jnp.maximum(jnp.maximum(l0, l1), l2)
    e0, e1, e2 = jnp.exp(l0 - m), jnp.exp(l1 - m), jnp.exp(l2 - m)
    num = (e0 * o0_ref[...].astype(jnp.float32) + e1 * o1_ref[...].astype(jnp.float32)
           + e2 * o2_ref[...].astype(jnp.float32))
    a = (num / (e0 + e1 + e2)).astype(jnp.bfloat16)
    y_ref[...] = x_ref[...] + jnp.dot(a, w_ref[...], preferred_element_type=jnp.float32)


def _merge_out(x2, os_, ls_, w, *, tm=512):
    T, D = x2.shape
    row = pl.BlockSpec((tm, D), lambda i: (i, 0))
    return pl.pallas_call(
        _merge_out_kernel,
        grid=(T // tm,),
        in_specs=[row] * 7 + [pl.BlockSpec((D, D), lambda i: (0, 0))],
        out_specs=row,
        out_shape=jax.ShapeDtypeStruct((T, D), jnp.float32),
        compiler_params=_cparams(("parallel",)),
        name="merge_out",
    )(x2, *os_, *ls_, w)


def _out_kernel(x_ref, a_ref, w_ref, y_ref):
    y_ref[...] = x_ref[...] + jnp.dot(a_ref[...], w_ref[...], preferred_element_type=jnp.float32)


def _out_proj(x2, a, w, *, tm=1024):
    T, D = x2.shape
    row = pl.BlockSpec((tm, D), lambda i: (i, 0))
    return pl.pallas_call(
        _out_kernel,
        grid=(T // tm,),
        in_specs=[row, row, pl.BlockSpec((D, D), lambda i: (0, 0))],
        out_specs=row,
        out_shape=jax.ShapeDtypeStruct((T, D), jnp.float32),
        compiler_params=_cparams(("parallel",)),
        name="out_proj",
    )(x2, a, w)


B_QROWS = 4
B_KROWS = 12
NA_KH = 8
NA_KW = 16


def _attn_b_kernel(q_ref, k_ref, v_ref, t_ref, o_ref, *, rows):
    is_a = _lane_is_first_head()
    nq = B_QROWS * GRID_W
    nk = B_KROWS * GRID_W
    units = rows // B_QROWS
    for u in range(units):
        qs = u * nq
        kr0 = min(max(u * B_QROWS - NA_KH // 2, 0), rows - B_KROWS)
        ks = kr0 * GRID_W
        geo = 0 if u == 0 else (2 if u == units - 1 else 1)
        qst = _stack_heads(q_ref[0, qs:qs + nq, :], is_a)
        kw = k_ref[0, ks:ks + nk, :]
        vw = v_ref[0, ks:ks + nk, :]
        s = lax.dot_general(qst, kw, NT_DIMS, preferred_element_type=jnp.float32)
        tbl = t_ref[:, geo].reshape(2 * nq, nk)
        s = jnp.where(tbl > 0.5 * NEG_INF, s + tbl, NEG_INF)
        o, _ = _softmax_pv(s, vw)
        o_ref[0, qs:qs + nq, :] = jnp.where(is_a, o[:nq], o[nq:]).astype(o_ref.dtype)


def _attn_b(qkv, table, *, B, S):
    rows = S // GRID_W
    npair = D_MODEL // LANES
    nq = B_QROWS * GRID_W
    nk = B_KROWS * GRID_W

    def in_spec(j):
        return pl.BlockSpec((1, S, LANES), lambda p, b: (b, 0, j * npair + p))

    kern = functools.partial(_attn_b_kernel, rows=rows)
    return pl.pallas_call(
        kern,
        grid=(npair, B),
        in_specs=[in_spec(0), in_spec(1), in_spec(2),
                  pl.BlockSpec((2, 3, nq, nk), lambda p, b: (p, 0, 0, 0))],
        out_specs=pl.BlockSpec((1, S, LANES), lambda p, b: (b, 0, p)),
        out_shape=jax.ShapeDtypeStruct((B, S, D_MODEL), jnp.bfloat16),
        compiler_params=_cparams(("parallel", "parallel")),
        name="attn_b",
    )(qkv, qkv, qkv, table)


def _b_bias_table(rpb, rows):
    H = rpb.shape[0]
    qc = np.arange(GRID_W)[:, None]
    kc = np.arange(GRID_W)[None, :]
    cs = np.clip(qc - NA_KW // 2, 0, GRID_W - NA_KW)
    col_ok = (kc >= cs) & (kc < cs + NA_KW)
    dc = np.clip(kc - qc + NA_KW - 1, 0, 2 * NA_KW - 2)
    onehot = ((dc[None] == np.arange(2 * NA_KW - 1)[:, None, None]) & col_ok[None]).astype(np.float32)
    cval = jnp.einsum("hac,cqk->haqk", rpb, jnp.asarray(onehot), precision=lax.Precision.HIGHEST)
    slabs = jnp.where(jnp.asarray(col_ok)[None, None], cval, NEG_INF)
    slabs = jnp.concatenate([slabs, jnp.full((H, 1, GRID_W, GRID_W), NEG_INF, jnp.float32)], axis=1)

    units = rows // B_QROWS
    idx = np.zeros((3, B_QROWS, B_KROWS), np.int32)
    for geo, u in enumerate((0, 1, units - 1)):
        kr0 = min(max(u * B_QROWS - NA_KH // 2, 0), rows - B_KROWS)
        for a in range(B_QROWS):
            qr = u * B_QROWS + a
            rs = min(max(qr - NA_KH // 2, 0), rows - NA_KH)
            for c in range(B_KROWS):
                kr = kr0 + c
                idx[geo, a, c] = kr - qr + NA_KH - 1 if rs <= kr < rs + NA_KH else 2 * NA_KH - 1
    t = jnp.take(slabs, jnp.asarray(idx.reshape(-1)), axis=1)
    t = t.reshape(H, 3, B_QROWS, B_KROWS, GRID_W, GRID_W).transpose(0, 1, 2, 4, 3, 5)
    return t.reshape(H, 3, B_QROWS * GRID_W, B_KROWS * GRID_W)


def _attn_c_kernel(q_ref, k_ref, v_ref, o_ref):
    is_a = _lane_is_first_head()
    tq = q_ref.shape[1]
    q = q_ref[0]
    qst = jnp.concatenate([_stack_heads(q[:, :LANES], is_a), _stack_heads(q[:, LANES:], is_a)], axis=0)
    s = lax.dot_general(qst, k_ref[0], NT_DIMS, preferred_element_type=jnp.float32)
    o, _ = _softmax_pv(s, v_ref[0])
    o_ref[0, :, :LANES] = jnp.where(is_a, o[:tq], o[tq:2 * tq]).astype(o_ref.dtype)
    o_ref[0, :, LANES:] = jnp.where(is_a, o[2 * tq:3 * tq], o[3 * tq:]).astype(o_ref.dtype)


def _attn_c(q, k2, v2, *, B, S, tq=128):
    gw = 2 * LANES
    return pl.pallas_call(
        _attn_c_kernel,
        grid=(B, C_KV_HEADS, S // tq),
        in_specs=[
            pl.BlockSpec((1, tq, gw), lambda b, g, i: (b, i, g)),
            pl.BlockSpec((1, S, LANES), lambda b, g, i: (b, 0, g)),
            pl.BlockSpec((1, S, LANES), lambda b, g, i: (b, 0, g)),
        ],
        out_specs=pl.BlockSpec((1, tq, gw), lambda b, g, i: (b, i, g)),
        out_shape=jax.ShapeDtypeStruct((B, S, D_MODEL), jnp.bfloat16),
        compiler_params=_cparams(("parallel", "parallel", "parallel")),
        name="attn_c",
    )(q, k2, v2)


def _mlp_kernel(x_ref, g_ref, wu_ref, wd_ref, gf_ref, y_ref, h_ref, acc_ref, *, final_norm):
    f = pl.program_id(1)

    @pl.when(f == 0)
    def _():
        h_ref[...] = _rms(x_ref[...], g_ref[...]).astype(jnp.bfloat16)
        acc_ref[...] = jnp.zeros_like(acc_ref)

    u = jnp.dot(h_ref[...], wu_ref[...], preferred_element_type=jnp.float32)
    r = jnp.maximum(u, 0.0)
    acc_ref[...] += jnp.dot((r * r).astype(jnp.bfloat16), wd_ref[...], preferred_element_type=jnp.float32)

    @pl.when(f == pl.num_programs(1) - 1)
    def _():
        y = x_ref[...] + acc_ref[...]
        if final_norm:
            y = _rms(y, gf_ref[...])
        y_ref[...] = y


def _mlp(x2, g, wu, wd, gf, *, final_norm, tm=1024, tf=512):
    T, D = x2.shape
    F = wu.shape[1]
    kern = functools.partial(_mlp_kernel, final_norm=final_norm)
    return pl.pallas_call(
        kern,
        grid=(T // tm, F // tf),
        in_specs=[
            pl.BlockSpec((tm, D), lambda i, f: (i, 0)),
            pl.BlockSpec((1, D), lambda i, f: (0, 0)),
            pl.BlockSpec((D, tf), lambda i, f: (0, f)),
            pl.BlockSpec((tf, D), lambda i, f: (f, 0)),
            pl.BlockSpec((1, D), lambda i, f: (0, 0)),
        ],
        out_specs=pl.BlockSpec((tm, D), lambda i, f: (i, 0)),
        out_shape=jax.ShapeDtypeStruct((T, D), jnp.float32),
        scratch_shapes=[pltpu.VMEM((tm, D), jnp.bfloat16), pltpu.VMEM((tm, D), jnp.float32)],
        compiler_params=_cparams(("parallel", "arbitrary")),
        name="mlp",
    )(x2, g, wu, wd, gf)


def _rope_angles(pos, dim):
    inv = 1.0 / (ROPE_THETA ** (jnp.arange(0, dim, 2, dtype=jnp.float32) / dim))
    return pos.astype(jnp.float32)[:, None] * inv[None, :]


def _rope_tables(ang):
    cos, sin = jnp.cos(ang), jnp.sin(ang)
    return jnp.tile(cos, (1, 4)), jnp.tile(jnp.concatenate([-sin, sin], axis=-1), (1, 2))


def kernel(x, l0_attn_norm, l0_w_in, l0_w_out, l0_mlp_norm, l0_w_up, l0_w_down, l1_attn_norm, l1_w_in, l1_rpb, l1_w_out, l1_mlp_norm, l1_w_up, l1_w_down, l2_attn_norm, l2_w_in, l2_q_norm, l2_k_norm, l2_w_out, l2_mlp_norm, l2_w_up, l2_w_down, l3_attn_norm, l3_w_in, l3_w_out, l3_mlp_norm, l3_w_up, l3_w_down, final_norm):
    B, S, D = x.shape
    bf = lambda w: w.astype(jnp.bfloat16)
    row = lambda g: g.reshape(1, -1).astype(jnp.float32)

    t = jnp.arange(S, dtype=jnp.int32)
    cos_a, sin_a = _rope_tables(_rope_angles(t, HEAD_DIM))
    cos_c, sin_c = _rope_tables(jnp.concatenate(
        [_rope_angles(t // GRID_W, HALF), _rope_angles(t % GRID_W, HALF)], axis=-1))

    x2 = x.reshape(B * S, D)

    def layer_a(x2, attn_norm, w_in, w_out):
        qkv = _proj(x2, row(attn_norm), bf(w_in), cos_a, sin_a, rope=True).reshape(B, S, -1)
        os_, ls_ = [], []
        for g, hc in enumerate((512, 1024, 1024)):
            o, lse = _attn_a_group(qkv, g, B=B, S=S, hc=hc)
            os_.append(o)
            ls_.append(lse)
        return _merge_out(x2, os_, ls_, bf(w_out))

    def layer_b(x2, attn_norm, w_in, rpb, w_out):
        qkv = _proj(x2, row(attn_norm), bf(w_in), cos_a, sin_a, rope=False).reshape(B, S, -1)
        table = _b_bias_table(rpb.astype(jnp.float32), S // GRID_W)
        a = _attn_b(qkv, table, B=B, S=S)
        return _out_proj(x2, a.reshape(B * S, D), bf(w_out))

    def layer_c(x2, attn_norm, w_in, q_norm, k_norm, w_out):
        gain2 = lambda g: jnp.tile(g.astype(jnp.float32), 2).reshape(1, LANES)
        bd = jnp.kron(jnp.eye(2, dtype=jnp.float32),
                      jnp.full((HEAD_DIM, HEAD_DIM), 1.0 / HEAD_DIM, jnp.float32)).astype(jnp.bfloat16)
        q, k2, v2 = _proj_c(x2, row(attn_norm), bf(w_in), cos_c, sin_c, gain2(q_norm), gain2(k_norm), bd)
        a = _attn_c(q.reshape(B, S, -1), k2.reshape(B, S, -1), v2.reshape(B, S, -1), B=B, S=S)
        return _out_proj(x2, a.reshape(B * S, D), bf(w_out))

    fn = row(final_norm)
    x2 = layer_a(x2, l0_attn_norm, l0_w_in, l0_w_out)
    x2 = _mlp(x2, row(l0_mlp_norm), bf(l0_w_up), bf(l0_w_down), fn, final_norm=False)
    x2 = layer_b(x2, l1_attn_norm, l1_w_in, l1_rpb, l1_w_out)
    x2 = _mlp(x2, row(l1_mlp_norm), bf(l1_w_up), bf(l1_w_down), fn, final_norm=False)
    x2 = layer_c(x2, l2_attn_norm, l2_w_in, l2_q_norm, l2_k_norm, l2_w_out)
    x2 = _mlp(x2, row(l2_mlp_norm), bf(l2_w_up), bf(l2_w_down), fn, final_norm=False)
    x2 = layer_a(x2, l3_attn_norm, l3_w_in, l3_w_out)
    x2 = _mlp(x2, row(l3_mlp_norm), bf(l3_w_up), bf(l3_w_down), fn, final_norm=True)
    return x2.reshape(B, S, D)
```

```python
import functools

import jax
import jax.numpy as jnp
import numpy as np
from jax import lax
from jax.experimental import pallas as pl
from jax.experimental.pallas import tpu as pltpu

D_MODEL = 1024
HEAD_DIM = 64
N_HEADS = 16
D_FF = 4 * D_MODEL
ROPE_THETA = 10000.0
RMS_EPS = 1e-6
NEG_INF = -1e30
GRID_W = 64
A_GROUPS = ((128, 1), (512, 4), (2048, 16))
C_KV_HEADS = 4
QK_SCALE = HEAD_DIM ** -0.5

LANES = 128
HALF = HEAD_DIM // 2
VMEM_LIMIT = 56 * 1024 * 1024

NT_DIMS = (((1,), (1,)), ((), ()))


def _cparams(sem):
    return pltpu.CompilerParams(dimension_semantics=sem, vmem_limit_bytes=VMEM_LIMIT)


def _rms(x, g):
    ms = jnp.mean(x * x, axis=-1, keepdims=True)
    return (x * lax.rsqrt(ms + RMS_EPS)) * g


def _lane_is_first_head():
    return lax.broadcasted_iota(jnp.int32, (1, LANES), 1) < HEAD_DIM


def _rope128(y, cos, sin_signed):
    lane = lax.broadcasted_iota(jnp.int32, (1, LANES), 1)
    first_half = (lane % HEAD_DIM) < HALF
    partner = jnp.where(first_half, pltpu.roll(y, LANES - HALF, 1), pltpu.roll(y, HALF, 1))
    return y * cos + partner * sin_signed


PERM_GROUP = 256
PROJ_ROWS = 256


def _token_perm(S):
    t = np.arange(S).reshape(S // PERM_GROUP, 16, 4, 4)
    return t.transpose(3, 0, 2, 1).reshape(S)


def _group_perm_matrix():
    token = np.arange(PERM_GROUP).reshape(16, 4, 4).transpose(2, 1, 0).reshape(-1)
    pm = np.zeros((PERM_GROUP, PERM_GROUP), np.float32)
    pm[token, np.arange(PERM_GROUP)] = 1.0
    return pm


def _proj_kernel(x_ref, g_ref, w_ref, cos_ref, sin_ref, pm_ref, o_ref, h_ref, *, rope, perm):
    n = pl.program_id(1)
    tm = x_ref.shape[0]
    tn = o_ref.shape[1]

    @pl.when(n == 0)
    def _():
        g = g_ref[...]
        if perm:
            ng = tm // PERM_GROUP
            run = PERM_GROUP // 4
            for grp in range(ng):
                hn = _rms(x_ref[grp * PERM_GROUP:(grp + 1) * PERM_GROUP, :], g).astype(jnp.bfloat16)
                hp = jnp.dot(pm_ref[...], hn, preferred_element_type=jnp.float32).astype(jnp.bfloat16)
                for rho in range(4):
                    dst = (rho * ng + grp) * run
                    h_ref[dst:dst + run, :] = hp[rho * run:(rho + 1) * run, :]
        else:
            h_ref[...] = _rms(x_ref[...], g).astype(jnp.bfloat16)

    sec = (n * tn // D_MODEL) % 3
    scale = jnp.where(sec == 0, QK_SCALE, 1.0).astype(jnp.float32)

    def sub_blocks(epilogue):
        for r0 in range(0, tm, PROJ_ROWS):
            rows = slice(r0, r0 + PROJ_ROWS)
            y = jnp.dot(h_ref[rows, :], w_ref[...], preferred_element_type=jnp.float32)
            epilogue(rows, y)

    def plain(rows, y):
        o_ref[rows, :] = (y * scale).astype(o_ref.dtype)

    def roped(rows, y):
        cos = cos_ref[rows, :]
        sin = sin_ref[rows, :]
        for c in range(tn // LANES):
            sl = slice(c * LANES, (c + 1) * LANES)
            o_ref[rows, sl] = (_rope128(y[:, sl], cos, sin) * scale).astype(o_ref.dtype)

    if rope:
        @pl.when(sec == 2)
        def _():
            sub_blocks(plain)

        @pl.when(sec != 2)
        def _():
            sub_blocks(roped)
    else:
        sub_blocks(plain)


def _proj(x2, g, w, cos, sin, *, rope, perm=False, tm=1024, tn=1024):
    T, D = x2.shape
    N = w.shape[1]
    S = cos.shape[0]
    nsb = S // tm
    kern = functools.partial(_proj_kernel, rope=rope, perm=perm)
    return pl.pallas_call(
        kern,
        grid=(T // tm, N // tn),
        in_specs=[
            pl.BlockSpec((tm, D), lambda i, n: (i, 0)),
            pl.BlockSpec((1, D), lambda i, n: (0, 0)),
            pl.BlockSpec((D, tn), lambda i, n: (0, n)),
            pl.BlockSpec((tm, LANES), lambda i, n: (i % nsb, 0)),
            pl.BlockSpec((tm, LANES), lambda i, n: (i % nsb, 0)),
            pl.BlockSpec((PERM_GROUP, PERM_GROUP), lambda i, n: (0, 0)),
        ],
        out_specs=pl.BlockSpec((tm, tn), lambda i, n: (i, n)),
        out_shape=jax.ShapeDtypeStruct((T, N), jnp.bfloat16),
        scratch_shapes=[pltpu.VMEM((tm, D), jnp.bfloat16)],
        compiler_params=_cparams(("parallel", "arbitrary")),
        name="proj_perm" if perm else "proj",
    )(x2, g, w, cos, sin, jnp.asarray(_group_perm_matrix().T, jnp.bfloat16))


def _proj_c_kernel(x_ref, g_ref, w_ref, cos_ref, sin_ref, qg_ref, kg_ref, bd_ref,
                   q_ref, k_ref, v_ref):
    h = _rms(x_ref[...], g_ref[...]).astype(jnp.bfloat16)
    y = jnp.dot(h, w_ref[...], preferred_element_type=jnp.float32)
    cos = cos_ref[...]
    sin = sin_ref[...]
    bd = bd_ref[...]
    is_a = _lane_is_first_head()

    def head_norm(c, gain):
        yc = y[:, c * LANES:(c + 1) * LANES]
        sq = yc * yc
        hi = sq.astype(jnp.bfloat16)
        lo = (sq - hi.astype(jnp.float32)).astype(jnp.bfloat16)
        ms = (jnp.dot(hi, bd, preferred_element_type=jnp.float32)
              + jnp.dot(lo, bd, preferred_element_type=jnp.float32))
        return (yc * lax.rsqrt(ms + RMS_EPS)) * gain

    nq = D_MODEL // LANES
    for c in range(nq):
        qn = head_norm(c, qg_ref[...])
        q_ref[:, c * LANES:(c + 1) * LANES] = (_rope128(qn, cos, sin) * QK_SCALE).astype(q_ref.dtype)
    nkv = C_KV_HEADS * HEAD_DIM // LANES
    for c in range(nkv):
        kn = _rope128(head_norm(nq + c, kg_ref[...]), cos, sin)
        ksw = pltpu.roll(kn, HEAD_DIM, 1)
        k_ref[:, (2 * c) * LANES:(2 * c + 1) * LANES] = jnp.where(is_a, kn, ksw).astype(k_ref.dtype)
        k_ref[:, (2 * c + 1) * LANES:(2 * c + 2) * LANES] = jnp.where(is_a, ksw, kn).astype(k_ref.dtype)
        vc = y[:, (nq + nkv + c) * LANES:(nq + nkv + c + 1) * LANES]
        vsw = pltpu.roll(vc, HEAD_DIM, 1)
        v_ref[:, (2 * c) * LANES:(2 * c + 1) * LANES] = jnp.where(is_a, vc, vsw).astype(v_ref.dtype)
        v_ref[:, (2 * c + 1) * LANES:(2 * c + 2) * LANES] = jnp.where(is_a, vsw, vc).astype(v_ref.dtype)


def _proj_c(x2, g, w, cos, sin, qg, kg, bd, *, tm=512):
    T, D = x2.shape
    N = w.shape[1]
    S = cos.shape[0]
    nsb = S // tm
    kvw = 2 * C_KV_HEADS * HEAD_DIM
    full = lambda shape: pl.BlockSpec(shape, lambda i: (0, 0))
    return pl.pallas_call(
        _proj_c_kernel,
        grid=(T // tm,),
        in_specs=[
            pl.BlockSpec((tm, D), lambda i: (i, 0)),
            full((1, D)),
            full((D, N)),
            pl.BlockSpec((tm, LANES), lambda i: (i % nsb, 0)),
            pl.BlockSpec((tm, LANES), lambda i: (i % nsb, 0)),
            full((1, LANES)),
            full((1, LANES)),
            full((LANES, LANES)),
        ],
        out_specs=[
            pl.BlockSpec((tm, D_MODEL), lambda i: (i, 0)),
            pl.BlockSpec((tm, kvw), lambda i: (i, 0)),
            pl.BlockSpec((tm, kvw), lambda i: (i, 0)),
        ],
        out_shape=[
            jax.ShapeDtypeStruct((T, D_MODEL), jnp.bfloat16),
            jax.ShapeDtypeStruct((T, kvw), jnp.bfloat16),
            jax.ShapeDtypeStruct((T, kvw), jnp.bfloat16),
        ],
        compiler_params=_cparams(("parallel",)),
        name="proj_c",
    )(x2, g, w, cos, sin, qg, kg, bd)


def _stack_heads(q2, is_a):
    zero = jnp.zeros_like(q2)
    return jnp.concatenate([jnp.where(is_a, q2, zero), jnp.where(is_a, zero, q2)], axis=0)


def _softmax_pv(s, vw):
    m = jnp.max(s, axis=-1, keepdims=True)
    p = jnp.exp(s - m)
    den = jnp.sum(p, axis=-1, keepdims=True)
    pv = jnp.dot(p.astype(vw.dtype), vw, preferred_element_type=jnp.float32)
    return pv / den, m + jnp.log(den)


A_HALF = 64
A_QB = 128
NPAIR = D_MODEL // LANES


def _band_pairs(load_q, load_k, load_v, valid, store_o, qb):
    is_a = _lane_is_first_head()
    lane = lax.broadcasted_iota(jnp.int32, (1, LANES), 1)
    lse_tile = jnp.zeros((qb, LANES), jnp.float32)
    for p in range(NPAIR):
        cols = slice(p * LANES, (p + 1) * LANES)
        qst = _stack_heads(load_q(cols), is_a)
        vw = load_v(cols)
        s = lax.dot_general(qst, load_k(cols), NT_DIMS, preferred_element_type=jnp.float32)
        s = jnp.where(valid, s, NEG_INF)
        o, lse = _softmax_pv(s, vw)
        store_o(cols, jnp.where(is_a, o[:qb], o[qb:]))
        lse_tile = jnp.where(lane == 2 * p, lse[:qb], jnp.where(lane == 2 * p + 1, lse[qb:], lse_tile))
    return lse_tile


def _attn_a_kernel(q_ref, k_ref, v_ref, o_ref, lse_ref, mask_ref, *, L, permuted):
    qb, W = A_QB, 2 * A_QB
    nq = L // qb

    def pos(l):
        return (l // 64) * 64 + 4 * (l % 16) + (l % 64) // 16 if permuted else l

    rel = (pos(lax.broadcasted_iota(jnp.int32, (2 * qb, W), 0) % qb)
           - pos(lax.broadcasted_iota(jnp.int32, (2 * qb, W), 1)))
    for case, delta in enumerate((0, A_HALF, W - qb)):
        mask_ref[case] = (jnp.abs(rel + delta) <= A_HALF).astype(jnp.int32)

    def body(iq, carry):
        qs = pl.multiple_of(iq * qb, qb)
        ks = pl.multiple_of(jnp.clip(qs - A_HALF, 0, L - W), A_HALF)
        case = jnp.where(iq == 0, 0, jnp.where(iq == nq - 1, 2, 1))
        valid = mask_ref[case] != 0

        def store_o(cols, o):
            o_ref[0, pl.ds(qs, qb), cols] = o.astype(o_ref.dtype)

        lse_ref[0, pl.ds(qs, qb), :] = _band_pairs(
            lambda cols: q_ref[0, pl.ds(qs, qb), cols],
            lambda cols: k_ref[0, pl.ds(ks, W), cols],
            lambda cols: v_ref[0, pl.ds(ks, W), cols],
            valid, store_o, qb)
        return carry

    lax.fori_loop(0, nq, body, 0)


def _attn_a01(qkv, *, B, S, L, permuted, name):
    runs = S // L
    spec = lambda j: pl.BlockSpec((1, L, D_MODEL), lambda b, r: (b, r, j))
    kern = functools.partial(_attn_a_kernel, L=L, permuted=permuted)
    return pl.pallas_call(
        kern,
        grid=(B, runs),
        in_specs=[spec(0), spec(1), spec(2)],
        out_specs=[pl.BlockSpec((1, L, D_MODEL), lambda b, r: (b, r, 0)),
                   pl.BlockSpec((1, L, LANES), lambda b, r: (b, r, 0))],
        out_shape=[jax.ShapeDtypeStruct((B, S, D_MODEL), jnp.bfloat16),
                   jax.ShapeDtypeStruct((B, S, LANES), jnp.float32)],
        scratch_shapes=[pltpu.VMEM((3, 2 * A_QB, 2 * A_QB), jnp.int32)],
        compiler_params=_cparams(("parallel", "parallel")),
        name=name,
    )(qkv, qkv, qkv)


def _attn_a2_kernel(q_ref, k_ref, v_ref, o_ref, lse_ref):
    ng = q_ref.shape[2]
    L = ng * 16
    row = lax.broadcasted_iota(jnp.int32, (2 * L, L), 0) % L
    col = lax.broadcasted_iota(jnp.int32, (2 * L, L), 1)
    valid = jnp.abs(row - col) <= A_HALF

    def body(a, carry):
        def load(ref):
            return lambda cols: ref[0, 0, :, a, :, cols].reshape(L, LANES)

        def store_o(cols, o):
            o_ref[0, 0, :, a, :, cols] = o.astype(o_ref.dtype).reshape(ng, 16, LANES)

        lse = _band_pairs(load(q_ref), load(k_ref), load(v_ref), valid, store_o, L)
        lse_ref[0, 0, :, a, :, :] = lse.reshape(ng, 16, LANES)
        return carry

    lax.fori_loop(0, 4, body, 0)


def _attn_a2(qkv, *, B, S):
    ng = S // PERM_GROUP
    view = qkv.reshape(B, 4, ng, 4, 16, qkv.shape[2])
    blk = lambda c: (1, 1, ng, 4, 16, c)
    spec = lambda j: pl.BlockSpec(blk(D_MODEL), lambda b, r: (b, r, 0, 0, 0, 3 + j))
    o, lse = pl.pallas_call(
        _attn_a2_kernel,
        grid=(B, 4),
        in_specs=[spec(0), spec(1), spec(2)],
        out_specs=[pl.BlockSpec(blk(D_MODEL), lambda b, r: (b, r, 0, 0, 0, 0)),
                   pl.BlockSpec(blk(LANES), lambda b, r: (b, r, 0, 0, 0, 0))],
        out_shape=[jax.ShapeDtypeStruct((B, 4, ng, 4, 16, D_MODEL), jnp.bfloat16),
                   jax.ShapeDtypeStruct((B, 4, ng, 4, 16, LANES), jnp.float32)],
        compiler_params=_cparams(("parallel", "parallel")),
        name="attn_a2",
    )(view, view, view)
    return o.reshape(B, S, D_MODEL), lse.reshape(B, S, LANES)


def _split_bf16(v, parts):
    out = []
    for _ in range(parts - 1):
        hi = v.astype(jnp.bfloat16)
        out.append(hi)
        v = v - hi.astype(jnp.float32)
    out.append(v.astype(jnp.bfloat16))
    return out


def _merge_out_kernel(x_ref, o0_ref, l0_ref, o1_ref, l1_ref, o2_ref, l2_ref, pm_ref, ee_ref, w_ref, y_ref):
    pm = pm_ref[...]
    ee = ee_ref[...]
    f32 = jnp.float32

    def natural(o_ref, l_ref, g):
        o = jnp.dot(pm, o_ref[0, :, g].reshape(PERM_GROUP, D_MODEL), preferred_element_type=f32)
        lse = l_ref[0, :, g].reshape(PERM_GROUP, LANES)
        lse = sum(jnp.dot(pm, part, preferred_element_type=f32) for part in _split_bf16(lse, 3))
        return o, lse

    for g in range(x_ref.shape[1] // PERM_GROUP):
        rows = slice(g * PERM_GROUP, (g + 1) * PERM_GROUP)
        o0 = o0_ref[0, rows, :].astype(f32)
        l0 = l0_ref[0, rows, :]
        o1, l1 = natural(o1_ref, l1_ref, g)
        o2, l2 = natural(o2_ref, l2_ref, g)
        m = jnp.maximum(jnp.maximum(l0, l1), l2)
        e0, e1, e2 = jnp.exp(l0 - m), jnp.exp(l1 - m), jnp.exp(l2 - m)
        den = e0 + e1 + e2

        def spread(e):
            return jnp.dot(jnp.concatenate(_split_bf16(e / den, 2), axis=1), ee, preferred_element_type=f32)

        a = (spread(e0) * o0 + spread(e1) * o1 + spread(e2) * o2).astype(jnp.bfloat16)
        y_ref[0, rows, :] = x_ref[0, rows, :] + jnp.dot(a, w_ref[...], preferred_element_type=f32)


def _merge_out(x3, o0, l0, o1, l1, o2, l2, w, *, groups=2):
    B, S, D = x3.shape
    ng = S // PERM_GROUP
    tm = groups * PERM_GROUP
    pview = lambda a: a.reshape(B, 4, ng, PERM_GROUP // 4, a.shape[2])
    nat = lambda c: pl.BlockSpec((1, tm, c), lambda b, j: (b, j, 0))
    per = lambda c: pl.BlockSpec((1, 4, groups, PERM_GROUP // 4, c), lambda b, j: (b, 0, j, 0, 0))
    full = lambda shape: pl.BlockSpec(shape, lambda b, j: (0, 0))

    pm = _group_perm_matrix()
    ee = np.zeros((2 * LANES, D_MODEL), np.float32)
    for h in range(N_HEADS):
        ee[h, h * HEAD_DIM:(h + 1) * HEAD_DIM] = 1.0
        ee[LANES + h, h * HEAD_DIM:(h + 1) * HEAD_DIM] = 1.0

    return pl.pallas_call(
        _merge_out_kernel,
        grid=(B, ng // groups),
        in_specs=[nat(D), nat(D), nat(LANES), per(D), per(LANES), per(D), per(LANES),
                  full((PERM_GROUP, PERM_GROUP)), full((2 * LANES, D)), full((D, D))],
        out_specs=nat(D),
        out_shape=jax.ShapeDtypeStruct((B, S, D), jnp.float32),
        compiler_params=_cparams(("parallel", "parallel")),
        name="merge_out",
    )(x3, o0, l0, pview(o1), pview(l1), pview(o2), pview(l2),
      jnp.asarray(pm, jnp.bfloat16), jnp.asarray(ee, jnp.bfloat16), w)


def _out_kernel(x_ref, a_ref, w_ref, y_ref):
    y_ref[...] = x_ref[...] + jnp.dot(a_ref[...], w_ref[...], preferred_element_type=jnp.float32)


def _out_proj(x2, a, w, *, tm=1024):
    T, D = x2.shape
    row = pl.BlockSpec((tm, D), lambda i: (i, 0))
    return pl.pallas_call(
        _out_kernel,
        grid=(T // tm,),
        in_specs=[row, row, pl.BlockSpec((D, D), lambda i: (0, 0))],
        out_specs=row,
        out_shape=jax.ShapeDtypeStruct((T, D), jnp.float32),
        compiler_params=_cparams(("parallel",)),
        name="out_proj",
    )(x2, a, w)


B_QROWS = 4
B_KROWS = 12
NA_KH = 8
NA_KW = 16


def _attn_b_kernel(q_ref, k_ref, v_ref, t_ref, o_ref, *, rows):
    is_a = _lane_is_first_head()
    nq = B_QROWS * GRID_W
    nk = B_KROWS * GRID_W
    units = rows // B_QROWS
    for u in range(units):
        qs = u * nq
        kr0 = min(max(u * B_QROWS - NA_KH // 2, 0), rows - B_KROWS)
        ks = kr0 * GRID_W
        geo = 0 if u == 0 else (2 if u == units - 1 else 1)
        qst = _stack_heads(q_ref[0, qs:qs + nq, :], is_a)
        kw = k_ref[0, ks:ks + nk, :]
        vw = v_ref[0, ks:ks + nk, :]
        s = lax.dot_general(qst, kw, NT_DIMS, preferred_element_type=jnp.float32)
        tbl = t_ref[:, geo].reshape(2 * nq, nk)
        s = jnp.where(tbl > 0.5 * NEG_INF, s + tbl, NEG_INF)
        o, _ = _softmax_pv(s, vw)
        o_ref[0, qs:qs + nq, :] = jnp.where(is_a, o[:nq], o[nq:]).astype(o_ref.dtype)


def _attn_b(qkv, table, *, B, S):
    rows = S // GRID_W
    npair = D_MODEL // LANES
    nq = B_QROWS * GRID_W
    nk = B_KROWS * GRID_W

    def in_spec(j):
        return pl.BlockSpec((1, S, LANES), lambda p, b: (b, 0, j * npair + p))

    kern = functools.partial(_attn_b_kernel, rows=rows)
    return pl.pallas_call(
        kern,
        grid=(npair, B),
        in_specs=[in_spec(0), in_spec(1), in_spec(2),
                  pl.BlockSpec((2, 3, nq, nk), lambda p, b: (p, 0, 0, 0))],
        out_specs=pl.BlockSpec((1, S, LANES), lambda p, b: (b, 0, p)),
        out_shape=jax.ShapeDtypeStruct((B, S, D_MODEL), jnp.bfloat16),
        compiler_params=_cparams(("parallel", "parallel")),
        name="attn_b",
    )(qkv, qkv, qkv, table)


def _b_bias_table(rpb, rows):
    H = rpb.shape[0]
    qc = np.arange(GRID_W)[:, None]
    kc = np.arange(GRID_W)[None, :]
    cs = np.clip(qc - NA_KW // 2, 0, GRID_W - NA_KW)
    col_ok = (kc >= cs) & (kc < cs + NA_KW)
    dc = np.clip(kc - qc + NA_KW - 1, 0, 2 * NA_KW - 2)
    onehot = ((dc[None] == np.arange(2 * NA_KW - 1)[:, None, None]) & col_ok[None]).astype(np.float32)
    cval = jnp.einsum("hac,cqk->haqk", rpb, jnp.asarray(onehot), precision=lax.Precision.HIGHEST)
    slabs = jnp.where(jnp.asarray(col_ok)[None, None], cval, NEG_INF)
    slabs = jnp.concatenate([slabs, jnp.full((H, 1, GRID_W, GRID_W), NEG_INF, jnp.float32)], axis=1)

    units = rows // B_QROWS
    idx = np.zeros((3, B_QROWS, B_KROWS), np.int32)
    for geo, u in enumerate((0, 1, units - 1)):
        kr0 = min(max(u * B_QROWS - NA_KH // 2, 0), rows - B_KROWS)
        for a in range(B_QROWS):
            qr = u * B_QROWS + a
            rs = min(max(qr - NA_KH // 2, 0), rows - NA_KH)
            for c in range(B_KROWS):
                kr = kr0 + c
                idx[geo, a, c] = kr - qr + NA_KH - 1 if rs <= kr < rs + NA_KH else 2 * NA_KH - 1
    t = jnp.take(slabs, jnp.asarray(idx.reshape(-1)), axis=1)
    t = t.reshape(H, 3, B_QROWS, B_KROWS, GRID_W, GRID_W).transpose(0, 1, 2, 4, 3, 5)
    return t.reshape(H, 3, B_QROWS * GRID_W, B_KROWS * GRID_W)


def _attn_c_kernel(q_ref, k_ref, v_ref, o_ref):
    is_a = _lane_is_first_head()
    tq = C_SUB_ROWS
    for r0 in range(0, q_ref.shape[1], tq):
        rows = slice(r0, r0 + tq)
        q = q_ref[0, rows, :]
        qst = jnp.concatenate([_stack_heads(q[:, :LANES], is_a), _stack_heads(q[:, LANES:], is_a)], axis=0)
        s = lax.dot_general(qst, k_ref[0], NT_DIMS, preferred_element_type=jnp.float32)
        o, _ = _softmax_pv(s, v_ref[0])
        o_ref[0, rows, :LANES] = jnp.where(is_a, o[:tq], o[tq:2 * tq]).astype(o_ref.dtype)
        o_ref[0, rows, LANES:] = jnp.where(is_a, o[2 * tq:3 * tq], o[3 * tq:]).astype(o_ref.dtype)


C_SUB_ROWS = 128


def _attn_c(q, k2, v2, *, B, S, tq=256):
    gw = 2 * LANES
    return pl.pallas_call(
        _attn_c_kernel,
        grid=(B, C_KV_HEADS, S // tq),
        in_specs=[
            pl.BlockSpec((1, tq, gw), lambda b, g, i: (b, i, g)),
            pl.BlockSpec((1, S, LANES), lambda b, g, i: (b, 0, g)),
            pl.BlockSpec((1, S, LANES), lambda b, g, i: (b, 0, g)),
        ],
        out_specs=pl.BlockSpec((1, tq, gw), lambda b, g, i: (b, i, g)),
        out_shape=jax.ShapeDtypeStruct((B, S, D_MODEL), jnp.bfloat16),
        compiler_params=_cparams(("parallel", "parallel", "parallel")),
        name="attn_c",
    )(q, k2, v2)


def _mlp_kernel(x_ref, g_ref, wu_ref, wd_ref, gf_ref, y_ref, h_ref, acc_ref, *, final_norm):
    f = pl.program_id(1)

    @pl.when(f == 0)
    def _():
        h_ref[...] = _rms(x_ref[...], g_ref[...]).astype(jnp.bfloat16)
        acc_ref[...] = jnp.zeros_like(acc_ref)

    u = jnp.dot(h_ref[...], wu_ref[...], preferred_element_type=jnp.float32)
    r = jnp.maximum(u, 0.0)
    acc_ref[...] += jnp.dot((r * r).astype(jnp.bfloat16), wd_ref[...], preferred_element_type=jnp.float32)

    @pl.when(f == pl.num_programs(1) - 1)
    def _():
        y = x_ref[...] + acc_ref[...]
        if final_norm:
            y = _rms(y, gf_ref[...])
        y_ref[...] = y


def _mlp(x2, g, wu, wd, gf, *, final_norm, tm=1024, tf=512):
    T, D = x2.shape
    F = wu.shape[1]
    kern = functools.partial(_mlp_kernel, final_norm=final_norm)
    return pl.pallas_call(
        kern,
        grid=(T // tm, F // tf),
        in_specs=[
            pl.BlockSpec((tm, D), lambda i, f: (i, 0)),
            pl.BlockSpec((1, D), lambda i, f: (0, 0)),
            pl.BlockSpec((D, tf), lambda i, f: (0, f)),
            pl.BlockSpec((tf, D), lambda i, f: (f, 0)),
            pl.BlockSpec((1, D), lambda i, f: (0, 0)),
        ],
        out_specs=pl.BlockSpec((tm, D), lambda i, f: (i, 0)),
        out_shape=jax.ShapeDtypeStruct((T, D), jnp.float32),
        scratch_shapes=[pltpu.VMEM((tm, D), jnp.bfloat16), pltpu.VMEM((tm, D), jnp.float32)],
        compiler_params=_cparams(("parallel", "arbitrary")),
        name="mlp",
    )(x2, g, wu, wd, gf)


def _rope_angles(pos, dim):
    inv = 1.0 / (ROPE_THETA ** (jnp.arange(0, dim, 2, dtype=jnp.float32) / dim))
    return pos.astype(jnp.float32)[:, None] * inv[None, :]


def _rope_tables(ang):
    cos, sin = jnp.cos(ang), jnp.sin(ang)
    return jnp.tile(cos, (1, 4)), jnp.tile(jnp.concatenate([-sin, sin], axis=-1), (1, 2))


def kernel(x, l0_attn_norm, l0_w_in, l0_w_out, l0_mlp_norm, l0_w_up, l0_w_down, l1_attn_norm, l1_w_in, l1_rpb, l1_w_out, l1_mlp_norm, l1_w_up, l1_w_down, l2_attn_norm, l2_w_in, l2_q_norm, l2_k_norm, l2_w_out, l2_mlp_norm, l2_w_up, l2_w_down, l3_attn_norm, l3_w_in, l3_w_out, l3_mlp_norm, l3_w_up, l3_w_down, final_norm):
    B, S, D = x.shape
    bf = lambda w: w.astype(jnp.bfloat16)
    row = lambda g: g.reshape(1, -1).astype(jnp.float32)

    t = jnp.arange(S, dtype=jnp.int32)
    cos_a, sin_a = _rope_tables(_rope_angles(t, HEAD_DIM))
    cos_c, sin_c = _rope_tables(jnp.concatenate(
        [_rope_angles(t // GRID_W, HALF), _rope_angles(t % GRID_W, HALF)], axis=-1))

    x2 = x.reshape(B * S, D)

    perm = _token_perm(S)
    cos_p, sin_p = cos_a[perm], sin_a[perm]

    def layer_a(x2, attn_norm, w_in, w_out):
        w = bf(w_in)
        gw = 3 * D_MODEL
        g = row(attn_norm)
        qkv0 = _proj(x2, g, w[:, :gw], cos_a, sin_a, rope=True).reshape(B, S, -1)
        qkv12 = _proj(x2, g, w[:, gw:], cos_p, sin_p, rope=True, perm=True, tm=S, tn=512).reshape(B, S, -1)
        o0, l0 = _attn_a01(qkv0, B=B, S=S, L=S, permuted=False, name="attn_a0")
        o1, l1 = _attn_a01(qkv12, B=B, S=S, L=S // 4, permuted=True, name="attn_a1")
        o2, l2 = _attn_a2(qkv12, B=B, S=S)
        return _merge_out(x2.reshape(B, S, D), o0, l0, o1, l1, o2, l2, bf(w_out)).reshape(B * S, D)

    def layer_b(x2, attn_norm, w_in, rpb, w_out):
        qkv = _proj(x2, row(attn_norm), bf(w_in), cos_a, sin_a, rope=False).reshape(B, S, -1)
        table = _b_bias_table(rpb.astype(jnp.float32), S // GRID_W)
        a = _attn_b(qkv, table, B=B, S=S)
        return _out_proj(x2, a.reshape(B * S, D), bf(w_out))

    def layer_c(x2, attn_norm, w_in, q_norm, k_norm, w_out):
        gain2 = lambda g: jnp.tile(g.astype(jnp.float32), 2).reshape(1, LANES)
        bd = jnp.kron(jnp.eye(2, dtype=jnp.float32),
                      jnp.full((HEAD_DIM, HEAD_DIM), 1.0 / HEAD_DIM, jnp.float32)).astype(jnp.bfloat16)
        q, k2, v2 = _proj_c(x2, row(attn_norm), bf(w_in), cos_c, sin_c, gain2(q_norm), gain2(k_norm), bd)
        a = _attn_c(q.reshape(B, S, -1), k2.reshape(B, S, -1), v2.reshape(B, S, -1), B=B, S=S)
        return _out_proj(x2, a.reshape(B * S, D), bf(w_out))

    fn = row(final_norm)
    x2 = layer_a(x2, l0_attn_norm, l0_w_in, l0_w_out)
    x2 = _mlp(x2, row(l0_mlp_norm), bf(l0_w_up), bf(l0_w_down), fn, final_norm=False)
    x2 = layer_b(x2, l1_attn_norm, l1_w_in, l1_rpb, l1_w_out)
    x2 = _mlp(x2, row(l1_mlp_norm), bf(l1_w_up), bf(l1_w_down), fn, final_norm=False)
    x2 = layer_c(x2, l2_attn_norm, l2_w_in, l2_q_norm, l2_k_norm, l2_w_out)
    x2 = _mlp(x2, row(l2_mlp_norm), bf(l2_w_up), bf(l2_w_down), fn, final_norm=False)
    x2 = layer_a(x2, l3_attn_norm, l3_w_in, l3_w_out)
    x2 = _mlp(x2, row(l3_mlp_norm), bf(l3_w_up), bf(l3_w_down), fn, final_norm=True)
    return x2.reshape(B, S, D)
```

```python
import functools

import jax
import jax.numpy as jnp
import numpy as np
from jax import lax
from jax.experimental import pallas as pl
from jax.experimental.pallas import tpu as pltpu

D_MODEL = 1024
HEAD_DIM = 64
N_HEADS = 16
D_FF = 4 * D_MODEL
ROPE_THETA = 10000.0
RMS_EPS = 1e-6
NEG_INF = -1e30
GRID_W = 64
A_GROUPS = ((128, 1), (512, 4), (2048, 16))
C_KV_HEADS = 4
LOG2E = 1.4426950408889634
LN2 = 0.6931471805599453
QK_SCALE = HEAD_DIM ** -0.5 * LOG2E

LANES = 128
HALF = HEAD_DIM // 2
VMEM_LIMIT = 56 * 1024 * 1024

NT_DIMS = (((1,), (1,)), ((), ()))


def _cparams(sem):
    return pltpu.CompilerParams(dimension_semantics=sem, vmem_limit_bytes=VMEM_LIMIT)


def _rms(x, g):
    ms = jnp.mean(x * x, axis=-1, keepdims=True)
    return (x * lax.rsqrt(ms + RMS_EPS)) * g


def _lane_is_first_head():
    return lax.broadcasted_iota(jnp.int32, (1, LANES), 1) < HEAD_DIM


def _rope128(y, cos, sin_signed):
    lane = lax.broadcasted_iota(jnp.int32, (1, LANES), 1)
    first_half = (lane % HEAD_DIM) < HALF
    partner = jnp.where(first_half, pltpu.roll(y, LANES - HALF, 1), pltpu.roll(y, HALF, 1))
    return y * cos + partner * sin_signed


PERM_GROUP = 256
PROJ_ROWS = 256


def _token_perm(S):
    t = np.arange(S).reshape(S // PERM_GROUP, 16, 4, 4)
    return t.transpose(3, 0, 2, 1).reshape(S)


def _group_perm_matrix():
    token = np.arange(PERM_GROUP).reshape(16, 4, 4).transpose(2, 1, 0).reshape(-1)
    pm = np.zeros((PERM_GROUP, PERM_GROUP), np.float32)
    pm[token, np.arange(PERM_GROUP)] = 1.0
    return pm


def _proj_kernel(x_ref, g_ref, w_ref, cos_ref, sin_ref, pm_ref, o_ref, h_ref, *, rope, perm):
    n = pl.program_id(1)
    tm = x_ref.shape[0]
    tn = o_ref.shape[1]

    @pl.when(n == 0)
    def _():
        g = g_ref[...]
        if perm:
            ng = tm // PERM_GROUP
            run = PERM_GROUP // 4
            for grp in range(ng):
                hn = _rms(x_ref[grp * PERM_GROUP:(grp + 1) * PERM_GROUP, :], g).astype(jnp.bfloat16)
                hp = jnp.dot(pm_ref[...], hn, preferred_element_type=jnp.float32).astype(jnp.bfloat16)
                for rho in range(4):
                    dst = (rho * ng + grp) * run
                    h_ref[dst:dst + run, :] = hp[rho * run:(rho + 1) * run, :]
        else:
            h_ref[...] = _rms(x_ref[...], g).astype(jnp.bfloat16)

    sec = (n * tn // D_MODEL) % 3
    scale = jnp.where(sec == 0, QK_SCALE, 1.0).astype(jnp.float32)

    def sub_blocks(epilogue):
        for r0 in range(0, tm, PROJ_ROWS):
            rows = slice(r0, r0 + PROJ_ROWS)
            y = jnp.dot(h_ref[rows, :], w_ref[...], preferred_element_type=jnp.float32)
            epilogue(rows, y)

    def plain(rows, y):
        o_ref[rows, :] = (y * scale).astype(o_ref.dtype)

    def roped(rows, y):
        cos = cos_ref[rows, :]
        sin = sin_ref[rows, :]
        for c in range(tn // LANES):
            sl = slice(c * LANES, (c + 1) * LANES)
            o_ref[rows, sl] = (_rope128(y[:, sl], cos, sin) * scale).astype(o_ref.dtype)

    if rope:
        @pl.when(sec == 2)
        def _():
            sub_blocks(plain)

        @pl.when(sec != 2)
        def _():
            sub_blocks(roped)
    else:
        sub_blocks(plain)


def _proj(x2, g, w, cos, sin, *, rope, perm=False, tm=1024, tn=1024):
    T, D = x2.shape
    N = w.shape[1]
    S = cos.shape[0]
    nsb = S // tm
    kern = functools.partial(_proj_kernel, rope=rope, perm=perm)
    return pl.pallas_call(
        kern,
        grid=(T // tm, N // tn),
        in_specs=[
            pl.BlockSpec((tm, D), lambda i, n: (i, 0)),
            pl.BlockSpec((1, D), lambda i, n: (0, 0)),
            pl.BlockSpec((D, tn), lambda i, n: (0, n)),
            pl.BlockSpec((tm, LANES), lambda i, n: (i % nsb, 0)),
            pl.BlockSpec((tm, LANES), lambda i, n: (i % nsb, 0)),
            pl.BlockSpec((PERM_GROUP, PERM_GROUP), lambda i, n: (0, 0)),
        ],
        out_specs=pl.BlockSpec((tm, tn), lambda i, n: (i, n)),
        out_shape=jax.ShapeDtypeStruct((T, N), jnp.bfloat16),
        scratch_shapes=[pltpu.VMEM((tm, D), jnp.bfloat16)],
        compiler_params=_cparams(("parallel", "arbitrary")),
        name="proj_perm" if perm else "proj",
    )(x2, g, w, cos, sin, jnp.asarray(_group_perm_matrix().T, jnp.bfloat16))


def _proj_c_kernel(x_ref, g_ref, w_ref, cos_ref, sin_ref, qg_ref, kg_ref, bd_ref,
                   q_ref, k_ref, v_ref):
    h = _rms(x_ref[...], g_ref[...]).astype(jnp.bfloat16)
    y = jnp.dot(h, w_ref[...], preferred_element_type=jnp.float32)
    cos = cos_ref[...]
    sin = sin_ref[...]
    bd = bd_ref[...]
    is_a = _lane_is_first_head()

    def head_norm(c, gain):
        yc = y[:, c * LANES:(c + 1) * LANES]
        sq = yc * yc
        hi = sq.astype(jnp.bfloat16)
        lo = (sq - hi.astype(jnp.float32)).astype(jnp.bfloat16)
        ms = (jnp.dot(hi, bd, preferred_element_type=jnp.float32)
              + jnp.dot(lo, bd, preferred_element_type=jnp.float32))
        return (yc * lax.rsqrt(ms + RMS_EPS)) * gain

    nq = D_MODEL // LANES
    for c in range(nq):
        qn = head_norm(c, qg_ref[...])
        q_ref[:, c * LANES:(c + 1) * LANES] = (_rope128(qn, cos, sin) * QK_SCALE).astype(q_ref.dtype)
    nkv = C_KV_HEADS * HEAD_DIM // LANES
    for c in range(nkv):
        kn = _rope128(head_norm(nq + c, kg_ref[...]), cos, sin)
        ksw = pltpu.roll(kn, HEAD_DIM, 1)
        k_ref[:, (2 * c) * LANES:(2 * c + 1) * LANES] = jnp.where(is_a, kn, ksw).astype(k_ref.dtype)
        k_ref[:, (2 * c + 1) * LANES:(2 * c + 2) * LANES] = jnp.where(is_a, ksw, kn).astype(k_ref.dtype)
        vc = y[:, (nq + nkv + c) * LANES:(nq + nkv + c + 1) * LANES]
        vsw = pltpu.roll(vc, HEAD_DIM, 1)
        v_ref[:, (2 * c) * LANES:(2 * c + 1) * LANES] = jnp.where(is_a, vc, vsw).astype(v_ref.dtype)
        v_ref[:, (2 * c + 1) * LANES:(2 * c + 2) * LANES] = jnp.where(is_a, vsw, vc).astype(v_ref.dtype)


def _proj_c(x2, g, w, cos, sin, qg, kg, bd, *, tm=512):
    T, D = x2.shape
    N = w.shape[1]
    S = cos.shape[0]
    nsb = S // tm
    kvw = 2 * C_KV_HEADS * HEAD_DIM
    full = lambda shape: pl.BlockSpec(shape, lambda i: (0, 0))
    return pl.pallas_call(
        _proj_c_kernel,
        grid=(T // tm,),
        in_specs=[
            pl.BlockSpec((tm, D), lambda i: (i, 0)),
            full((1, D)),
            full((D, N)),
            pl.BlockSpec((tm, LANES), lambda i: (i % nsb, 0)),
            pl.BlockSpec((tm, LANES), lambda i: (i % nsb, 0)),
            full((1, LANES)),
            full((1, LANES)),
            full((LANES, LANES)),
        ],
        out_specs=[
            pl.BlockSpec((tm, D_MODEL), lambda i: (i, 0)),
            pl.BlockSpec((tm, kvw), lambda i: (i, 0)),
            pl.BlockSpec((tm, kvw), lambda i: (i, 0)),
        ],
        out_shape=[
            jax.ShapeDtypeStruct((T, D_MODEL), jnp.bfloat16),
            jax.ShapeDtypeStruct((T, kvw), jnp.bfloat16),
            jax.ShapeDtypeStruct((T, kvw), jnp.bfloat16),
        ],
        compiler_params=_cparams(("parallel",)),
        name="proj_c",
    )(x2, g, w, cos, sin, qg, kg, bd)


def _stack_heads(q2, is_a):
    zero = jnp.zeros_like(q2)
    return jnp.concatenate([jnp.where(is_a, q2, zero), jnp.where(is_a, zero, q2)], axis=0)


def _softmax_pv(s, vw):
    m = jnp.max(s, axis=-1, keepdims=True)
    p = jnp.exp2(s - m)
    den = jnp.sum(p, axis=-1, keepdims=True)
    pv = jnp.dot(p.astype(vw.dtype), vw, preferred_element_type=jnp.float32)
    return pv, m, den


def _unstack(is_a, x, n):
    return jnp.where(is_a, x[:n], x[n:2 * n])


A_HALF = 64
A_QB = 128
NPAIR = D_MODEL // LANES


def _band_pairs(load_q, load_k, load_v, valid, store_o, qb):
    is_a = _lane_is_first_head()
    lane = lax.broadcasted_iota(jnp.int32, (1, LANES), 1)
    m_tile = jnp.zeros((qb, LANES), jnp.float32)
    den_tile = jnp.ones((qb, LANES), jnp.float32)
    for p in range(NPAIR):
        cols = slice(p * LANES, (p + 1) * LANES)
        qst = _stack_heads(load_q(cols), is_a)
        vw = load_v(cols)
        s = lax.dot_general(qst, load_k(cols), NT_DIMS, preferred_element_type=jnp.float32)
        s = jnp.where(valid, s, NEG_INF)
        pv, m, den = _softmax_pv(s, vw)
        store_o(cols, _unstack(is_a, pv, qb) / _unstack(is_a, den, qb))
        in_a, in_b = lane == 2 * p, lane == 2 * p + 1
        m_tile = jnp.where(in_a, m[:qb], jnp.where(in_b, m[qb:], m_tile))
        den_tile = jnp.where(in_a, den[:qb], jnp.where(in_b, den[qb:], den_tile))
    return m_tile * LN2 + jnp.log(den_tile)


def _attn_a_kernel(q_ref, k_ref, v_ref, o_ref, lse_ref, mask_ref, *, L, permuted):
    qb, W = A_QB, 2 * A_QB
    nq = L // qb

    def pos(l):
        return (l // 64) * 64 + 4 * (l % 16) + (l % 64) // 16 if permuted else l

    rel = (pos(lax.broadcasted_iota(jnp.int32, (2 * qb, W), 0) % qb)
           - pos(lax.broadcasted_iota(jnp.int32, (2 * qb, W), 1)))
    for case, delta in enumerate((0, A_HALF, W - qb)):
        mask_ref[case] = (jnp.abs(rel + delta) <= A_HALF).astype(jnp.int32)

    def body(iq, carry):
        qs = pl.multiple_of(iq * qb, qb)
        ks = pl.multiple_of(jnp.clip(qs - A_HALF, 0, L - W), A_HALF)
        case = jnp.where(iq == 0, 0, jnp.where(iq == nq - 1, 2, 1))
        valid = mask_ref[case] != 0

        def store_o(cols, o):
            o_ref[0, pl.ds(qs, qb), cols] = o.astype(o_ref.dtype)

        lse_ref[0, pl.ds(qs, qb), :] = _band_pairs(
            lambda cols: q_ref[0, pl.ds(qs, qb), cols],
            lambda cols: k_ref[0, pl.ds(ks, W), cols],
            lambda cols: v_ref[0, pl.ds(ks, W), cols],
            valid, store_o, qb)
        return carry

    lax.fori_loop(0, nq, body, 0)


def _attn_a01(qkv, *, B, S, L, permuted, name):
    runs = S // L
    spec = lambda j: pl.BlockSpec((1, L, D_MODEL), lambda b, r: (b, r, j))
    kern = functools.partial(_attn_a_kernel, L=L, permuted=permuted)
    return pl.pallas_call(
        kern,
        grid=(B, runs),
        in_specs=[spec(0), spec(1), spec(2)],
        out_specs=[pl.BlockSpec((1, L, D_MODEL), lambda b, r: (b, r, 0)),
                   pl.BlockSpec((1, L, LANES), lambda b, r: (b, r, 0))],
        out_shape=[jax.ShapeDtypeStruct((B, S, D_MODEL), jnp.bfloat16),
                   jax.ShapeDtypeStruct((B, S, LANES), jnp.float32)],
        scratch_shapes=[pltpu.VMEM((3, 2 * A_QB, 2 * A_QB), jnp.int32)],
        compiler_params=_cparams(("parallel", "parallel")),
        name=name,
    )(qkv, qkv, qkv)


def _attn_a2_kernel(q_ref, k_ref, v_ref, o_ref, lse_ref):
    ng = q_ref.shape[2]
    L = ng * 16
    row = lax.broadcasted_iota(jnp.int32, (2 * L, L), 0) % L
    col = lax.broadcasted_iota(jnp.int32, (2 * L, L), 1)
    valid = jnp.abs(row - col) <= A_HALF

    for a in range(4):
        def load(ref, a=a):
            return lambda cols: ref[0, 0, :, a, :, cols].reshape(L, LANES)

        def store_o(cols, o, a=a):
            o_ref[0, 0, :, a, :, cols] = o.astype(o_ref.dtype).reshape(ng, 16, LANES)

        lse = _band_pairs(load(q_ref), load(k_ref), load(v_ref), valid, store_o, L)
        lse_ref[0, 0, :, a, :, :] = lse.reshape(ng, 16, LANES)


def _attn_a2(qkv, *, B, S):
    ng = S // PERM_GROUP
    view = qkv.reshape(B, 4, ng, 4, 16, qkv.shape[2])
    blk = lambda c: (1, 1, ng, 4, 16, c)
    spec = lambda j: pl.BlockSpec(blk(D_MODEL), lambda b, r: (b, r, 0, 0, 0, 3 + j))
    o, lse = pl.pallas_call(
        _attn_a2_kernel,
        grid=(B, 4),
        in_specs=[spec(0), spec(1), spec(2)],
        out_specs=[pl.BlockSpec(blk(D_MODEL), lambda b, r: (b, r, 0, 0, 0, 0)),
                   pl.BlockSpec(blk(LANES), lambda b, r: (b, r, 0, 0, 0, 0))],
        out_shape=[jax.ShapeDtypeStruct((B, 4, ng, 4, 16, D_MODEL), jnp.bfloat16),
                   jax.ShapeDtypeStruct((B, 4, ng, 4, 16, LANES), jnp.float32)],
        compiler_params=_cparams(("parallel", "parallel")),
        name="attn_a2",
    )(view, view, view)
    return o.reshape(B, S, D_MODEL), lse.reshape(B, S, LANES)


def _split_bf16(v, parts):
    out = []
    for _ in range(parts - 1):
        hi = v.astype(jnp.bfloat16)
        out.append(hi)
        v = v - hi.astype(jnp.float32)
    out.append(v.astype(jnp.bfloat16))
    return out


def _merge_out_kernel(x_ref, o0_ref, l0_ref, o1_ref, l1_ref, o2_ref, l2_ref, pm_ref, ee_ref, w_ref, y_ref):
    pm = pm_ref[...]
    ee = ee_ref[...]
    f32 = jnp.float32

    def natural(o_ref, l_ref, g):
        o = jnp.dot(pm, o_ref[0, :, g].reshape(PERM_GROUP, D_MODEL), preferred_element_type=f32)
        lse = l_ref[0, :, g].reshape(PERM_GROUP, LANES)
        lse = sum(jnp.dot(pm, part, preferred_element_type=f32) for part in _split_bf16(lse, 3))
        return o, lse

    for g in range(x_ref.shape[1] // PERM_GROUP):
        rows = slice(g * PERM_GROUP, (g + 1) * PERM_GROUP)
        o0 = o0_ref[0, rows, :].astype(f32)
        l0 = l0_ref[0, rows, :]
        o1, l1 = natural(o1_ref, l1_ref, g)
        o2, l2 = natural(o2_ref, l2_ref, g)
        m = jnp.maximum(jnp.maximum(l0, l1), l2)
        e0, e1, e2 = jnp.exp(l0 - m), jnp.exp(l1 - m), jnp.exp(l2 - m)
        den = e0 + e1 + e2

        def spread(e):
            return jnp.dot(jnp.concatenate(_split_bf16(e / den, 2), axis=1), ee, preferred_element_type=f32)

        a = (spread(e0) * o0 + spread(e1) * o1 + spread(e2) * o2).astype(jnp.bfloat16)
        y_ref[0, rows, :] = x_ref[0, rows, :] + jnp.dot(a, w_ref[...], preferred_element_type=f32)


def _merge_out(x3, o0, l0, o1, l1, o2, l2, w, *, groups=2):
    B, S, D = x3.shape
    ng = S // PERM_GROUP
    tm = groups * PERM_GROUP
    pview = lambda a: a.reshape(B, 4, ng, PERM_GROUP // 4, a.shape[2])
    nat = lambda c: pl.BlockSpec((1, tm, c), lambda b, j: (b, j, 0))
    per = lambda c: pl.BlockSpec((1, 4, groups, PERM_GROUP // 4, c), lambda b, j: (b, 0, j, 0, 0))
    full = lambda shape: pl.BlockSpec(shape, lambda b, j: (0, 0))

    pm = _group_perm_matrix()
    ee = np.zeros((2 * LANES, D_MODEL), np.float32)
    for h in range(N_HEADS):
        ee[h, h * HEAD_DIM:(h + 1) * HEAD_DIM] = 1.0
        ee[LANES + h, h * HEAD_DIM:(h + 1) * HEAD_DIM] = 1.0

    return pl.pallas_call(
        _merge_out_kernel,
        grid=(B, ng // groups),
        in_specs=[nat(D), nat(D), nat(LANES), per(D), per(LANES), per(D), per(LANES),
                  full((PERM_GROUP, PERM_GROUP)), full((2 * LANES, D)), full((D, D))],
        out_specs=nat(D),
        out_shape=jax.ShapeDtypeStruct((B, S, D), jnp.float32),
        compiler_params=_cparams(("parallel", "parallel")),
        name="merge_out",
    )(x3, o0, l0, pview(o1), pview(l1), pview(o2), pview(l2),
      jnp.asarray(pm, jnp.bfloat16), jnp.asarray(ee, jnp.bfloat16), w)


def _out_kernel(x_ref, a_ref, w_ref, y_ref):
    y_ref[...] = x_ref[...] + jnp.dot(a_ref[...], w_ref[...], preferred_element_type=jnp.float32)


def _out_proj(x2, a, w, *, tm=1024):
    T, D = x2.shape
    row = pl.BlockSpec((tm, D), lambda i: (i, 0))
    return pl.pallas_call(
        _out_kernel,
        grid=(T // tm,),
        in_specs=[row, row, pl.BlockSpec((D, D), lambda i: (0, 0))],
        out_specs=row,
        out_shape=jax.ShapeDtypeStruct((T, D), jnp.float32),
        compiler_params=_cparams(("parallel",)),
        name="out_proj",
    )(x2, a, w)


B_QROWS = 4
B_KROWS = 12
NA_KH = 8
NA_KW = 16


def _attn_b_kernel(q_ref, k_ref, v_ref, t_ref, o_ref, *, rows):
    is_a = _lane_is_first_head()
    nq = B_QROWS * GRID_W
    nk = B_KROWS * GRID_W
    units = rows // B_QROWS
    for u in range(units):
        qs = u * nq
        kr0 = min(max(u * B_QROWS - NA_KH // 2, 0), rows - B_KROWS)
        ks = kr0 * GRID_W
        geo = 0 if u == 0 else (2 if u == units - 1 else 1)
        qst = _stack_heads(q_ref[0, qs:qs + nq, :], is_a)
        kw = k_ref[0, ks:ks + nk, :]
        vw = v_ref[0, ks:ks + nk, :]
        s = lax.dot_general(qst, kw, NT_DIMS, preferred_element_type=jnp.float32)
        tbl = t_ref[:, geo].reshape(2 * nq, nk)
        s = jnp.where(tbl > 0.5 * NEG_INF, s + tbl, NEG_INF)
        pv, _, den = _softmax_pv(s, vw)
        o_ref[0, qs:qs + nq, :] = (_unstack(is_a, pv, nq) / _unstack(is_a, den, nq)).astype(o_ref.dtype)


def _attn_b(qkv, table, *, B, S):
    rows = S // GRID_W
    npair = D_MODEL // LANES
    nq = B_QROWS * GRID_W
    nk = B_KROWS * GRID_W

    def in_spec(j):
        return pl.BlockSpec((1, S, LANES), lambda p, b: (b, 0, j * npair + p))

    kern = functools.partial(_attn_b_kernel, rows=rows)
    return pl.pallas_call(
        kern,
        grid=(npair, B),
        in_specs=[in_spec(0), in_spec(1), in_spec(2),
                  pl.BlockSpec((2, 3, nq, nk), lambda p, b: (p, 0, 0, 0))],
        out_specs=pl.BlockSpec((1, S, LANES), lambda p, b: (b, 0, p)),
        out_shape=jax.ShapeDtypeStruct((B, S, D_MODEL), jnp.bfloat16),
        compiler_params=_cparams(("parallel", "parallel")),
        name="attn_b",
    )(qkv, qkv, qkv, table)


def _b_bias_table(rpb, rows):
    H = rpb.shape[0]
    qc = np.arange(GRID_W)[:, None]
    kc = np.arange(GRID_W)[None, :]
    cs = np.clip(qc - NA_KW // 2, 0, GRID_W - NA_KW)
    col_ok = (kc >= cs) & (kc < cs + NA_KW)
    dc = np.clip(kc - qc + NA_KW - 1, 0, 2 * NA_KW - 2)
    onehot = ((dc[None] == np.arange(2 * NA_KW - 1)[:, None, None]) & col_ok[None]).astype(np.float32)
    cval = jnp.einsum("hac,cqk->haqk", rpb, jnp.asarray(onehot), precision=lax.Precision.HIGHEST)
    slabs = jnp.where(jnp.asarray(col_ok)[None, None], cval * LOG2E, NEG_INF)
    slabs = jnp.concatenate([slabs, jnp.full((H, 1, GRID_W, GRID_W), NEG_INF, jnp.float32)], axis=1)

    units = rows // B_QROWS
    idx = np.zeros((3, B_QROWS, B_KROWS), np.int32)
    for geo, u in enumerate((0, 1, units - 1)):
        kr0 = min(max(u * B_QROWS - NA_KH // 2, 0), rows - B_KROWS)
        for a in range(B_QROWS):
            qr = u * B_QROWS + a
            rs = min(max(qr - NA_KH // 2, 0), rows - NA_KH)
            for c in range(B_KROWS):
                kr = kr0 + c
                idx[geo, a, c] = kr - qr + NA_KH - 1 if rs <= kr < rs + NA_KH else 2 * NA_KH - 1
    t = jnp.take(slabs, jnp.asarray(idx.reshape(-1)), axis=1)
    t = t.reshape(H, 3, B_QROWS, B_KROWS, GRID_W, GRID_W).transpose(0, 1, 2, 4, 3, 5)
    return t.reshape(H, 3, B_QROWS * GRID_W, B_KROWS * GRID_W)


def _attn_c_kernel(q_ref, k_ref, v_ref, o_ref):
    is_a = _lane_is_first_head()
    tq = C_SUB_ROWS
    for r0 in range(0, q_ref.shape[1], tq):
        rows = slice(r0, r0 + tq)
        q = q_ref[0, rows, :]
        qst = jnp.concatenate([_stack_heads(q[:, :LANES], is_a), _stack_heads(q[:, LANES:], is_a)], axis=0)
        s = lax.dot_general(qst, k_ref[0], NT_DIMS, preferred_element_type=jnp.float32)
        pv, _, den = _softmax_pv(s, v_ref[0])
        o_ref[0, rows, :LANES] = (_unstack(is_a, pv, tq) / _unstack(is_a, den, tq)).astype(o_ref.dtype)
        o_ref[0, rows, LANES:] = (_unstack(is_a, pv[2 * tq:], tq) / _unstack(is_a, den[2 * tq:], tq)).astype(o_ref.dtype)


C_SUB_ROWS = 128


def _attn_c(q, k2, v2, *, B, S, tq=1024):
    gw = 2 * LANES
    return pl.pallas_call(
        _attn_c_kernel,
        grid=(B, C_KV_HEADS, S // tq),
        in_specs=[
            pl.BlockSpec((1, tq, gw), lambda b, g, i: (b, i, g)),
            pl.BlockSpec((1, S, LANES), lambda b, g, i: (b, 0, g)),
            pl.BlockSpec((1, S, LANES), lambda b, g, i: (b, 0, g)),
        ],
        out_specs=pl.BlockSpec((1, tq, gw), lambda b, g, i: (b, i, g)),
        out_shape=jax.ShapeDtypeStruct((B, S, D_MODEL), jnp.bfloat16),
        compiler_params=_cparams(("parallel", "parallel", "parallel")),
        name="attn_c",
    )(q, k2, v2)


def _mlp_kernel(x_ref, g_ref, wu_ref, wd_ref, gf_ref, y_ref, h_ref, acc_ref, *, final_norm):
    f = pl.program_id(1)

    @pl.when(f == 0)
    def _():
        h_ref[...] = _rms(x_ref[...], g_ref[...]).astype(jnp.bfloat16)
        acc_ref[...] = jnp.zeros_like(acc_ref)

    u = jnp.dot(h_ref[...], wu_ref[...], preferred_element_type=jnp.float32)
    r = jnp.maximum(u, 0.0)
    acc_ref[...] += jnp.dot((r * r).astype(jnp.bfloat16), wd_ref[...], preferred_element_type=jnp.float32)

    @pl.when(f == pl.num_programs(1) - 1)
    def _():
        y = x_ref[...] + acc_ref[...]
        if final_norm:
            y = _rms(y, gf_ref[...])
        y_ref[...] = y


def _mlp(x2, g, wu, wd, gf, *, final_norm, tm=1024, tf=1024):
    T, D = x2.shape
    F = wu.shape[1]
    kern = functools.partial(_mlp_kernel, final_norm=final_norm)
    return pl.pallas_call(
        kern,
        grid=(T // tm, F // tf),
        in_specs=[
            pl.BlockSpec((tm, D), lambda i, f: (i, 0)),
            pl.BlockSpec((1, D), lambda i, f: (0, 0)),
            pl.BlockSpec((D, tf), lambda i, f: (0, f)),
            pl.BlockSpec((tf, D), lambda i, f: (f, 0)),
            pl.BlockSpec((1, D), lambda i, f: (0, 0)),
        ],
        out_specs=pl.BlockSpec((tm, D), lambda i, f: (i, 0)),
        out_shape=jax.ShapeDtypeStruct((T, D), jnp.float32),
        scratch_shapes=[pltpu.VMEM((tm, D), jnp.bfloat16), pltpu.VMEM((tm, D), jnp.float32)],
        compiler_params=_cparams(("parallel", "arbitrary")),
        name="mlp",
    )(x2, g, wu, wd, gf)


def _rope_angles(pos, dim):
    inv = 1.0 / (ROPE_THETA ** (jnp.arange(0, dim, 2, dtype=jnp.float32) / dim))
    return pos.astype(jnp.float32)[:, None] * inv[None, :]


def _rope_tables(ang):
    cos, sin = jnp.cos(ang), jnp.sin(ang)
    return jnp.tile(cos, (1, 4)), jnp.tile(jnp.concatenate([-sin, sin], axis=-1), (1, 2))


def kernel(x, l0_attn_norm, l0_w_in, l0_w_out, l0_mlp_norm, l0_w_up, l0_w_down, l1_attn_norm, l1_w_in, l1_rpb, l1_w_out, l1_mlp_norm, l1_w_up, l1_w_down, l2_attn_norm, l2_w_in, l2_q_norm, l2_k_norm, l2_w_out, l2_mlp_norm, l2_w_up, l2_w_down, l3_attn_norm, l3_w_in, l3_w_out, l3_mlp_norm, l3_w_up, l3_w_down, final_norm):
    B, S, D = x.shape
    bf = lambda w: w.astype(jnp.bfloat16)
    row = lambda g: g.reshape(1, -1).astype(jnp.float32)

    t = jnp.arange(S, dtype=jnp.int32)
    cos_a, sin_a = _rope_tables(_rope_angles(t, HEAD_DIM))
    cos_c, sin_c = _rope_tables(jnp.concatenate(
        [_rope_angles(t // GRID_W, HALF), _rope_angles(t % GRID_W, HALF)], axis=-1))

    x2 = x.reshape(B * S, D)

    perm = _token_perm(S)
    cos_p, sin_p = cos_a[perm], sin_a[perm]

    def layer_a(x2, attn_norm, w_in, w_out):
        w = bf(w_in)
        gw = 3 * D_MODEL
        g = row(attn_norm)
        qkv0 = _proj(x2, g, w[:, :gw], cos_a, sin_a, rope=True).reshape(B, S, -1)
        qkv12 = _proj(x2, g, w[:, gw:], cos_p, sin_p, rope=True, perm=True, tm=S, tn=1024).reshape(B, S, -1)
        o0, l0 = _attn_a01(qkv0, B=B, S=S, L=S, permuted=False, name="attn_a0")
        o1, l1 = _attn_a01(qkv12, B=B, S=S, L=S // 4, permuted=True, name="attn_a1")
        o2, l2 = _attn_a2(qkv12, B=B, S=S)
        return _merge_out(x2.reshape(B, S, D), o0, l0, o1, l1, o2, l2, bf(w_out)).reshape(B * S, D)

    def layer_b(x2, attn_norm, w_in, rpb, w_out):
        qkv = _proj(x2, row(attn_norm), bf(w_in), cos_a, sin_a, rope=False).reshape(B, S, -1)
        table = _b_bias_table(rpb.astype(jnp.float32), S // GRID_W)
        a = _attn_b(qkv, table, B=B, S=S)
        return _out_proj(x2, a.reshape(B * S, D), bf(w_out))

    def layer_c(x2, attn_norm, w_in, q_norm, k_norm, w_out):
        gain2 = lambda g: jnp.tile(g.astype(jnp.float32), 2).reshape(1, LANES)
        bd = jnp.kron(jnp.eye(2, dtype=jnp.float32),
                      jnp.full((HEAD_DIM, HEAD_DIM), 1.0 / HEAD_DIM, jnp.float32)).astype(jnp.bfloat16)
        q, k2, v2 = _proj_c(x2, row(attn_norm), bf(w_in), cos_c, sin_c, gain2(q_norm), gain2(k_norm), bd)
        a = _attn_c(q.reshape(B, S, -1), k2.reshape(B, S, -1), v2.reshape(B, S, -1), B=B, S=S)
        return _out_proj(x2, a.reshape(B * S, D), bf(w_out))

    fn = row(final_norm)
    x2 = layer_a(x2, l0_attn_norm, l0_w_in, l0_w_out)
    x2 = _mlp(x2, row(l0_mlp_norm), bf(l0_w_up), bf(l0_w_down), fn, final_norm=False)
    x2 = layer_b(x2, l1_attn_norm, l1_w_in, l1_rpb, l1_w_out)
    x2 = _mlp(x2, row(l1_mlp_norm), bf(l1_w_up), bf(l1_w_down), fn, final_norm=False)
    x2 = layer_c(x2, l2_attn_norm, l2_w_in, l2_q_norm, l2_k_norm, l2_w_out)
    x2 = _mlp(x2, row(l2_mlp_norm), bf(l2_w_up), bf(l2_w_down), fn, final_norm=False)
    x2 = layer_a(x2, l3_attn_norm, l3_w_in, l3_w_out)
    x2 = _mlp(x2, row(l3_mlp_norm), bf(l3_w_up), bf(l3_w_down), fn, final_norm=True)
    return x2.reshape(B, S, D)
```

```python
import functools

import jax
import jax.numpy as jnp
import numpy as np
from jax import lax
from jax.experimental import pallas as pl
from jax.experimental.pallas import tpu as pltpu

D_MODEL = 1024
HEAD_DIM = 64
N_HEADS = 16
D_FF = 4 * D_MODEL
ROPE_THETA = 10000.0
RMS_EPS = 1e-6
NEG_INF = -1e30
GRID_W = 64
A_GROUPS = ((128, 1), (512, 4), (2048, 16))
C_KV_HEADS = 4
LOG2E = 1.4426950408889634
LN2 = 0.6931471805599453
QK_SCALE = HEAD_DIM ** -0.5 * LOG2E

LANES = 128
HALF = HEAD_DIM // 2
VMEM_LIMIT = 56 * 1024 * 1024

NT_DIMS = (((1,), (1,)), ((), ()))


def _cparams(sem):
    return pltpu.CompilerParams(dimension_semantics=sem, vmem_limit_bytes=VMEM_LIMIT)


def _rms(x, g):
    ms = jnp.mean(x * x, axis=-1, keepdims=True)
    return (x * lax.rsqrt(ms + RMS_EPS)) * g


def _lane_is_first_head():
    return lax.broadcasted_iota(jnp.int32, (1, LANES), 1) < HEAD_DIM


def _rope128(y, cos, sin_signed):
    lane = lax.broadcasted_iota(jnp.int32, (1, LANES), 1)
    first_half = (lane % HEAD_DIM) < HALF
    partner = jnp.where(first_half, pltpu.roll(y, LANES - HALF, 1), pltpu.roll(y, HALF, 1))
    return y * cos + partner * sin_signed


PERM_GROUP = 256
PROJ_ROWS = 256


def _token_perm(S):
    t = np.arange(S).reshape(S // PERM_GROUP, 16, 4, 4)
    return t.transpose(3, 0, 2, 1).reshape(S)


def _group_perm_matrix():
    token = np.arange(PERM_GROUP).reshape(16, 4, 4).transpose(2, 1, 0).reshape(-1)
    pm = np.zeros((PERM_GROUP, PERM_GROUP), np.float32)
    pm[token, np.arange(PERM_GROUP)] = 1.0
    return pm


def _proj_kernel(x_ref, g_ref, w_ref, cos_ref, sin_ref, pm_ref, o_ref, h_ref, *, rope, perm):
    n = pl.program_id(1)
    tm = x_ref.shape[0]
    tn = o_ref.shape[1]

    @pl.when(n == 0)
    def _():
        g = g_ref[...]
        if perm:
            ng = tm // PERM_GROUP
            run = PERM_GROUP // 4
            for grp in range(ng):
                hn = _rms(x_ref[grp * PERM_GROUP:(grp + 1) * PERM_GROUP, :], g).astype(jnp.bfloat16)
                hp = jnp.dot(pm_ref[...], hn, preferred_element_type=jnp.float32).astype(jnp.bfloat16)
                for rho in range(4):
                    dst = (rho * ng + grp) * run
                    h_ref[dst:dst + run, :] = hp[rho * run:(rho + 1) * run, :]
        else:
            h_ref[...] = _rms(x_ref[...], g).astype(jnp.bfloat16)

    sec = (n * tn // D_MODEL) % 3
    scale = jnp.where(sec == 0, QK_SCALE, 1.0).astype(jnp.float32)

    def sub_blocks(epilogue):
        for r0 in range(0, tm, PROJ_ROWS):
            rows = slice(r0, r0 + PROJ_ROWS)
            y = jnp.dot(h_ref[rows, :], w_ref[...], preferred_element_type=jnp.float32)
            epilogue(rows, y)

    def plain(rows, y):
        o_ref[rows, :] = (y * scale).astype(o_ref.dtype)

    def roped(rows, y):
        cos = cos_ref[rows, :]
        sin = sin_ref[rows, :]
        for c in range(tn // LANES):
            sl = slice(c * LANES, (c + 1) * LANES)
            o_ref[rows, sl] = (_rope128(y[:, sl], cos, sin) * scale).astype(o_ref.dtype)

    if rope:
        @pl.when(sec == 2)
        def _():
            sub_blocks(plain)

        @pl.when(sec != 2)
        def _():
            sub_blocks(roped)
    else:
        sub_blocks(plain)


def _proj(x2, g, w, cos, sin, *, rope, perm=False, col0=0, ncols=None, tm=2048, tn=1024):
    T, D = x2.shape
    N = w.shape[1] - col0 if ncols is None else ncols
    S = cos.shape[0]
    nsb = S // tm
    cb0 = col0 // tn
    kern = functools.partial(_proj_kernel, rope=rope, perm=perm)
    return pl.pallas_call(
        kern,
        grid=(T // tm, N // tn),
        in_specs=[
            pl.BlockSpec((tm, D), lambda i, n: (i, 0)),
            pl.BlockSpec((1, D), lambda i, n: (0, 0)),
            pl.BlockSpec((D, tn), lambda i, n: (0, cb0 + n)),
            pl.BlockSpec((tm, LANES), lambda i, n: (i % nsb, 0)),
            pl.BlockSpec((tm, LANES), lambda i, n: (i % nsb, 0)),
            pl.BlockSpec((PERM_GROUP, PERM_GROUP), lambda i, n: (0, 0)),
        ],
        out_specs=pl.BlockSpec((tm, tn), lambda i, n: (i, n)),
        out_shape=jax.ShapeDtypeStruct((T, N), jnp.bfloat16),
        scratch_shapes=[pltpu.VMEM((tm, D), jnp.bfloat16)],
        compiler_params=_cparams(("parallel", "arbitrary")),
        name="proj_perm" if perm else "proj",
    )(x2, g, w, cos, sin, jnp.asarray(_group_perm_matrix().T, jnp.bfloat16))


def _proj_c_kernel(x_ref, g_ref, w_ref, cos_ref, sin_ref, qg_ref, kg_ref, bd_ref,
                   q_ref, k_ref, v_ref):
    h = _rms(x_ref[...], g_ref[...]).astype(jnp.bfloat16)
    y = jnp.dot(h, w_ref[...], preferred_element_type=jnp.float32)
    cos = cos_ref[...]
    sin = sin_ref[...]
    bd = bd_ref[...]
    is_a = _lane_is_first_head()

    def head_norm(c, gain):
        yc = y[:, c * LANES:(c + 1) * LANES]
        sq = yc * yc
        hi = sq.astype(jnp.bfloat16)
        lo = (sq - hi.astype(jnp.float32)).astype(jnp.bfloat16)
        ms = (jnp.dot(hi, bd, preferred_element_type=jnp.float32)
              + jnp.dot(lo, bd, preferred_element_type=jnp.float32))
        return (yc * lax.rsqrt(ms + RMS_EPS)) * gain

    nq = D_MODEL // LANES
    for c in range(nq):
        qn = head_norm(c, qg_ref[...])
        q_ref[:, c * LANES:(c + 1) * LANES] = (_rope128(qn, cos, sin) * QK_SCALE).astype(q_ref.dtype)
    nkv = C_KV_HEADS * HEAD_DIM // LANES
    for c in range(nkv):
        kn = _rope128(head_norm(nq + c, kg_ref[...]), cos, sin)
        ksw = pltpu.roll(kn, HEAD_DIM, 1)
        k_ref[:, (2 * c) * LANES:(2 * c + 1) * LANES] = jnp.where(is_a, kn, ksw).astype(k_ref.dtype)
        k_ref[:, (2 * c + 1) * LANES:(2 * c + 2) * LANES] = jnp.where(is_a, ksw, kn).astype(k_ref.dtype)
        vc = y[:, (nq + nkv + c) * LANES:(nq + nkv + c + 1) * LANES]
        vsw = pltpu.roll(vc, HEAD_DIM, 1)
        v_ref[:, (2 * c) * LANES:(2 * c + 1) * LANES] = jnp.where(is_a, vc, vsw).astype(v_ref.dtype)
        v_ref[:, (2 * c + 1) * LANES:(2 * c + 2) * LANES] = jnp.where(is_a, vsw, vc).astype(v_ref.dtype)


def _proj_c(x2, g, w, cos, sin, qg, kg, bd, *, tm=512):
    T, D = x2.shape
    N = w.shape[1]
    S = cos.shape[0]
    nsb = S // tm
    kvw = 2 * C_KV_HEADS * HEAD_DIM
    full = lambda shape: pl.BlockSpec(shape, lambda i: (0, 0))
    return pl.pallas_call(
        _proj_c_kernel,
        grid=(T // tm,),
        in_specs=[
            pl.BlockSpec((tm, D), lambda i: (i, 0)),
            full((1, D)),
            full((D, N)),
            pl.BlockSpec((tm, LANES), lambda i: (i % nsb, 0)),
            pl.BlockSpec((tm, LANES), lambda i: (i % nsb, 0)),
            full((1, LANES)),
            full((1, LANES)),
            full((LANES, LANES)),
        ],
        out_specs=[
            pl.BlockSpec((tm, D_MODEL), lambda i: (i, 0)),
            pl.BlockSpec((tm, kvw), lambda i: (i, 0)),
            pl.BlockSpec((tm, kvw), lambda i: (i, 0)),
        ],
        out_shape=[
            jax.ShapeDtypeStruct((T, D_MODEL), jnp.bfloat16),
            jax.ShapeDtypeStruct((T, kvw), jnp.bfloat16),
            jax.ShapeDtypeStruct((T, kvw), jnp.bfloat16),
        ],
        compiler_params=_cparams(("parallel",)),
        name="proj_c",
    )(x2, g, w, cos, sin, qg, kg, bd)


def _stack_heads(q2, is_a):
    zero = jnp.zeros_like(q2)
    return jnp.concatenate([jnp.where(is_a, q2, zero), jnp.where(is_a, zero, q2)], axis=0)


def _softmax_pv(s, vw, mxu_sums=True):
    m = jnp.max(s, axis=-1, keepdims=True)
    if not mxu_sums:
        p = jnp.exp2(s - m)
        den = jnp.sum(p, axis=-1, keepdims=True)
        return jnp.dot(p.astype(vw.dtype), vw, preferred_element_type=jnp.float32), m, den
    p = jnp.exp2((s - m).astype(jnp.bfloat16))
    vaug = jnp.concatenate([vw, jnp.ones_like(vw)], axis=1)
    r = jnp.dot(p, vaug, preferred_element_type=jnp.float32)
    return r[:, :LANES], m, r[:, LANES:]


def _unstack(is_a, x, n):
    return jnp.where(is_a, x[:n], x[n:2 * n])


A_HALF = 64
A_QB = 128
NPAIR = D_MODEL // LANES


def _band_pairs(load_q, load_k, load_v, valid, store_o, qb, mxu_sums=True):
    is_a = _lane_is_first_head()
    lane = lax.broadcasted_iota(jnp.int32, (1, LANES), 1)
    m_tile = jnp.zeros((qb, LANES), jnp.float32)
    den_tile = jnp.ones((qb, LANES), jnp.float32)
    for p in range(NPAIR):
        cols = slice(p * LANES, (p + 1) * LANES)
        qst = _stack_heads(load_q(cols), is_a)
        vw = load_v(cols)
        s = lax.dot_general(qst, load_k(cols), NT_DIMS, preferred_element_type=jnp.float32)
        s = jnp.where(valid, s, NEG_INF)
        pv, m, den = _softmax_pv(s, vw, mxu_sums)
        store_o(cols, _unstack(is_a, pv, qb) / _unstack(is_a, den, qb))
        in_a, in_b = lane == 2 * p, lane == 2 * p + 1
        m_tile = jnp.where(in_a, m[:qb], jnp.where(in_b, m[qb:], m_tile))
        den_tile = jnp.where(in_a, den[:qb], jnp.where(in_b, den[qb:], den_tile))
    return m_tile * LN2 + jnp.log(den_tile)


def _attn_a_kernel(q_ref, k_ref, v_ref, o_ref, lse_ref, mask_ref, *, L, permuted):
    qb, W = A_QB, 2 * A_QB
    nq = L // qb

    def pos(l):
        return (l // 64) * 64 + 4 * (l % 16) + (l % 64) // 16 if permuted else l

    rel = (pos(lax.broadcasted_iota(jnp.int32, (2 * qb, W), 0) % qb)
           - pos(lax.broadcasted_iota(jnp.int32, (2 * qb, W), 1)))
    for case, delta in enumerate((0, A_HALF, W - qb)):
        mask_ref[case] = (jnp.abs(rel + delta) <= A_HALF).astype(jnp.int32)

    def body(iq, carry):
        qs = pl.multiple_of(iq * qb, qb)
        ks = pl.multiple_of(jnp.clip(qs - A_HALF, 0, L - W), A_HALF)
        case = jnp.where(iq == 0, 0, jnp.where(iq == nq - 1, 2, 1))
        valid = mask_ref[case] != 0

        def store_o(cols, o):
            o_ref[0, pl.ds(qs, qb), cols] = o.astype(o_ref.dtype)

        lse_ref[0, pl.ds(qs, qb), :] = _band_pairs(
            lambda cols: q_ref[0, pl.ds(qs, qb), cols],
            lambda cols: k_ref[0, pl.ds(ks, W), cols],
            lambda cols: v_ref[0, pl.ds(ks, W), cols],
            valid, store_o, qb)
        return carry

    lax.fori_loop(0, nq, body, 0)


def _attn_a01(qkv, *, B, S, L, permuted, name):
    runs = S // L
    spec = lambda j: pl.BlockSpec((1, L, D_MODEL), lambda b, r: (b, r, j))
    kern = functools.partial(_attn_a_kernel, L=L, permuted=permuted)
    return pl.pallas_call(
        kern,
        grid=(B, runs),
        in_specs=[spec(0), spec(1), spec(2)],
        out_specs=[pl.BlockSpec((1, L, D_MODEL), lambda b, r: (b, r, 0)),
                   pl.BlockSpec((1, L, LANES), lambda b, r: (b, r, 0))],
        out_shape=[jax.ShapeDtypeStruct((B, S, D_MODEL), jnp.bfloat16),
                   jax.ShapeDtypeStruct((B, S, LANES), jnp.float32)],
        scratch_shapes=[pltpu.VMEM((3, 2 * A_QB, 2 * A_QB), jnp.int32)],
        compiler_params=_cparams(("parallel", "parallel")),
        name=name,
    )(qkv, qkv, qkv)


def _attn_a2_kernel(q_ref, k_ref, v_ref, o_ref, lse_ref):
    ng = q_ref.shape[2]
    L = ng * 16
    row = lax.broadcasted_iota(jnp.int32, (2 * L, L), 0) % L
    col = lax.broadcasted_iota(jnp.int32, (2 * L, L), 1)
    valid = jnp.abs(row - col) <= A_HALF

    for a in range(4):
        def load(ref, a=a):
            return lambda cols: ref[0, 0, :, a, :, cols].reshape(L, LANES)

        def store_o(cols, o, a=a):
            o_ref[0, 0, :, a, :, cols] = o.astype(o_ref.dtype).reshape(ng, 16, LANES)

        lse = _band_pairs(load(q_ref), load(k_ref), load(v_ref), valid, store_o, L, mxu_sums=False)
        lse_ref[0, 0, :, a, :, :] = lse.reshape(ng, 16, LANES)


def _attn_a2(qkv, *, B, S):
    ng = S // PERM_GROUP
    view = qkv.reshape(B, 4, ng, 4, 16, qkv.shape[2])
    blk = lambda c: (1, 1, ng, 4, 16, c)
    spec = lambda j: pl.BlockSpec(blk(D_MODEL), lambda b, r: (b, r, 0, 0, 0, 3 + j))
    o, lse = pl.pallas_call(
        _attn_a2_kernel,
        grid=(B, 4),
        in_specs=[spec(0), spec(1), spec(2)],
        out_specs=[pl.BlockSpec(blk(D_MODEL), lambda b, r: (b, r, 0, 0, 0, 0)),
                   pl.BlockSpec(blk(LANES), lambda b, r: (b, r, 0, 0, 0, 0))],
        out_shape=[jax.ShapeDtypeStruct((B, 4, ng, 4, 16, D_MODEL), jnp.bfloat16),
                   jax.ShapeDtypeStruct((B, 4, ng, 4, 16, LANES), jnp.float32)],
        compiler_params=_cparams(("parallel", "parallel")),
        name="attn_a2",
    )(view, view, view)
    return o.reshape(B, S, D_MODEL), lse.reshape(B, S, LANES)


def _split_bf16(v, parts):
    out = []
    for _ in range(parts - 1):
        hi = v.astype(jnp.bfloat16)
        out.append(hi)
        v = v - hi.astype(jnp.float32)
    out.append(v.astype(jnp.bfloat16))
    return out


def _merge_out_kernel(x_ref, o0_ref, l0_ref, o1_ref, l1_ref, o2_ref, l2_ref, pm_ref, ee_ref, w_ref, y_ref):
    pm = pm_ref[...]
    ee = ee_ref[...]
    f32 = jnp.float32

    def natural(o_ref, l_ref, g):
        o = jnp.dot(pm, o_ref[0, :, g].reshape(PERM_GROUP, D_MODEL), preferred_element_type=f32)
        lse = l_ref[0, :, g].reshape(PERM_GROUP, LANES)
        lse = sum(jnp.dot(pm, part, preferred_element_type=f32) for part in _split_bf16(lse, 3))
        return o, lse

    for g in range(x_ref.shape[1] // PERM_GROUP):
        rows = slice(g * PERM_GROUP, (g + 1) * PERM_GROUP)
        o0 = o0_ref[0, rows, :].astype(f32)
        l0 = l0_ref[0, rows, :]
        o1, l1 = natural(o1_ref, l1_ref, g)
        o2, l2 = natural(o2_ref, l2_ref, g)
        m = jnp.maximum(jnp.maximum(l0, l1), l2)
        e0, e1, e2 = jnp.exp(l0 - m), jnp.exp(l1 - m), jnp.exp(l2 - m)
        den = e0 + e1 + e2

        def spread(e):
            return jnp.dot(jnp.concatenate(_split_bf16(e / den, 2), axis=1), ee, preferred_element_type=f32)

        a = (spread(e0) * o0 + spread(e1) * o1 + spread(e2) * o2).astype(jnp.bfloat16)
        y_ref[0, rows, :] = x_ref[0, rows, :] + jnp.dot(a, w_ref[...], preferred_element_type=f32)


def _merge_out(x3, o0, l0, o1, l1, o2, l2, w, *, groups=2):
    B, S, D = x3.shape
    ng = S // PERM_GROUP
    tm = groups * PERM_GROUP
    pview = lambda a: a.reshape(B, 4, ng, PERM_GROUP // 4, a.shape[2])
    nat = lambda c: pl.BlockSpec((1, tm, c), lambda b, j: (b, j, 0))
    per = lambda c: pl.BlockSpec((1, 4, groups, PERM_GROUP // 4, c), lambda b, j: (b, 0, j, 0, 0))
    full = lambda shape: pl.BlockSpec(shape, lambda b, j: (0, 0))

    pm = _group_perm_matrix()
    ee = np.zeros((2 * LANES, D_MODEL), np.float32)
    for h in range(N_HEADS):
        ee[h, h * HEAD_DIM:(h + 1) * HEAD_DIM] = 1.0
        ee[LANES + h, h * HEAD_DIM:(h + 1) * HEAD_DIM] = 1.0

    return pl.pallas_call(
        _merge_out_kernel,
        grid=(B, ng // groups),
        in_specs=[nat(D), nat(D), nat(LANES), per(D), per(LANES), per(D), per(LANES),
                  full((PERM_GROUP, PERM_GROUP)), full((2 * LANES, D)), full((D, D))],
        out_specs=nat(D),
        out_shape=jax.ShapeDtypeStruct((B, S, D), jnp.float32),
        compiler_params=_cparams(("parallel", "parallel")),
        name="merge_out",
    )(x3, o0, l0, pview(o1), pview(l1), pview(o2), pview(l2),
      jnp.asarray(pm, jnp.bfloat16), jnp.asarray(ee, jnp.bfloat16), w)


def _out_kernel(x_ref, a_ref, w_ref, y_ref):
    y_ref[...] = x_ref[...] + jnp.dot(a_ref[...], w_ref[...], preferred_element_type=jnp.float32)


def _out_proj(x2, a, w, *, tm=1024):
    T, D = x2.shape
    row = pl.BlockSpec((tm, D), lambda i: (i, 0))
    return pl.pallas_call(
        _out_kernel,
        grid=(T // tm,),
        in_specs=[row, row, pl.BlockSpec((D, D), lambda i: (0, 0))],
        out_specs=row,
        out_shape=jax.ShapeDtypeStruct((T, D), jnp.float32),
        compiler_params=_cparams(("parallel",)),
        name="out_proj",
    )(x2, a, w)


B_QROWS = 4
B_KROWS = 12
NA_KH = 8
NA_KW = 16


def _b_slab_index(rows):
    units = rows // B_QROWS
    idx = np.zeros((3, B_QROWS, B_KROWS), np.int32)
    for geo, u in enumerate((0, 1, units - 1)):
        kr0 = min(max(u * B_QROWS - NA_KH // 2, 0), rows - B_KROWS)
        for a in range(B_QROWS):
            qr = u * B_QROWS + a
            rs = min(max(qr - NA_KH // 2, 0), rows - NA_KH)
            for c in range(B_KROWS):
                kr = kr0 + c
                idx[geo, a, c] = kr - qr + NA_KH - 1 if rs <= kr < rs + NA_KH else 2 * NA_KH - 1
    return idx


def _attn_b_kernel(q_ref, k_ref, v_ref, slab_ref, o_ref, tbl_ref, *, rows):
    is_a = _lane_is_first_head()
    nq = B_QROWS * GRID_W
    nk = B_KROWS * GRID_W
    units = rows // B_QROWS
    idx = _b_slab_index(rows)

    @pl.when(pl.program_id(1) == 0)
    def _():
        for h in range(2):
            for geo in range(3):
                for qr in range(B_QROWS):
                    for j in range(B_KROWS // 2):
                        even = slab_ref[h, int(idx[geo, qr, 2 * j])]
                        odd = slab_ref[h, int(idx[geo, qr, 2 * j + 1])]
                        tbl_ref[h, geo, qr * GRID_W:(qr + 1) * GRID_W, j * LANES:(j + 1) * LANES] = (
                            jnp.where(is_a, even, odd))

    for u in range(units):
        qs = u * nq
        kr0 = min(max(u * B_QROWS - NA_KH // 2, 0), rows - B_KROWS)
        ks = kr0 * GRID_W
        geo = 0 if u == 0 else (2 if u == units - 1 else 1)
        qst = _stack_heads(q_ref[0, qs:qs + nq, :], is_a)
        kw = k_ref[0, ks:ks + nk, :]
        vw = v_ref[0, ks:ks + nk, :]
        s = lax.dot_general(qst, kw, NT_DIMS, preferred_element_type=jnp.float32)
        tbl = tbl_ref[:, geo].reshape(2 * nq, nk)
        s = jnp.where(tbl > 0.5 * NEG_INF, s + tbl, NEG_INF)
        pv, _, den = _softmax_pv(s, vw)
        o_ref[0, qs:qs + nq, :] = (_unstack(is_a, pv, nq) / _unstack(is_a, den, nq)).astype(o_ref.dtype)


def _attn_b(qkv, slabs, *, B, S):
    rows = S // GRID_W
    npair = D_MODEL // LANES
    nq = B_QROWS * GRID_W
    nk = B_KROWS * GRID_W

    def in_spec(j):
        return pl.BlockSpec((1, S, LANES), lambda p, b: (b, 0, j * npair + p))

    kern = functools.partial(_attn_b_kernel, rows=rows)
    return pl.pallas_call(
        kern,
        grid=(npair, B),
        in_specs=[in_spec(0), in_spec(1), in_spec(2),
                  pl.BlockSpec((2, 2 * NA_KH, GRID_W, LANES), lambda p, b: (p, 0, 0, 0))],
        out_specs=pl.BlockSpec((1, S, LANES), lambda p, b: (b, 0, p)),
        out_shape=jax.ShapeDtypeStruct((B, S, D_MODEL), jnp.bfloat16),
        scratch_shapes=[pltpu.VMEM((2, 3, nq, nk), jnp.float32)],
        compiler_params=_cparams(("parallel", "arbitrary")),
        name="attn_b",
    )(qkv, qkv, qkv, slabs)


def _b_bias_slabs(rpb):
    H = rpb.shape[0]
    qc = np.arange(GRID_W)[:, None]
    kc = np.arange(GRID_W)[None, :]
    cs = np.clip(qc - NA_KW // 2, 0, GRID_W - NA_KW)
    col_ok = (kc >= cs) & (kc < cs + NA_KW)
    dc = np.clip(kc - qc + NA_KW - 1, 0, 2 * NA_KW - 2)
    onehot = ((dc[None] == np.arange(2 * NA_KW - 1)[:, None, None]) & col_ok[None]).astype(np.float32)
    cval = jnp.einsum("hac,cqk->haqk", rpb, jnp.asarray(onehot), precision=lax.Precision.HIGHEST)
    slabs = jnp.where(jnp.asarray(col_ok)[None, None], cval * LOG2E, NEG_INF)
    slabs = jnp.concatenate([slabs, jnp.full((H, 1, GRID_W, GRID_W), NEG_INF, jnp.float32)], axis=1)
    return jnp.concatenate([slabs, slabs], axis=-1)


def _attn_c_kernel(q_ref, k_ref, v_ref, o_ref):
    is_a = _lane_is_first_head()
    tq = C_SUB_ROWS
    for r0 in range(0, q_ref.shape[1], tq):
        rows = slice(r0, r0 + tq)
        q = q_ref[0, rows, :]
        qst = jnp.concatenate([_stack_heads(q[:, :LANES], is_a), _stack_heads(q[:, LANES:], is_a)], axis=0)
        s = lax.dot_general(qst, k_ref[0], NT_DIMS, preferred_element_type=jnp.float32)
        pv, _, den = _softmax_pv(s, v_ref[0])
        o_ref[0, rows, :LANES] = (_unstack(is_a, pv, tq) / _unstack(is_a, den, tq)).astype(o_ref.dtype)
        o_ref[0, rows, LANES:] = (_unstack(is_a, pv[2 * tq:], tq) / _unstack(is_a, den[2 * tq:], tq)).astype(o_ref.dtype)


C_SUB_ROWS = 128


def _attn_c(q, k2, v2, *, B, S, tq=1024):
    gw = 2 * LANES
    return pl.pallas_call(
        _attn_c_kernel,
        grid=(B, C_KV_HEADS, S // tq),
        in_specs=[
            pl.BlockSpec((1, tq, gw), lambda b, g, i: (b, i, g)),
            pl.BlockSpec((1, S, LANES), lambda b, g, i: (b, 0, g)),
            pl.BlockSpec((1, S, LANES), lambda b, g, i: (b, 0, g)),
        ],
        out_specs=pl.BlockSpec((1, tq, gw), lambda b, g, i: (b, i, g)),
        out_shape=jax.ShapeDtypeStruct((B, S, D_MODEL), jnp.bfloat16),
        compiler_params=_cparams(("parallel", "parallel", "parallel")),
        name="attn_c",
    )(q, k2, v2)


def _mlp_kernel(x_ref, g_ref, wu_ref, wd_ref, gf_ref, y_ref, h_ref, acc_ref, *, final_norm):
    f = pl.program_id(1)

    @pl.when(f == 0)
    def _():
        h_ref[...] = _rms(x_ref[...], g_ref[...]).astype(jnp.bfloat16)
        acc_ref[...] = jnp.zeros_like(acc_ref)

    u = jnp.dot(h_ref[...], wu_ref[...], preferred_element_type=jnp.float32)
    r = jnp.maximum(u, 0.0)
    acc_ref[...] += jnp.dot((r * r).astype(jnp.bfloat16), wd_ref[...], preferred_element_type=jnp.float32)

    @pl.when(f == pl.num_programs(1) - 1)
    def _():
        y = x_ref[...] + acc_ref[...]
        if final_norm:
            y = _rms(y, gf_ref[...])
        y_ref[...] = y


def _mlp(x2, g, wu, wd, gf, *, final_norm, tm=1024, tf=1024):
    T, D = x2.shape
    F = wu.shape[1]
    kern = functools.partial(_mlp_kernel, final_norm=final_norm)
    return pl.pallas_call(
        kern,
        grid=(T // tm, F // tf),
        in_specs=[
            pl.BlockSpec((tm, D), lambda i, f: (i, 0)),
            pl.BlockSpec((1, D), lambda i, f: (0, 0)),
            pl.BlockSpec((D, tf), lambda i, f: (0, f)),
            pl.BlockSpec((tf, D), lambda i, f: (f, 0)),
            pl.BlockSpec((1, D), lambda i, f: (0, 0)),
        ],
        out_specs=pl.BlockSpec((tm, D), lambda i, f: (i, 0)),
        out_shape=jax.ShapeDtypeStruct((T, D), jnp.float32),
        scratch_shapes=[pltpu.VMEM((tm, D), jnp.bfloat16), pltpu.VMEM((tm, D), jnp.float32)],
        compiler_params=_cparams(("parallel", "arbitrary")),
        name="mlp",
    )(x2, g, wu, wd, gf)


def _rope_angles(pos, dim):
    inv = 1.0 / (ROPE_THETA ** (jnp.arange(0, dim, 2, dtype=jnp.float32) / dim))
    return pos.astype(jnp.float32)[:, None] * inv[None, :]


def _rope_tables(ang):
    cos, sin = jnp.cos(ang), jnp.sin(ang)
    return jnp.tile(cos, (1, 4)), jnp.tile(jnp.concatenate([-sin, sin], axis=-1), (1, 2))


def kernel(x, l0_attn_norm, l0_w_in, l0_w_out, l0_mlp_norm, l0_w_up, l0_w_down, l1_attn_norm, l1_w_in, l1_rpb, l1_w_out, l1_mlp_norm, l1_w_up, l1_w_down, l2_attn_norm, l2_w_in, l2_q_norm, l2_k_norm, l2_w_out, l2_mlp_norm, l2_w_up, l2_w_down, l3_attn_norm, l3_w_in, l3_w_out, l3_mlp_norm, l3_w_up, l3_w_down, final_norm):
    B, S, D = x.shape
    bf = lambda w: w.astype(jnp.bfloat16)
    row = lambda g: g.reshape(1, -1).astype(jnp.float32)

    t = jnp.arange(S, dtype=jnp.int32)
    cos_a, sin_a = _rope_tables(_rope_angles(t, HEAD_DIM))
    cos_c, sin_c = _rope_tables(jnp.concatenate(
        [_rope_angles(t // GRID_W, HALF), _rope_angles(t % GRID_W, HALF)], axis=-1))

    x2 = x.reshape(B * S, D)

    perm = _token_perm(S)
    cos_p, sin_p = cos_a[perm], sin_a[perm]

    def layer_a(x2, attn_norm, w_in, w_out):
        w = bf(w_in)
        gw = 3 * D_MODEL
        g = row(attn_norm)
        qkv0 = _proj(x2, g, w, cos_a, sin_a, rope=True, ncols=gw).reshape(B, S, -1)
        qkv12 = _proj(x2, g, w, cos_p, sin_p, rope=True, perm=True, col0=gw, tm=S).reshape(B, S, -1)
        o0, l0 = _attn_a01(qkv0, B=B, S=S, L=S, permuted=False, name="attn_a0")
        o1, l1 = _attn_a01(qkv12, B=B, S=S, L=S // 4, permuted=True, name="attn_a1")
        o2, l2 = _attn_a2(qkv12, B=B, S=S)
        return _merge_out(x2.reshape(B, S, D), o0, l0, o1, l1, o2, l2, bf(w_out)).reshape(B * S, D)

    def layer_b(x2, attn_norm, w_in, rpb, w_out):
        qkv = _proj(x2, row(attn_norm), bf(w_in), cos_a, sin_a, rope=False).reshape(B, S, -1)
        a = _attn_b(qkv, _b_bias_slabs(rpb.astype(jnp.float32)), B=B, S=S)
        return _out_proj(x2, a.reshape(B * S, D), bf(w_out))

    def layer_c(x2, attn_norm, w_in, q_norm, k_norm, w_out):
        gain2 = lambda g: jnp.tile(g.astype(jnp.float32), 2).reshape(1, LANES)
        bd = jnp.kron(jnp.eye(2, dtype=jnp.float32),
                      jnp.full((HEAD_DIM, HEAD_DIM), 1.0 / HEAD_DIM, jnp.float32)).astype(jnp.bfloat16)
        q, k2, v2 = _proj_c(x2, row(attn_norm), bf(w_in), cos_c, sin_c, gain2(q_norm), gain2(k_norm), bd)
        a = _attn_c(q.reshape(B, S, -1), k2.reshape(B, S, -1), v2.reshape(B, S, -1), B=B, S=S)
        return _out_proj(x2, a.reshape(B * S, D), bf(w_out))

    fn = row(final_norm)
    x2 = layer_a(x2, l0_attn_norm, l0_w_in, l0_w_out)
    x2 = _mlp(x2, row(l0_mlp_norm), bf(l0_w_up), bf(l0_w_down), fn, final_norm=False)
    x2 = layer_b(x2, l1_attn_norm, l1_w_in, l1_rpb, l1_w_out)
    x2 = _mlp(x2, row(l1_mlp_norm), bf(l1_w_up), bf(l1_w_down), fn, final_norm=False)
    x2 = layer_c(x2, l2_attn_norm, l2_w_in, l2_q_norm, l2_k_norm, l2_w_out)
    x2 = _mlp(x2, row(l2_mlp_norm), bf(l2_w_up), bf(l2_w_down), fn, final_norm=False)
    x2 = layer_a(x2, l3_attn_norm, l3_w_in, l3_w_out)
    x2 = _mlp(x2, row(l3_mlp_norm), bf(l3_w_up), bf(l3_w_down), fn, final_norm=True)
    return x2.reshape(B, S, D)
```

```python
import functools

import jax
import jax.numpy as jnp
import numpy as np
from jax import lax
from jax.experimental import pallas as pl
from jax.experimental.pallas import tpu as pltpu

D_MODEL = 1024
HEAD_DIM = 64
N_HEADS = 16
D_FF = 4 * D_MODEL
ROPE_THETA = 10000.0
RMS_EPS = 1e-6
NEG_INF = -1e30
GRID_W = 64
A_GROUPS = ((128, 1), (512, 4), (2048, 16))
C_KV_HEADS = 4
LOG2E = 1.4426950408889634
LN2 = 0.6931471805599453
QK_SCALE = HEAD_DIM ** -0.5 * LOG2E

LANES = 128
HALF = HEAD_DIM // 2
VMEM_LIMIT = 56 * 1024 * 1024

NT_DIMS = (((1,), (1,)), ((), ()))


def _cparams(sem):
    return pltpu.CompilerParams(dimension_semantics=sem, vmem_limit_bytes=VMEM_LIMIT)


def _rms(x, g):
    ms = jnp.mean(x * x, axis=-1, keepdims=True)
    return (x * lax.rsqrt(ms + RMS_EPS)) * g


def _lane_is_first_head():
    return lax.broadcasted_iota(jnp.int32, (1, LANES), 1) < HEAD_DIM


def _rope128(y, cos, sin_signed):
    lane = lax.broadcasted_iota(jnp.int32, (1, LANES), 1)
    first_half = (lane % HEAD_DIM) < HALF
    partner = jnp.where(first_half, pltpu.roll(y, LANES - HALF, 1), pltpu.roll(y, HALF, 1))
    return y * cos + partner * sin_signed


PERM_GROUP = 256
PROJ_ROWS = 256


def _token_perm(S):
    t = np.arange(S).reshape(S // PERM_GROUP, 16, 4, 4)
    return t.transpose(3, 0, 2, 1).reshape(S)


def _group_perm_matrix():
    token = np.arange(PERM_GROUP).reshape(16, 4, 4).transpose(2, 1, 0).reshape(-1)
    pm = np.zeros((PERM_GROUP, PERM_GROUP), np.float32)
    pm[token, np.arange(PERM_GROUP)] = 1.0
    return pm


def _proj_kernel(x_ref, g_ref, w_ref, cos_ref, sin_ref, pm_ref, o_ref, h_ref, *, rope, perm):
    n = pl.program_id(1)
    tm = x_ref.shape[0]
    tn = o_ref.shape[1]

    def normalise():
        g = g_ref[...]
        if perm:
            ng = tm // PERM_GROUP
            run = PERM_GROUP // 4
            for grp in range(ng):
                hn = _rms(x_ref[grp * PERM_GROUP:(grp + 1) * PERM_GROUP, :], g).astype(jnp.bfloat16)
                hp = jnp.dot(pm_ref[...], hn, preferred_element_type=jnp.float32).astype(jnp.bfloat16)
                for rho in range(4):
                    dst = (rho * ng + grp) * run
                    h_ref[dst:dst + run, :] = hp[rho * run:(rho + 1) * run, :]
        else:
            for r0 in range(0, tm, PROJ_ROWS):
                h_ref[r0:r0 + PROJ_ROWS, :] = _rms(x_ref[r0:r0 + PROJ_ROWS, :], g).astype(jnp.bfloat16)

    sec = (n * tn // D_MODEL) % 3

    def sub_blocks(epilogue, scale):
        w = w_ref[...].astype(jnp.bfloat16)
        for r0 in range(0, tm, PROJ_ROWS):
            rows = slice(r0, r0 + PROJ_ROWS)
            y = jnp.dot(h_ref[rows, :], w, preferred_element_type=jnp.float32)
            epilogue(rows, y, scale)

    def plain(rows, y, scale):
        o_ref[rows, :] = (y * scale).astype(o_ref.dtype)

    def roped(rows, y, scale):
        cos = cos_ref[rows, :]
        sin = sin_ref[rows, :]
        for c in range(tn // LANES):
            sl = slice(c * LANES, (c + 1) * LANES)
            o_ref[rows, sl] = (_rope128(y[:, sl], cos, sin) * scale).astype(o_ref.dtype)

    qk = roped if rope else plain

    @pl.when(n == 0)
    def _():
        normalise()
        sub_blocks(qk, QK_SCALE)

    @pl.when((n != 0) & (sec == 0))
    def _():
        sub_blocks(qk, QK_SCALE)

    @pl.when(sec == 1)
    def _():
        sub_blocks(qk, 1.0)

    @pl.when(sec == 2)
    def _():
        sub_blocks(plain, 1.0)


def _proj(x2, g, w, cos, sin, *, rope, perm=False, col0=0, ncols=None, tm=2048, tn=1024):
    T, D = x2.shape
    N = w.shape[1] - col0 if ncols is None else ncols
    S = cos.shape[0]
    nsb = S // tm
    cb0 = col0 // tn
    kern = functools.partial(_proj_kernel, rope=rope, perm=perm)
    return pl.pallas_call(
        kern,
        grid=(T // tm, N // tn),
        in_specs=[
            pl.BlockSpec((tm, D), lambda i, n: (i, 0)),
            pl.BlockSpec((1, D), lambda i, n: (0, 0)),
            pl.BlockSpec((D, tn), lambda i, n: (0, cb0 + n)),
            pl.BlockSpec((tm, LANES), lambda i, n: (i % nsb, 0)),
            pl.BlockSpec((tm, LANES), lambda i, n: (i % nsb, 0)),
            pl.BlockSpec((PERM_GROUP, PERM_GROUP), lambda i, n: (0, 0)),
        ],
        out_specs=pl.BlockSpec((tm, tn), lambda i, n: (i, n)),
        out_shape=jax.ShapeDtypeStruct((T, N), jnp.bfloat16),
        scratch_shapes=[pltpu.VMEM((tm, D), jnp.bfloat16)],
        compiler_params=_cparams(("parallel", "arbitrary")),
        name="proj_perm" if perm else "proj",
    )(x2, g, w, cos, sin, jnp.asarray(_group_perm_matrix().T, jnp.bfloat16))


def _proj_c_kernel(x_ref, g_ref, w_ref, cos_ref, sin_ref, qg_ref, kg_ref, bd_ref,
                   q_ref, k_ref, v_ref):
    bd = bd_ref[...]
    is_a = _lane_is_first_head()
    nq = D_MODEL // LANES
    nkv = C_KV_HEADS * HEAD_DIM // LANES

    for r0 in range(0, x_ref.shape[0], PROJ_ROWS):
        rows = slice(r0, r0 + PROJ_ROWS)
        h = _rms(x_ref[rows, :], g_ref[...]).astype(jnp.bfloat16)
        y = jnp.dot(h, w_ref[...], preferred_element_type=jnp.float32)
        cos = cos_ref[rows, :]
        sin = sin_ref[rows, :]

        def head_norm(c, gain):
            yc = y[:, c * LANES:(c + 1) * LANES]
            sq = yc * yc
            hi = sq.astype(jnp.bfloat16)
            lo = (sq - hi.astype(jnp.float32)).astype(jnp.bfloat16)
            ms = (jnp.dot(hi, bd, preferred_element_type=jnp.float32)
                  + jnp.dot(lo, bd, preferred_element_type=jnp.float32))
            return (yc * lax.rsqrt(ms + RMS_EPS)) * gain

        for c in range(nq):
            qn = head_norm(c, qg_ref[...])
            q_ref[rows, c * LANES:(c + 1) * LANES] = (_rope128(qn, cos, sin) * QK_SCALE).astype(q_ref.dtype)
        for c in range(nkv):
            kn = _rope128(head_norm(nq + c, kg_ref[...]), cos, sin)
            ksw = pltpu.roll(kn, HEAD_DIM, 1)
            k_ref[rows, (2 * c) * LANES:(2 * c + 1) * LANES] = jnp.where(is_a, kn, ksw).astype(k_ref.dtype)
            k_ref[rows, (2 * c + 1) * LANES:(2 * c + 2) * LANES] = jnp.where(is_a, ksw, kn).astype(k_ref.dtype)
            vc = y[:, (nq + nkv + c) * LANES:(nq + nkv + c + 1) * LANES]
            vsw = pltpu.roll(vc, HEAD_DIM, 1)
            v_ref[rows, (2 * c) * LANES:(2 * c + 1) * LANES] = jnp.where(is_a, vc, vsw).astype(v_ref.dtype)
            v_ref[rows, (2 * c + 1) * LANES:(2 * c + 2) * LANES] = jnp.where(is_a, vsw, vc).astype(v_ref.dtype)


def _proj_c(x2, g, w, cos, sin, qg, kg, bd, *, tm=1024):
    T, D = x2.shape
    N = w.shape[1]
    S = cos.shape[0]
    nsb = S // tm
    kvw = 2 * C_KV_HEADS * HEAD_DIM
    full = lambda shape: pl.BlockSpec(shape, lambda i: (0, 0))
    return pl.pallas_call(
        _proj_c_kernel,
        grid=(T // tm,),
        in_specs=[
            pl.BlockSpec((tm, D), lambda i: (i, 0)),
            full((1, D)),
            full((D, N)),
            pl.BlockSpec((tm, LANES), lambda i: (i % nsb, 0)),
            pl.BlockSpec((tm, LANES), lambda i: (i % nsb, 0)),
            full((1, LANES)),
            full((1, LANES)),
            full((LANES, LANES)),
        ],
        out_specs=[
            pl.BlockSpec((tm, D_MODEL), lambda i: (i, 0)),
            pl.BlockSpec((tm, kvw), lambda i: (i, 0)),
            pl.BlockSpec((tm, kvw), lambda i: (i, 0)),
        ],
        out_shape=[
            jax.ShapeDtypeStruct((T, D_MODEL), jnp.bfloat16),
            jax.ShapeDtypeStruct((T, kvw), jnp.bfloat16),
            jax.ShapeDtypeStruct((T, kvw), jnp.bfloat16),
        ],
        compiler_params=_cparams(("parallel",)),
        name="proj_c",
    )(x2, g, w, cos, sin, qg, kg, bd)


def _stack_heads(q2, is_a):
    zero = jnp.zeros_like(q2)
    return jnp.concatenate([jnp.where(is_a, q2, zero), jnp.where(is_a, zero, q2)], axis=0)


def _softmax_pv(s, vw, mxu_sums=True):
    m = jnp.max(s, axis=-1, keepdims=True)
    if not mxu_sums:
        p = jnp.exp2(s - m)
        den = jnp.sum(p, axis=-1, keepdims=True)
        return jnp.dot(p.astype(vw.dtype), vw, preferred_element_type=jnp.float32), m, den
    p = jnp.exp2((s - m).astype(jnp.bfloat16))
    vaug = jnp.concatenate([vw, jnp.ones_like(vw)], axis=1)
    r = jnp.dot(p, vaug, preferred_element_type=jnp.float32)
    return r[:, :LANES], m, r[:, LANES:]


def _unstack(is_a, x, n):
    return jnp.where(is_a, x[:n], x[n:2 * n])


A_HALF = 64
A_QB = 128
NPAIR = D_MODEL // LANES


def _band_pairs(load_q, load_k, load_v, valid, store_o, qb, mxu_sums=True):
    is_a = _lane_is_first_head()
    lane = lax.broadcasted_iota(jnp.int32, (1, LANES), 1)
    m_tile = jnp.zeros((qb, LANES), jnp.float32)
    den_tile = jnp.ones((qb, LANES), jnp.float32)
    for p in range(NPAIR):
        cols = slice(p * LANES, (p + 1) * LANES)
        qst = _stack_heads(load_q(cols), is_a)
        vw = load_v(cols)
        s = lax.dot_general(qst, load_k(cols), NT_DIMS, preferred_element_type=jnp.float32)
        s = jnp.where(valid, s, NEG_INF)
        pv, m, den = _softmax_pv(s, vw, mxu_sums)
        store_o(cols, _unstack(is_a, pv, qb) / _unstack(is_a, den, qb))
        in_a, in_b = lane == 2 * p, lane == 2 * p + 1
        m_tile = jnp.where(in_a, m[:qb], jnp.where(in_b, m[qb:], m_tile))
        den_tile = jnp.where(in_a, den[:qb], jnp.where(in_b, den[qb:], den_tile))
    return m_tile * LN2 + jnp.log(den_tile)


def _attn_a_kernel(q_ref, k_ref, v_ref, o_ref, lse_ref, mask_ref, *, L, permuted):
    qb, W = A_QB, 2 * A_QB
    nq = L // qb

    def pos(l):
        return (l // 64) * 64 + 4 * (l % 16) + (l % 64) // 16 if permuted else l

    rel = (pos(lax.broadcasted_iota(jnp.int32, (2 * qb, W), 0) % qb)
           - pos(lax.broadcasted_iota(jnp.int32, (2 * qb, W), 1)))
    for case, delta in enumerate((0, A_HALF, W - qb)):
        mask_ref[case] = (jnp.abs(rel + delta) <= A_HALF).astype(jnp.int32)

    def body(iq, carry):
        qs = pl.multiple_of(iq * qb, qb)
        ks = pl.multiple_of(jnp.clip(qs - A_HALF, 0, L - W), A_HALF)
        case = jnp.where(iq == 0, 0, jnp.where(iq == nq - 1, 2, 1))
        valid = mask_ref[case] != 0

        def store_o(cols, o):
            o_ref[0, pl.ds(qs, qb), cols] = o.astype(o_ref.dtype)

        lse_ref[0, pl.ds(qs, qb), :] = _band_pairs(
            lambda cols: q_ref[0, pl.ds(qs, qb), cols],
            lambda cols: k_ref[0, pl.ds(ks, W), cols],
            lambda cols: v_ref[0, pl.ds(ks, W), cols],
            valid, store_o, qb)
        return carry

    lax.fori_loop(0, nq, body, 0)


def _attn_a01(qkv, *, B, S, L, permuted, name):
    runs = S // L
    spec = lambda j: pl.BlockSpec((1, L, D_MODEL), lambda b, r: (b, r, j))
    kern = functools.partial(_attn_a_kernel, L=L, permuted=permuted)
    return pl.pallas_call(
        kern,
        grid=(B, runs),
        in_specs=[spec(0), spec(1), spec(2)],
        out_specs=[pl.BlockSpec((1, L, D_MODEL), lambda b, r: (b, r, 0)),
                   pl.BlockSpec((1, L, LANES), lambda b, r: (b, r, 0))],
        out_shape=[jax.ShapeDtypeStruct((B, S, D_MODEL), jnp.bfloat16),
                   jax.ShapeDtypeStruct((B, S, LANES), jnp.float32)],
        scratch_shapes=[pltpu.VMEM((3, 2 * A_QB, 2 * A_QB), jnp.int32)],
        compiler_params=_cparams(("parallel", "parallel")),
        name=name,
    )(qkv, qkv, qkv)


def _attn_a2_kernel(q_ref, k_ref, v_ref, o_ref, lse_ref):
    ng = q_ref.shape[2]
    L = ng * 16
    row = lax.broadcasted_iota(jnp.int32, (2 * L, L), 0) % L
    col = lax.broadcasted_iota(jnp.int32, (2 * L, L), 1)
    valid = jnp.abs(row - col) <= A_HALF

    for a in range(4):
        def load(ref, a=a):
            return lambda cols: ref[0, 0, :, a, :, cols].reshape(L, LANES)

        def store_o(cols, o, a=a):
            o_ref[0, 0, :, a, :, cols] = o.astype(o_ref.dtype).reshape(ng, 16, LANES)

        lse = _band_pairs(load(q_ref), load(k_ref), load(v_ref), valid, store_o, L, mxu_sums=False)
        lse_ref[0, 0, :, a, :, :] = lse.reshape(ng, 16, LANES)


def _attn_a2(qkv, *, B, S):
    ng = S // PERM_GROUP
    view = qkv.reshape(B, 4, ng, 4, 16, qkv.shape[2])
    blk = lambda c: (1, 1, ng, 4, 16, c)
    spec = lambda j: pl.BlockSpec(blk(D_MODEL), lambda b, r: (b, r, 0, 0, 0, 3 + j))
    o, lse = pl.pallas_call(
        _attn_a2_kernel,
        grid=(B, 4),
        in_specs=[spec(0), spec(1), spec(2)],
        out_specs=[pl.BlockSpec(blk(D_MODEL), lambda b, r: (b, r, 0, 0, 0, 0)),
                   pl.BlockSpec(blk(LANES), lambda b, r: (b, r, 0, 0, 0, 0))],
        out_shape=[jax.ShapeDtypeStruct((B, 4, ng, 4, 16, D_MODEL), jnp.bfloat16),
                   jax.ShapeDtypeStruct((B, 4, ng, 4, 16, LANES), jnp.float32)],
        compiler_params=_cparams(("parallel", "parallel")),
        name="attn_a2",
    )(view, view, view)
    return o.reshape(B, S, D_MODEL), lse.reshape(B, S, LANES)


def _split_bf16(v, parts):
    out = []
    for _ in range(parts - 1):
        hi = v.astype(jnp.bfloat16)
        out.append(hi)
        v = v - hi.astype(jnp.float32)
    out.append(v.astype(jnp.bfloat16))
    return out


def _merge_out_kernel(x_ref, o0_ref, l0_ref, o1_ref, l1_ref, o2_ref, l2_ref, pm_ref, ee_ref, w_ref, y_ref):
    pm = pm_ref[...]
    ee = ee_ref[...]
    f32 = jnp.float32

    def natural(o_ref, l_ref, g):
        o = jnp.dot(pm, o_ref[0, :, g].reshape(PERM_GROUP, D_MODEL), preferred_element_type=f32)
        lse = l_ref[0, :, g].reshape(PERM_GROUP, LANES)
        lse = sum(jnp.dot(pm, part, preferred_element_type=f32) for part in _split_bf16(lse, 3))
        return o, lse

    for g in range(x_ref.shape[1] // PERM_GROUP):
        rows = slice(g * PERM_GROUP, (g + 1) * PERM_GROUP)
        o0 = o0_ref[0, rows, :].astype(f32)
        l0 = l0_ref[0, rows, :]
        o1, l1 = natural(o1_ref, l1_ref, g)
        o2, l2 = natural(o2_ref, l2_ref, g)
        m = jnp.maximum(jnp.maximum(l0, l1), l2)
        e0, e1, e2 = jnp.exp(l0 - m), jnp.exp(l1 - m), jnp.exp(l2 - m)
        den = e0 + e1 + e2

        def spread(e):
            return jnp.dot(jnp.concatenate(_split_bf16(e / den, 2), axis=1), ee, preferred_element_type=f32)

        a = (spread(e0) * o0 + spread(e1) * o1 + spread(e2) * o2).astype(jnp.bfloat16)
        y_ref[0, rows, :] = x_ref[0, rows, :] + jnp.dot(a, w_ref[...], preferred_element_type=f32)


def _merge_out(x3, o0, l0, o1, l1, o2, l2, w, *, groups=2):
    B, S, D = x3.shape
    ng = S // PERM_GROUP
    tm = groups * PERM_GROUP
    pview = lambda a: a.reshape(B, 4, ng, PERM_GROUP // 4, a.shape[2])
    nat = lambda c: pl.BlockSpec((1, tm, c), lambda b, j: (b, j, 0))
    per = lambda c: pl.BlockSpec((1, 4, groups, PERM_GROUP // 4, c), lambda b, j: (b, 0, j, 0, 0))
    full = lambda shape: pl.BlockSpec(shape, lambda b, j: (0, 0))

    pm = _group_perm_matrix()
    ee = np.zeros((2 * LANES, D_MODEL), np.float32)
    for h in range(N_HEADS):
        ee[h, h * HEAD_DIM:(h + 1) * HEAD_DIM] = 1.0
        ee[LANES + h, h * HEAD_DIM:(h + 1) * HEAD_DIM] = 1.0

    return pl.pallas_call(
        _merge_out_kernel,
        grid=(B, ng // groups),
        in_specs=[nat(D), nat(D), nat(LANES), per(D), per(LANES), per(D), per(LANES),
                  full((PERM_GROUP, PERM_GROUP)), full((2 * LANES, D)), full((D, D))],
        out_specs=nat(D),
        out_shape=jax.ShapeDtypeStruct((B, S, D), jnp.float32),
        compiler_params=_cparams(("parallel", "parallel")),
        name="merge_out",
    )(x3, o0, l0, pview(o1), pview(l1), pview(o2), pview(l2),
      jnp.asarray(pm, jnp.bfloat16), jnp.asarray(ee, jnp.bfloat16), w)


def _out_kernel(x_ref, a_ref, w_ref, y_ref):
    y_ref[...] = x_ref[...] + jnp.dot(a_ref[...], w_ref[...], preferred_element_type=jnp.float32)


def _out_proj(x2, a, w, *, tm=1024):
    T, D = x2.shape
    row = pl.BlockSpec((tm, D), lambda i: (i, 0))
    return pl.pallas_call(
        _out_kernel,
        grid=(T // tm,),
        in_specs=[row, row, pl.BlockSpec((D, D), lambda i: (0, 0))],
        out_specs=row,
        out_shape=jax.ShapeDtypeStruct((T, D), jnp.float32),
        compiler_params=_cparams(("parallel",)),
        name="out_proj",
    )(x2, a, w)


B_QROWS = 4
B_KROWS = 12
NA_KH = 8
NA_KW = 16


def _b_slab_index(rows):
    units = rows // B_QROWS
    idx = np.zeros((3, B_QROWS, B_KROWS), np.int32)
    for geo, u in enumerate((0, 1, units - 1)):
        kr0 = min(max(u * B_QROWS - NA_KH // 2, 0), rows - B_KROWS)
        for a in range(B_QROWS):
            qr = u * B_QROWS + a
            rs = min(max(qr - NA_KH // 2, 0), rows - NA_KH)
            for c in range(B_KROWS):
                kr = kr0 + c
                idx[geo, a, c] = kr - qr + NA_KH - 1 if rs <= kr < rs + NA_KH else 2 * NA_KH - 1
    return idx


def _attn_b_kernel(q_ref, k_ref, v_ref, slab_ref, o_ref, tbl_ref, *, rows):
    is_a = _lane_is_first_head()
    nq = B_QROWS * GRID_W
    nk = B_KROWS * GRID_W
    units = rows // B_QROWS
    idx = _b_slab_index(rows)

    @pl.when(pl.program_id(1) == 0)
    def _():
        for h in range(2):
            for geo in range(3):
                for qr in range(B_QROWS):
                    for j in range(B_KROWS // 2):
                        even = slab_ref[h, int(idx[geo, qr, 2 * j])]
                        odd = slab_ref[h, int(idx[geo, qr, 2 * j + 1])]
                        tbl_ref[h, geo, qr * GRID_W:(qr + 1) * GRID_W, j * LANES:(j + 1) * LANES] = (
                            jnp.where(is_a, even, odd))

    for u in range(units):
        qs = u * nq
        kr0 = min(max(u * B_QROWS - NA_KH // 2, 0), rows - B_KROWS)
        ks = kr0 * GRID_W
        geo = 0 if u == 0 else (2 if u == units - 1 else 1)
        qst = _stack_heads(q_ref[0, qs:qs + nq, :], is_a)
        kw = k_ref[0, ks:ks + nk, :]
        vw = v_ref[0, ks:ks + nk, :]
        s = lax.dot_general(qst, kw, NT_DIMS, preferred_element_type=jnp.float32)
        tbl = tbl_ref[:, geo].reshape(2 * nq, nk)
        s = jnp.where(tbl > 0.5 * NEG_INF, s + tbl, NEG_INF)
        pv, _, den = _softmax_pv(s, vw)
        o_ref[0, qs:qs + nq, :] = (_unstack(is_a, pv, nq) / _unstack(is_a, den, nq)).astype(o_ref.dtype)


def _attn_b(qkv, slabs, *, B, S):
    rows = S // GRID_W
    npair = D_MODEL // LANES
    nq = B_QROWS * GRID_W
    nk = B_KROWS * GRID_W

    def in_spec(j):
        return pl.BlockSpec((1, S, LANES), lambda p, b: (b, 0, j * npair + p))

    kern = functools.partial(_attn_b_kernel, rows=rows)
    return pl.pallas_call(
        kern,
        grid=(npair, B),
        in_specs=[in_spec(0), in_spec(1), in_spec(2),
                  pl.BlockSpec((2, 2 * NA_KH, GRID_W, LANES), lambda p, b: (p, 0, 0, 0))],
        out_specs=pl.BlockSpec((1, S, LANES), lambda p, b: (b, 0, p)),
        out_shape=jax.ShapeDtypeStruct((B, S, D_MODEL), jnp.bfloat16),
        scratch_shapes=[pltpu.VMEM((2, 3, nq, nk), jnp.float32)],
        compiler_params=_cparams(("parallel", "arbitrary")),
        name="attn_b",
    )(qkv, qkv, qkv, slabs)


def _b_bias_slabs(rpb):
    H = rpb.shape[0]
    qc = np.arange(GRID_W)[:, None]
    kc = np.arange(GRID_W)[None, :]
    cs = np.clip(qc - NA_KW // 2, 0, GRID_W - NA_KW)
    col_ok = (kc >= cs) & (kc < cs + NA_KW)
    dc = np.clip(kc - qc + NA_KW - 1, 0, 2 * NA_KW - 2)
    onehot = ((dc[None] == np.arange(2 * NA_KW - 1)[:, None, None]) & col_ok[None]).astype(np.float32)
    cval = jnp.einsum("hac,cqk->haqk", rpb, jnp.asarray(onehot), precision=lax.Precision.HIGHEST)
    slabs = jnp.where(jnp.asarray(col_ok)[None, None], cval * LOG2E, NEG_INF)
    slabs = jnp.concatenate([slabs, jnp.full((H, 1, GRID_W, GRID_W), NEG_INF, jnp.float32)], axis=1)
    return jnp.concatenate([slabs, slabs], axis=-1)


def _attn_c_kernel(q_ref, k_ref, v_ref, o_ref):
    is_a = _lane_is_first_head()
    tq = C_SUB_ROWS
    for r0 in range(0, q_ref.shape[1], tq):
        rows = slice(r0, r0 + tq)
        q = q_ref[0, rows, :]
        qst = jnp.concatenate([_stack_heads(q[:, :LANES], is_a), _stack_heads(q[:, LANES:], is_a)], axis=0)
        s = lax.dot_general(qst, k_ref[0], NT_DIMS, preferred_element_type=jnp.float32)
        pv, _, den = _softmax_pv(s, v_ref[0])
        o_ref[0, rows, :LANES] = (_unstack(is_a, pv, tq) / _unstack(is_a, den, tq)).astype(o_ref.dtype)
        o_ref[0, rows, LANES:] = (_unstack(is_a, pv[2 * tq:], tq) / _unstack(is_a, den[2 * tq:], tq)).astype(o_ref.dtype)


C_SUB_ROWS = 128


def _attn_c(q, k2, v2, *, B, S, tq=1024):
    gw = 2 * LANES
    return pl.pallas_call(
        _attn_c_kernel,
        grid=(B, C_KV_HEADS, S // tq),
        in_specs=[
            pl.BlockSpec((1, tq, gw), lambda b, g, i: (b, i, g)),
            pl.BlockSpec((1, S, LANES), lambda b, g, i: (b, 0, g)),
            pl.BlockSpec((1, S, LANES), lambda b, g, i: (b, 0, g)),
        ],
        out_specs=pl.BlockSpec((1, tq, gw), lambda b, g, i: (b, i, g)),
        out_shape=jax.ShapeDtypeStruct((B, S, D_MODEL), jnp.bfloat16),
        compiler_params=_cparams(("parallel", "parallel", "parallel")),
        name="attn_c",
    )(q, k2, v2)


MLP_ROWS = 256


def _mlp_kernel(x_ref, g_ref, wu_ref, wd_ref, gf_ref, y_ref, h_ref, acc_ref, *, final_norm):
    f = pl.program_id(1)
    last = pl.num_programs(1) - 1
    tm = x_ref.shape[0]

    def weights():
        return wu_ref[...].astype(jnp.bfloat16), wd_ref[...].astype(jnp.bfloat16)

    def partial_out(rows, wu, wd):
        u = jnp.dot(h_ref[rows, :], wu, preferred_element_type=jnp.float32)
        r = jnp.maximum(u, 0.0)
        return jnp.dot((r * r).astype(jnp.bfloat16), wd, preferred_element_type=jnp.float32)

    @pl.when(f == 0)
    def _():
        wu, wd = weights()
        for r0 in range(0, tm, MLP_ROWS):
            rows = slice(r0, r0 + MLP_ROWS)
            h_ref[rows, :] = _rms(x_ref[rows, :], g_ref[...]).astype(jnp.bfloat16)
            acc_ref[rows, :] = partial_out(rows, wu, wd)

    @pl.when((f != 0) & (f != last))
    def _():
        acc_ref[...] += partial_out(slice(None), *weights())

    @pl.when(f == last)
    def _():
        wu, wd = weights()
        for r0 in range(0, tm, MLP_ROWS):
            rows = slice(r0, r0 + MLP_ROWS)
            y = x_ref[rows, :] + (acc_ref[rows, :] + partial_out(rows, wu, wd))
            if final_norm:
                y = _rms(y, gf_ref[...])
            y_ref[rows, :] = y


def _mlp(x2, g, wu, wd, gf, *, final_norm, tm=1024, tf=1024):
    T, D = x2.shape
    F = wu.shape[1]
    assert F // tf >= 2, "the kernel treats the first and the last hidden chunk separately"
    kern = functools.partial(_mlp_kernel, final_norm=final_norm)
    return pl.pallas_call(
        kern,
        grid=(T // tm, F // tf),
        in_specs=[
            pl.BlockSpec((tm, D), lambda i, f: (i, 0)),
            pl.BlockSpec((1, D), lambda i, f: (0, 0)),
            pl.BlockSpec((D, tf), lambda i, f: (0, f)),
            pl.BlockSpec((tf, D), lambda i, f: (f, 0)),
            pl.BlockSpec((1, D), lambda i, f: (0, 0)),
        ],
        out_specs=pl.BlockSpec((tm, D), lambda i, f: (i, 0)),
        out_shape=jax.ShapeDtypeStruct((T, D), jnp.float32),
        scratch_shapes=[pltpu.VMEM((tm, D), jnp.bfloat16), pltpu.VMEM((tm, D), jnp.float32)],
        compiler_params=_cparams(("parallel", "arbitrary")),
        name="mlp",
    )(x2, g, wu, wd, gf)


def _rope_angles(pos, dim):
    inv = 1.0 / (ROPE_THETA ** (jnp.arange(0, dim, 2, dtype=jnp.float32) / dim))
    return pos.astype(jnp.float32)[:, None] * inv[None, :]


def _rope_tables(ang):
    cos, sin = jnp.cos(ang), jnp.sin(ang)
    return jnp.tile(cos, (1, 4)), jnp.tile(jnp.concatenate([-sin, sin], axis=-1), (1, 2))


def kernel(x, l0_attn_norm, l0_w_in, l0_w_out, l0_mlp_norm, l0_w_up, l0_w_down, l1_attn_norm, l1_w_in, l1_rpb, l1_w_out, l1_mlp_norm, l1_w_up, l1_w_down, l2_attn_norm, l2_w_in, l2_q_norm, l2_k_norm, l2_w_out, l2_mlp_norm, l2_w_up, l2_w_down, l3_attn_norm, l3_w_in, l3_w_out, l3_mlp_norm, l3_w_up, l3_w_down, final_norm):
    B, S, D = x.shape
    bf = lambda w: w.astype(jnp.bfloat16)
    row = lambda g: g.reshape(1, -1).astype(jnp.float32)

    t = jnp.arange(S, dtype=jnp.int32)
    cos_a, sin_a = _rope_tables(_rope_angles(t, HEAD_DIM))
    cos_c, sin_c = _rope_tables(jnp.concatenate(
        [_rope_angles(t // GRID_W, HALF), _rope_angles(t % GRID_W, HALF)], axis=-1))

    x2 = x.reshape(B * S, D)

    perm = _token_perm(S)
    cos_p, sin_p = cos_a[perm], sin_a[perm]

    def layer_a(x2, attn_norm, w_in, w_out):
        gw = 3 * D_MODEL
        g = row(attn_norm)
        qkv0 = _proj(x2, g, w_in, cos_a, sin_a, rope=True, ncols=gw).reshape(B, S, -1)
        qkv12 = _proj(x2, g, w_in, cos_p, sin_p, rope=True, perm=True, col0=gw, tm=S).reshape(B, S, -1)
        o0, l0 = _attn_a01(qkv0, B=B, S=S, L=S, permuted=False, name="attn_a0")
        o1, l1 = _attn_a01(qkv12, B=B, S=S, L=S // 4, permuted=True, name="attn_a1")
        o2, l2 = _attn_a2(qkv12, B=B, S=S)
        return _merge_out(x2.reshape(B, S, D), o0, l0, o1, l1, o2, l2, bf(w_out)).reshape(B * S, D)

    def layer_b(x2, attn_norm, w_in, rpb, w_out):
        qkv = _proj(x2, row(attn_norm), w_in, cos_a, sin_a, rope=False).reshape(B, S, -1)
        a = _attn_b(qkv, _b_bias_slabs(rpb.astype(jnp.float32)), B=B, S=S)
        return _out_proj(x2, a.reshape(B * S, D), bf(w_out))

    def layer_c(x2, attn_norm, w_in, q_norm, k_norm, w_out):
        gain2 = lambda g: jnp.tile(g.astype(jnp.float32), 2).reshape(1, LANES)
        bd = jnp.kron(jnp.eye(2, dtype=jnp.float32),
                      jnp.full((HEAD_DIM, HEAD_DIM), 1.0 / HEAD_DIM, jnp.float32)).astype(jnp.bfloat16)
        q, k2, v2 = _proj_c(x2, row(attn_norm), bf(w_in), cos_c, sin_c, gain2(q_norm), gain2(k_norm), bd)
        a = _attn_c(q.reshape(B, S, -1), k2.reshape(B, S, -1), v2.reshape(B, S, -1), B=B, S=S)
        return _out_proj(x2, a.reshape(B * S, D), bf(w_out))

    fn = row(final_norm)
    x2 = layer_a(x2, l0_attn_norm, l0_w_in, l0_w_out)
    x2 = _mlp(x2, row(l0_mlp_norm), l0_w_up, l0_w_down, fn, final_norm=False)
    x2 = layer_b(x2, l1_attn_norm, l1_w_in, l1_rpb, l1_w_out)
    x2 = _mlp(x2, row(l1_mlp_norm), l1_w_up, l1_w_down, fn, final_norm=False)
    x2 = layer_c(x2, l2_attn_norm, l2_w_in, l2_q_norm, l2_k_norm, l2_w_out)
    x2 = _mlp(x2, row(l2_mlp_norm), l2_w_up, l2_w_down, fn, final_norm=False)
    x2 = layer_a(x2, l3_attn_norm, l3_w_in, l3_w_out)
    x2 = _mlp(x2, row(l3_mlp_norm), l3_w_up, l3_w_down, fn, final_norm=True)
    return x2.reshape(B, S, D)
```

```python
import functools

import jax
import jax.numpy as jnp
import numpy as np
from jax import lax
from jax.experimental import pallas as pl
from jax.experimental.pallas import tpu as pltpu

D_MODEL = 1024
HEAD_DIM = 64
N_HEADS = 16
D_FF = 4 * D_MODEL
ROPE_THETA = 10000.0
RMS_EPS = 1e-6
NEG_INF = -1e30
GRID_W = 64
A_GROUPS = ((128, 1), (512, 4), (2048, 16))
C_KV_HEADS = 4
LOG2E = 1.4426950408889634
LN2 = 0.6931471805599453
QK_SCALE = HEAD_DIM ** -0.5 * LOG2E

LANES = 128
HALF = HEAD_DIM // 2
VMEM_LIMIT = 56 * 1024 * 1024

NT_DIMS = (((1,), (1,)), ((), ()))


def _cparams(sem):
    return pltpu.CompilerParams(dimension_semantics=sem, vmem_limit_bytes=VMEM_LIMIT)


def _rms(x, g):
    ms = jnp.mean(x * x, axis=-1, keepdims=True)
    return (x * lax.rsqrt(ms + RMS_EPS)) * g


def _lane_is_first_head():
    return lax.broadcasted_iota(jnp.int32, (1, LANES), 1) < HEAD_DIM


def _rope128(y, cos, sin_signed):
    lane = lax.broadcasted_iota(jnp.int32, (1, LANES), 1)
    first_half = (lane % HEAD_DIM) < HALF
    partner = jnp.where(first_half, pltpu.roll(y, LANES - HALF, 1), pltpu.roll(y, HALF, 1))
    return y * cos + partner * sin_signed


PERM_GROUP = 256
PROJ_ROWS = 256


def _token_perm(S):
    t = np.arange(S).reshape(S // PERM_GROUP, 16, 4, 4)
    return t.transpose(3, 0, 2, 1).reshape(S)


def _group_perm_matrix():
    token = np.arange(PERM_GROUP).reshape(16, 4, 4).transpose(2, 1, 0).reshape(-1)
    pm = np.zeros((PERM_GROUP, PERM_GROUP), np.float32)
    pm[token, np.arange(PERM_GROUP)] = 1.0
    return pm


def _proj_kernel(x_ref, g_ref, w_ref, cos_ref, sin_ref, pm_ref, o_ref, h_ref, *, rope, perm):
    n = pl.program_id(1)
    tm = x_ref.shape[0]
    tn = o_ref.shape[1]

    def normalise():
        g = g_ref[...]
        if perm:
            ng = tm // PERM_GROUP
            run = PERM_GROUP // 4
            for grp in range(ng):
                hn = _rms(x_ref[grp * PERM_GROUP:(grp + 1) * PERM_GROUP, :], g).astype(jnp.bfloat16)
                hp = jnp.dot(pm_ref[...], hn, preferred_element_type=jnp.float32).astype(jnp.bfloat16)
                for rho in range(4):
                    dst = (rho * ng + grp) * run
                    h_ref[dst:dst + run, :] = hp[rho * run:(rho + 1) * run, :]
        else:
            for r0 in range(0, tm, PROJ_ROWS):
                h_ref[r0:r0 + PROJ_ROWS, :] = _rms(x_ref[r0:r0 + PROJ_ROWS, :], g).astype(jnp.bfloat16)

    sec = (n * tn // D_MODEL) % 3

    def sub_blocks(epilogue, scale):
        w = w_ref[...].astype(jnp.bfloat16)
        for r0 in range(0, tm, PROJ_ROWS):
            rows = slice(r0, r0 + PROJ_ROWS)
            y = jnp.dot(h_ref[rows, :], w, preferred_element_type=jnp.float32)
            epilogue(rows, y, scale)

    def plain(rows, y, scale):
        o_ref[rows, :] = (y * scale).astype(o_ref.dtype)

    def roped(rows, y, scale):
        cos = cos_ref[rows, :]
        sin = sin_ref[rows, :]
        for c in range(tn // LANES):
            sl = slice(c * LANES, (c + 1) * LANES)
            o_ref[rows, sl] = (_rope128(y[:, sl], cos, sin) * scale).astype(o_ref.dtype)

    qk = roped if rope else plain

    @pl.when(n == 0)
    def _():
        normalise()
        sub_blocks(qk, QK_SCALE)

    @pl.when((n != 0) & (sec == 0))
    def _():
        sub_blocks(qk, QK_SCALE)

    @pl.when(sec == 1)
    def _():
        sub_blocks(qk, 1.0)

    @pl.when(sec == 2)
    def _():
        sub_blocks(plain, 1.0)


def _proj(x2, g, w, cos, sin, *, rope, perm=False, col0=0, ncols=None, tm=2048, tn=1024):
    T, D = x2.shape
    N = w.shape[1] - col0 if ncols is None else ncols
    S = cos.shape[0]
    nsb = S // tm
    cb0 = col0 // tn
    kern = functools.partial(_proj_kernel, rope=rope, perm=perm)
    return pl.pallas_call(
        kern,
        grid=(T // tm, N // tn),
        in_specs=[
            pl.BlockSpec((tm, D), lambda i, n: (i, 0)),
            pl.BlockSpec((1, D), lambda i, n: (0, 0)),
            pl.BlockSpec((D, tn), lambda i, n: (0, cb0 + n)),
            pl.BlockSpec((tm, LANES), lambda i, n: (i % nsb, 0)),
            pl.BlockSpec((tm, LANES), lambda i, n: (i % nsb, 0)),
            pl.BlockSpec((PERM_GROUP, PERM_GROUP), lambda i, n: (0, 0)),
        ],
        out_specs=pl.BlockSpec((tm, tn), lambda i, n: (i, n)),
        out_shape=jax.ShapeDtypeStruct((T, N), jnp.bfloat16),
        scratch_shapes=[pltpu.VMEM((tm, D), jnp.bfloat16)],
        compiler_params=_cparams(("parallel", "arbitrary")),
        name="proj_perm" if perm else "proj",
    )(x2, g, w, cos, sin, jnp.asarray(_group_perm_matrix().T, jnp.bfloat16))


def _proj_c_kernel(x_ref, g_ref, w_ref, cos_ref, sin_ref, qg_ref, kg_ref, bd_ref,
                   q_ref, k_ref, v_ref):
    bd = bd_ref[...]
    is_a = _lane_is_first_head()
    nq = D_MODEL // LANES
    nkv = C_KV_HEADS * HEAD_DIM // LANES

    for r0 in range(0, x_ref.shape[0], PROJ_ROWS):
        rows = slice(r0, r0 + PROJ_ROWS)
        h = _rms(x_ref[rows, :], g_ref[...]).astype(jnp.bfloat16)
        y = jnp.dot(h, w_ref[...], preferred_element_type=jnp.float32)
        cos = cos_ref[rows, :]
        sin = sin_ref[rows, :]

        def head_norm(c, gain):
            yc = y[:, c * LANES:(c + 1) * LANES]
            sq = yc * yc
            hi = sq.astype(jnp.bfloat16)
            lo = (sq - hi.astype(jnp.float32)).astype(jnp.bfloat16)
            ms = (jnp.dot(hi, bd, preferred_element_type=jnp.float32)
                  + jnp.dot(lo, bd, preferred_element_type=jnp.float32))
            return (yc * lax.rsqrt(ms + RMS_EPS)) * gain

        for c in range(nq):
            qn = head_norm(c, qg_ref[...])
            q_ref[rows, c * LANES:(c + 1) * LANES] = (_rope128(qn, cos, sin) * QK_SCALE).astype(q_ref.dtype)
        for c in range(nkv):
            kn = _rope128(head_norm(nq + c, kg_ref[...]), cos, sin)
            ksw = pltpu.roll(kn, HEAD_DIM, 1)
            k_ref[rows, (2 * c) * LANES:(2 * c + 1) * LANES] = jnp.where(is_a, kn, ksw).astype(k_ref.dtype)
            k_ref[rows, (2 * c + 1) * LANES:(2 * c + 2) * LANES] = jnp.where(is_a, ksw, kn).astype(k_ref.dtype)
            vc = y[:, (nq + nkv + c) * LANES:(nq + nkv + c + 1) * LANES]
            vsw = pltpu.roll(vc, HEAD_DIM, 1)
            v_ref[rows, (2 * c) * LANES:(2 * c + 1) * LANES] = jnp.where(is_a, vc, vsw).astype(v_ref.dtype)
            v_ref[rows, (2 * c + 1) * LANES:(2 * c + 2) * LANES] = jnp.where(is_a, vsw, vc).astype(v_ref.dtype)


def _proj_c(x2, g, w, cos, sin, qg, kg, bd, *, tm=1024):
    T, D = x2.shape
    N = w.shape[1]
    S = cos.shape[0]
    nsb = S // tm
    kvw = 2 * C_KV_HEADS * HEAD_DIM
    full = lambda shape: pl.BlockSpec(shape, lambda i: (0, 0))
    return pl.pallas_call(
        _proj_c_kernel,
        grid=(T // tm,),
        in_specs=[
            pl.BlockSpec((tm, D), lambda i: (i, 0)),
            full((1, D)),
            full((D, N)),
            pl.BlockSpec((tm, LANES), lambda i: (i % nsb, 0)),
            pl.BlockSpec((tm, LANES), lambda i: (i % nsb, 0)),
            full((1, LANES)),
            full((1, LANES)),
            full((LANES, LANES)),
        ],
        out_specs=[
            pl.BlockSpec((tm, D_MODEL), lambda i: (i, 0)),
            pl.BlockSpec((tm, kvw), lambda i: (i, 0)),
            pl.BlockSpec((tm, kvw), lambda i: (i, 0)),
        ],
        out_shape=[
            jax.ShapeDtypeStruct((T, D_MODEL), jnp.bfloat16),
            jax.ShapeDtypeStruct((T, kvw), jnp.bfloat16),
            jax.ShapeDtypeStruct((T, kvw), jnp.bfloat16),
        ],
        compiler_params=_cparams(("parallel",)),
        name="proj_c",
    )(x2, g, w, cos, sin, qg, kg, bd)


def _stack_heads(q2, is_a):
    zero = jnp.zeros_like(q2)
    return jnp.concatenate([jnp.where(is_a, q2, zero), jnp.where(is_a, zero, q2)], axis=0)


def _softmax_pv(s, vw, mxu_sums=True):
    m = jnp.max(s, axis=-1, keepdims=True)
    if not mxu_sums:
        p = jnp.exp2(s - m)
        den = jnp.sum(p, axis=-1, keepdims=True)
        return jnp.dot(p.astype(vw.dtype), vw, preferred_element_type=jnp.float32), m, den
    p = jnp.exp2((s - m).astype(jnp.bfloat16))
    vaug = jnp.concatenate([vw, jnp.ones_like(vw)], axis=1)
    r = jnp.dot(p, vaug, preferred_element_type=jnp.float32)
    return r[:, :LANES], m, r[:, LANES:]


def _unstack(is_a, x, n):
    return jnp.where(is_a, x[:n], x[n:2 * n])


A_HALF = 64
A_QB = 128
NPAIR = D_MODEL // LANES


def _band_pairs(load_q, load_k, load_v, valid, store_o, qb, mxu_sums=True):
    is_a = _lane_is_first_head()
    lane = lax.broadcasted_iota(jnp.int32, (1, LANES), 1)
    m_tile = jnp.zeros((qb, LANES), jnp.float32)
    den_tile = jnp.ones((qb, LANES), jnp.float32)
    for p in range(NPAIR):
        cols = slice(p * LANES, (p + 1) * LANES)
        qst = _stack_heads(load_q(cols), is_a)
        vw = load_v(cols)
        s = lax.dot_general(qst, load_k(cols), NT_DIMS, preferred_element_type=jnp.float32)
        s = jnp.where(valid, s, NEG_INF)
        pv, m, den = _softmax_pv(s, vw, mxu_sums)
        store_o(cols, _unstack(is_a, pv, qb) / _unstack(is_a, den, qb))
        in_a, in_b = lane == 2 * p, lane == 2 * p + 1
        m_tile = jnp.where(in_a, m[:qb], jnp.where(in_b, m[qb:], m_tile))
        den_tile = jnp.where(in_a, den[:qb], jnp.where(in_b, den[qb:], den_tile))
    return m_tile * LN2 + jnp.log(den_tile)


def _attn_a_kernel(q_ref, k_ref, v_ref, o_ref, lse_ref, mask_ref, *, L, permuted):
    qb, W = A_QB, 2 * A_QB
    nq = L // qb

    def pos(l):
        return (l // 64) * 64 + 4 * (l % 16) + (l % 64) // 16 if permuted else l

    rel = (pos(lax.broadcasted_iota(jnp.int32, (2 * qb, W), 0) % qb)
           - pos(lax.broadcasted_iota(jnp.int32, (2 * qb, W), 1)))
    for case, delta in enumerate((0, A_HALF, W - qb)):
        mask_ref[case] = (jnp.abs(rel + delta) <= A_HALF).astype(jnp.int32)

    def body(iq, carry):
        qs = pl.multiple_of(iq * qb, qb)
        ks = pl.multiple_of(jnp.clip(qs - A_HALF, 0, L - W), A_HALF)
        case = jnp.where(iq == 0, 0, jnp.where(iq == nq - 1, 2, 1))
        valid = mask_ref[case] != 0

        def store_o(cols, o):
            o_ref[0, pl.ds(qs, qb), cols] = o.astype(o_ref.dtype)

        lse_ref[0, pl.ds(qs, qb), :] = _band_pairs(
            lambda cols: q_ref[0, pl.ds(qs, qb), cols],
            lambda cols: k_ref[0, pl.ds(ks, W), cols],
            lambda cols: v_ref[0, pl.ds(ks, W), cols],
            valid, store_o, qb)
        return carry

    lax.fori_loop(0, nq, body, 0, unroll=4)


def _attn_a01(qkv, *, B, S, L, permuted, name):
    runs = S // L
    spec = lambda j: pl.BlockSpec((1, L, D_MODEL), lambda b, r: (b, r, j))
    kern = functools.partial(_attn_a_kernel, L=L, permuted=permuted)
    return pl.pallas_call(
        kern,
        grid=(B, runs),
        in_specs=[spec(0), spec(1), spec(2)],
        out_specs=[pl.BlockSpec((1, L, D_MODEL), lambda b, r: (b, r, 0)),
                   pl.BlockSpec((1, L, LANES), lambda b, r: (b, r, 0))],
        out_shape=[jax.ShapeDtypeStruct((B, S, D_MODEL), jnp.bfloat16),
                   jax.ShapeDtypeStruct((B, S, LANES), jnp.float32)],
        scratch_shapes=[pltpu.VMEM((3, 2 * A_QB, 2 * A_QB), jnp.int32)],
        compiler_params=_cparams(("parallel", "parallel")),
        name=name,
    )(qkv, qkv, qkv)


def _attn_a2_kernel(q_ref, k_ref, v_ref, o_ref, lse_ref):
    ng = q_ref.shape[2]
    L = ng * 16
    row = lax.broadcasted_iota(jnp.int32, (2 * L, L), 0) % L
    col = lax.broadcasted_iota(jnp.int32, (2 * L, L), 1)
    valid = jnp.abs(row - col) <= A_HALF

    for a in range(4):
        def load(ref, a=a):
            return lambda cols: ref[0, 0, :, a, :, cols].reshape(L, LANES)

        def store_o(cols, o, a=a):
            o_ref[0, 0, :, a, :, cols] = o.astype(o_ref.dtype).reshape(ng, 16, LANES)

        lse = _band_pairs(load(q_ref), load(k_ref), load(v_ref), valid, store_o, L, mxu_sums=False)
        lse_ref[0, 0, :, a, :, :] = lse.reshape(ng, 16, LANES)


def _attn_a2(qkv, *, B, S):
    ng = S // PERM_GROUP
    view = qkv.reshape(B, 4, ng, 4, 16, qkv.shape[2])
    blk = lambda c: (1, 1, ng, 4, 16, c)
    spec = lambda j: pl.BlockSpec(blk(D_MODEL), lambda b, r: (b, r, 0, 0, 0, 3 + j))
    o, lse = pl.pallas_call(
        _attn_a2_kernel,
        grid=(B, 4),
        in_specs=[spec(0), spec(1), spec(2)],
        out_specs=[pl.BlockSpec(blk(D_MODEL), lambda b, r: (b, r, 0, 0, 0, 0)),
                   pl.BlockSpec(blk(LANES), lambda b, r: (b, r, 0, 0, 0, 0))],
        out_shape=[jax.ShapeDtypeStruct((B, 4, ng, 4, 16, D_MODEL), jnp.bfloat16),
                   jax.ShapeDtypeStruct((B, 4, ng, 4, 16, LANES), jnp.float32)],
        compiler_params=_cparams(("parallel", "parallel")),
        name="attn_a2",
    )(view, view, view)
    return o.reshape(B, S, D_MODEL), lse.reshape(B, S, LANES)


def _split_bf16(v, parts):
    out = []
    for _ in range(parts - 1):
        hi = v.astype(jnp.bfloat16)
        out.append(hi)
        v = v - hi.astype(jnp.float32)
    out.append(v.astype(jnp.bfloat16))
    return out


def _merge_out_kernel(x_ref, o0_ref, l0_ref, o1_ref, l1_ref, o2_ref, l2_ref, pm_ref, ee_ref, w_ref, y_ref):
    pm = pm_ref[...]
    ee = ee_ref[...]
    f32 = jnp.float32

    def natural(o_ref, l_ref, g):
        o = jnp.dot(pm, o_ref[0, :, g].reshape(PERM_GROUP, D_MODEL), preferred_element_type=f32)
        lse = l_ref[0, :, g].reshape(PERM_GROUP, LANES)
        lse = sum(jnp.dot(pm, part, preferred_element_type=f32) for part in _split_bf16(lse, 3))
        return o, lse

    for g in range(x_ref.shape[1] // PERM_GROUP):
        rows = slice(g * PERM_GROUP, (g + 1) * PERM_GROUP)
        o0 = o0_ref[0, rows, :].astype(f32)
        l0 = l0_ref[0, rows, :]
        o1, l1 = natural(o1_ref, l1_ref, g)
        o2, l2 = natural(o2_ref, l2_ref, g)
        m = jnp.maximum(jnp.maximum(l0, l1), l2)
        e0, e1, e2 = jnp.exp(l0 - m), jnp.exp(l1 - m), jnp.exp(l2 - m)
        den = e0 + e1 + e2

        def spread(e):
            return jnp.dot(jnp.concatenate(_split_bf16(e / den, 2), axis=1), ee, preferred_element_type=f32)

        a = (spread(e0) * o0 + spread(e1) * o1 + spread(e2) * o2).astype(jnp.bfloat16)
        y_ref[0, rows, :] = x_ref[0, rows, :] + jnp.dot(a, w_ref[...], preferred_element_type=f32)


def _merge_out(x3, o0, l0, o1, l1, o2, l2, w, *, groups=2):
    B, S, D = x3.shape
    ng = S // PERM_GROUP
    tm = groups * PERM_GROUP
    pview = lambda a: a.reshape(B, 4, ng, PERM_GROUP // 4, a.shape[2])
    nat = lambda c: pl.BlockSpec((1, tm, c), lambda b, j: (b, j, 0))
    per = lambda c: pl.BlockSpec((1, 4, groups, PERM_GROUP // 4, c), lambda b, j: (b, 0, j, 0, 0))
    full = lambda shape: pl.BlockSpec(shape, lambda b, j: (0, 0))

    pm = _group_perm_matrix()
    ee = np.zeros((2 * LANES, D_MODEL), np.float32)
    for h in range(N_HEADS):
        ee[h, h * HEAD_DIM:(h + 1) * HEAD_DIM] = 1.0
        ee[LANES + h, h * HEAD_DIM:(h + 1) * HEAD_DIM] = 1.0

    return pl.pallas_call(
        _merge_out_kernel,
        grid=(B, ng // groups),
        in_specs=[nat(D), nat(D), nat(LANES), per(D), per(LANES), per(D), per(LANES),
                  full((PERM_GROUP, PERM_GROUP)), full((2 * LANES, D)), full((D, D))],
        out_specs=nat(D),
        out_shape=jax.ShapeDtypeStruct((B, S, D), jnp.float32),
        compiler_params=_cparams(("parallel", "parallel")),
        name="merge_out",
    )(x3, o0, l0, pview(o1), pview(l1), pview(o2), pview(l2),
      jnp.asarray(pm, jnp.bfloat16), jnp.asarray(ee, jnp.bfloat16), w)


def _out_kernel(x_ref, a_ref, w_ref, y_ref):
    y_ref[...] = x_ref[...] + jnp.dot(a_ref[...], w_ref[...], preferred_element_type=jnp.float32)


def _out_proj(x2, a, w, *, tm=1024):
    T, D = x2.shape
    row = pl.BlockSpec((tm, D), lambda i: (i, 0))
    return pl.pallas_call(
        _out_kernel,
        grid=(T // tm,),
        in_specs=[row, row, pl.BlockSpec((D, D), lambda i: (0, 0))],
        out_specs=row,
        out_shape=jax.ShapeDtypeStruct((T, D), jnp.float32),
        compiler_params=_cparams(("parallel",)),
        name="out_proj",
    )(x2, a, w)


B_QROWS = 4
B_KROWS = 12
NA_KH = 8
NA_KW = 16


def _b_slab_index(rows):
    units = rows // B_QROWS
    idx = np.zeros((3, B_QROWS, B_KROWS), np.int32)
    for geo, u in enumerate((0, 1, units - 1)):
        kr0 = min(max(u * B_QROWS - NA_KH // 2, 0), rows - B_KROWS)
        for a in range(B_QROWS):
            qr = u * B_QROWS + a
            rs = min(max(qr - NA_KH // 2, 0), rows - NA_KH)
            for c in range(B_KROWS):
                kr = kr0 + c
                idx[geo, a, c] = kr - qr + NA_KH - 1 if rs <= kr < rs + NA_KH else 2 * NA_KH - 1
    return idx


def _attn_b_kernel(q_ref, k_ref, v_ref, slab_ref, o_ref, tbl_ref, *, rows):
    is_a = _lane_is_first_head()
    nq = B_QROWS * GRID_W
    nk = B_KROWS * GRID_W
    units = rows // B_QROWS
    idx = _b_slab_index(rows)

    @pl.when(pl.program_id(1) == 0)
    def _():
        for h in range(2):
            for geo in range(3):
                for qr in range(B_QROWS):
                    for j in range(B_KROWS // 2):
                        even = slab_ref[h, int(idx[geo, qr, 2 * j])]
                        odd = slab_ref[h, int(idx[geo, qr, 2 * j + 1])]
                        tbl_ref[h, geo, qr * GRID_W:(qr + 1) * GRID_W, j * LANES:(j + 1) * LANES] = (
                            jnp.where(is_a, even, odd))

    for u in range(units):
        qs = u * nq
        kr0 = min(max(u * B_QROWS - NA_KH // 2, 0), rows - B_KROWS)
        ks = kr0 * GRID_W
        geo = 0 if u == 0 else (2 if u == units - 1 else 1)
        qst = _stack_heads(q_ref[0, qs:qs + nq, :], is_a)
        kw = k_ref[0, ks:ks + nk, :]
        vw = v_ref[0, ks:ks + nk, :]
        s = lax.dot_general(qst, kw, NT_DIMS, preferred_element_type=jnp.float32)
        tbl = tbl_ref[:, geo].reshape(2 * nq, nk)
        s = jnp.where(tbl > 0.5 * NEG_INF, s + tbl, NEG_INF)
        pv, _, den = _softmax_pv(s, vw)
        o_ref[0, qs:qs + nq, :] = (_unstack(is_a, pv, nq) / _unstack(is_a, den, nq)).astype(o_ref.dtype)


def _attn_b(qkv, slabs, *, B, S):
    rows = S // GRID_W
    npair = D_MODEL // LANES
    nq = B_QROWS * GRID_W
    nk = B_KROWS * GRID_W

    def in_spec(j):
        return pl.BlockSpec((1, S, LANES), lambda p, b: (b, 0, j * npair + p))

    kern = functools.partial(_attn_b_kernel, rows=rows)
    return pl.pallas_call(
        kern,
        grid=(npair, B),
        in_specs=[in_spec(0), in_spec(1), in_spec(2),
                  pl.BlockSpec((2, 2 * NA_KH, GRID_W, LANES), lambda p, b: (p, 0, 0, 0))],
        out_specs=pl.BlockSpec((1, S, LANES), lambda p, b: (b, 0, p)),
        out_shape=jax.ShapeDtypeStruct((B, S, D_MODEL), jnp.bfloat16),
        scratch_shapes=[pltpu.VMEM((2, 3, nq, nk), jnp.float32)],
        compiler_params=_cparams(("parallel", "arbitrary")),
        name="attn_b",
    )(qkv, qkv, qkv, slabs)


def _b_bias_slabs(rpb):
    H = rpb.shape[0]
    qc = np.arange(GRID_W)[:, None]
    kc = (np.arange(LANES) % GRID_W)[None, :]
    cs = np.clip(qc - NA_KW // 2, 0, GRID_W - NA_KW)
    col_ok = (kc >= cs) & (kc < cs + NA_KW)
    dc = np.clip(kc - qc + NA_KW - 1, 0, 2 * NA_KW - 2)
    onehot = ((dc[None] == np.arange(2 * NA_KW - 1)[:, None, None]) & col_ok[None]).astype(np.float32)
    cval = jnp.einsum("hac,cqk->haqk", rpb, jnp.asarray(onehot), precision=lax.Precision.HIGHEST)
    slabs = jnp.where(jnp.asarray(col_ok)[None, None], cval * LOG2E, NEG_INF)
    return jnp.concatenate([slabs, jnp.full((H, 1, GRID_W, LANES), NEG_INF, jnp.float32)], axis=1)


def _attn_c_kernel(q_ref, k_ref, v_ref, o_ref):
    is_a = _lane_is_first_head()
    tq = C_SUB_ROWS
    for r0 in range(0, q_ref.shape[1], tq):
        rows = slice(r0, r0 + tq)
        q = q_ref[0, rows, :]
        qst = jnp.concatenate([_stack_heads(q[:, :LANES], is_a), _stack_heads(q[:, LANES:], is_a)], axis=0)
        s = lax.dot_general(qst, k_ref[0], NT_DIMS, preferred_element_type=jnp.float32)
        pv, _, den = _softmax_pv(s, v_ref[0])
        o_ref[0, rows, :LANES] = (_unstack(is_a, pv, tq) / _unstack(is_a, den, tq)).astype(o_ref.dtype)
        o_ref[0, rows, LANES:] = (_unstack(is_a, pv[2 * tq:], tq) / _unstack(is_a, den[2 * tq:], tq)).astype(o_ref.dtype)


C_SUB_ROWS = 128


def _attn_c(q, k2, v2, *, B, S, tq=1024):
    gw = 2 * LANES
    return pl.pallas_call(
        _attn_c_kernel,
        grid=(B, C_KV_HEADS, S // tq),
        in_specs=[
            pl.BlockSpec((1, tq, gw), lambda b, g, i: (b, i, g)),
            pl.BlockSpec((1, S, LANES), lambda b, g, i: (b, 0, g)),
            pl.BlockSpec((1, S, LANES), lambda b, g, i: (b, 0, g)),
        ],
        out_specs=pl.BlockSpec((1, tq, gw), lambda b, g, i: (b, i, g)),
        out_shape=jax.ShapeDtypeStruct((B, S, D_MODEL), jnp.bfloat16),
        compiler_params=_cparams(("parallel", "parallel", "parallel")),
        name="attn_c",
    )(q, k2, v2)


MLP_ROWS = 256


def _mlp_kernel(x_ref, g_ref, wu_ref, wd_ref, gf_ref, y_ref, h_ref, acc_ref, *, final_norm):
    f = pl.program_id(1)
    last = pl.num_programs(1) - 1
    tm = x_ref.shape[0]

    def weights():
        return wu_ref[...].astype(jnp.bfloat16), wd_ref[...].astype(jnp.bfloat16)

    def partial_out(rows, wu, wd):
        u = jnp.dot(h_ref[rows, :], wu, preferred_element_type=jnp.float32)
        r = jnp.maximum(u, 0.0)
        return jnp.dot((r * r).astype(jnp.bfloat16), wd, preferred_element_type=jnp.float32)

    @pl.when(f == 0)
    def _():
        wu, wd = weights()
        for r0 in range(0, tm, MLP_ROWS):
            rows = slice(r0, r0 + MLP_ROWS)
            h_ref[rows, :] = _rms(x_ref[rows, :], g_ref[...]).astype(jnp.bfloat16)
            acc_ref[rows, :] = partial_out(rows, wu, wd)

    @pl.when((f != 0) & (f != last))
    def _():
        acc_ref[...] += partial_out(slice(None), *weights())

    @pl.when(f == last)
    def _():
        wu, wd = weights()
        for r0 in range(0, tm, MLP_ROWS):
            rows = slice(r0, r0 + MLP_ROWS)
            y = x_ref[rows, :] + (acc_ref[rows, :] + partial_out(rows, wu, wd))
            if final_norm:
                y = _rms(y, gf_ref[...])
            y_ref[rows, :] = y


def _mlp(x2, g, wu, wd, gf, *, final_norm, tm=1024, tf=1024):
    T, D = x2.shape
    F = wu.shape[1]
    assert F // tf >= 2, "the kernel treats the first and the last hidden chunk separately"
    kern = functools.partial(_mlp_kernel, final_norm=final_norm)
    return pl.pallas_call(
        kern,
        grid=(T // tm, F // tf),
        in_specs=[
            pl.BlockSpec((tm, D), lambda i, f: (i, 0)),
            pl.BlockSpec((1, D), lambda i, f: (0, 0)),
            pl.BlockSpec((D, tf), lambda i, f: (0, f)),
            pl.BlockSpec((tf, D), lambda i, f: (f, 0)),
            pl.BlockSpec((1, D), lambda i, f: (0, 0)),
        ],
        out_specs=pl.BlockSpec((tm, D), lambda i, f: (i, 0)),
        out_shape=jax.ShapeDtypeStruct((T, D), jnp.float32),
        scratch_shapes=[pltpu.VMEM((tm, D), jnp.bfloat16), pltpu.VMEM((tm, D), jnp.float32)],
        compiler_params=_cparams(("parallel", "arbitrary")),
        name="mlp",
    )(x2, g, wu, wd, gf)


def _rope_angles(pos, dim):
    inv = 1.0 / (ROPE_THETA ** (jnp.arange(0, dim, 2, dtype=jnp.float32) / dim))
    return pos.astype(jnp.float32)[:, None] * inv[None, :]


def _rope_tables(ang):
    cos, sin = jnp.cos(ang), jnp.sin(ang)
    return jnp.tile(cos, (1, 4)), jnp.tile(jnp.concatenate([-sin, sin], axis=-1), (1, 2))


def kernel(x, l0_attn_norm, l0_w_in, l0_w_out, l0_mlp_norm, l0_w_up, l0_w_down, l1_attn_norm, l1_w_in, l1_rpb, l1_w_out, l1_mlp_norm, l1_w_up, l1_w_down, l2_attn_norm, l2_w_in, l2_q_norm, l2_k_norm, l2_w_out, l2_mlp_norm, l2_w_up, l2_w_down, l3_attn_norm, l3_w_in, l3_w_out, l3_mlp_norm, l3_w_up, l3_w_down, final_norm):
    B, S, D = x.shape
    bf = lambda w: w.astype(jnp.bfloat16)
    row = lambda g: g.reshape(1, -1).astype(jnp.float32)

    t = jnp.arange(S, dtype=jnp.int32)
    cos_a, sin_a = _rope_tables(_rope_angles(t, HEAD_DIM))
    cos_c, sin_c = _rope_tables(jnp.concatenate(
        [_rope_angles(t // GRID_W, HALF), _rope_angles(t % GRID_W, HALF)], axis=-1))

    x2 = x.reshape(B * S, D)

    perm = _token_perm(S)
    cos_p, sin_p = cos_a[perm], sin_a[perm]

    def layer_a(x2, attn_norm, w_in, w_out):
        gw = 3 * D_MODEL
        g = row(attn_norm)
        qkv0 = _proj(x2, g, w_in, cos_a, sin_a, rope=True, ncols=gw).reshape(B, S, -1)
        qkv12 = _proj(x2, g, w_in, cos_p, sin_p, rope=True, perm=True, col0=gw, tm=S).reshape(B, S, -1)
        o0, l0 = _attn_a01(qkv0, B=B, S=S, L=S, permuted=False, name="attn_a0")
        o1, l1 = _attn_a01(qkv12, B=B, S=S, L=S // 4, permuted=True, name="attn_a1")
        o2, l2 = _attn_a2(qkv12, B=B, S=S)
        return _merge_out(x2.reshape(B, S, D), o0, l0, o1, l1, o2, l2, bf(w_out)).reshape(B * S, D)

    def layer_b(x2, attn_norm, w_in, rpb, w_out):
        qkv = _proj(x2, row(attn_norm), w_in, cos_a, sin_a, rope=False).reshape(B, S, -1)
        a = _attn_b(qkv, _b_bias_slabs(rpb.astype(jnp.float32)), B=B, S=S)
        return _out_proj(x2, a.reshape(B * S, D), bf(w_out))

    def layer_c(x2, attn_norm, w_in, q_norm, k_norm, w_out):
        gain2 = lambda g: jnp.tile(g.astype(jnp.float32), 2).reshape(1, LANES)
        bd = jnp.kron(jnp.eye(2, dtype=jnp.float32),
                      jnp.full((HEAD_DIM, HEAD_DIM), 1.0 / HEAD_DIM, jnp.float32)).astype(jnp.bfloat16)
        q, k2, v2 = _proj_c(x2, row(attn_norm), bf(w_in), cos_c, sin_c, gain2(q_norm), gain2(k_norm), bd)
        a = _attn_c(q.reshape(B, S, -1), k2.reshape(B, S, -1), v2.reshape(B, S, -1), B=B, S=S)
        return _out_proj(x2, a.reshape(B * S, D), bf(w_out))

    fn = row(final_norm)
    x2 = layer_a(x2, l0_attn_norm, l0_w_in, l0_w_out)
    x2 = _mlp(x2, row(l0_mlp_norm), l0_w_up, l0_w_down, fn, final_norm=False)
    x2 = layer_b(x2, l1_attn_norm, l1_w_in, l1_rpb, l1_w_out)
    x2 = _mlp(x2, row(l1_mlp_norm), l1_w_up, l1_w_down, fn, final_norm=False)
    x2 = layer_c(x2, l2_attn_norm, l2_w_in, l2_q_norm, l2_k_norm, l2_w_out)
    x2 = _mlp(x2, row(l2_mlp_norm), l2_w_up, l2_w_down, fn, final_norm=False)
    x2 = layer_a(x2, l3_attn_norm, l3_w_in, l3_w_out)
    x2 = _mlp(x2, row(l3_mlp_norm), l3_w_up, l3_w_down, fn, final_norm=True)
    return x2.reshape(B, S, D)
```

```python
import functools

import jax
import jax.numpy as jnp
import numpy as np
from jax import lax
from jax.experimental import pallas as pl
from jax.experimental.pallas import tpu as pltpu

D_MODEL = 1024
HEAD_DIM = 64
N_HEADS = 16
D_FF = 4 * D_MODEL
ROPE_THETA = 10000.0
RMS_EPS = 1e-6
NEG_INF = -1e30
GRID_W = 64
A_GROUPS = ((128, 1), (512, 4), (2048, 16))
C_KV_HEADS = 4
LOG2E = 1.4426950408889634
LN2 = 0.6931471805599453
QK_SCALE = HEAD_DIM ** -0.5 * LOG2E

LANES = 128
HALF = HEAD_DIM // 2
VMEM_LIMIT = 56 * 1024 * 1024

NT_DIMS = (((1,), (1,)), ((), ()))


def _cparams(sem):
    return pltpu.CompilerParams(dimension_semantics=sem, vmem_limit_bytes=VMEM_LIMIT)


def _rms(x, g):
    ms = jnp.mean(x * x, axis=-1, keepdims=True)
    return (x * lax.rsqrt(ms + RMS_EPS)) * g


def _lane_is_first_head():
    return lax.broadcasted_iota(jnp.int32, (1, LANES), 1) < HEAD_DIM


def _rope128(y, cos, sin_signed):
    lane = lax.broadcasted_iota(jnp.int32, (1, LANES), 1)
    first_half = (lane % HEAD_DIM) < HALF
    partner = jnp.where(first_half, pltpu.roll(y, LANES - HALF, 1), pltpu.roll(y, HALF, 1))
    return y * cos + partner * sin_signed


PERM_GROUP = 256
PROJ_ROWS = 256


def _token_perm(S):
    t = np.arange(S).reshape(S // PERM_GROUP, 16, 4, 4)
    return t.transpose(3, 0, 2, 1).reshape(S)


def _group_perm_matrix():
    token = np.arange(PERM_GROUP).reshape(16, 4, 4).transpose(2, 1, 0).reshape(-1)
    pm = np.zeros((PERM_GROUP, PERM_GROUP), np.float32)
    pm[token, np.arange(PERM_GROUP)] = 1.0
    return pm


def _proj_kernel(x_ref, g_ref, w_ref, cos_ref, sin_ref, pm_ref, o_ref, h_ref, *, rope, perm):
    n = pl.program_id(1)
    tm = x_ref.shape[0]
    tn = o_ref.shape[1]

    def normalise():
        g = g_ref[...]
        if perm:
            ng = tm // PERM_GROUP
            run = PERM_GROUP // 4
            for grp in range(ng):
                hn = _rms(x_ref[grp * PERM_GROUP:(grp + 1) * PERM_GROUP, :], g).astype(jnp.bfloat16)
                hp = jnp.dot(pm_ref[...], hn, preferred_element_type=jnp.float32).astype(jnp.bfloat16)
                for rho in range(4):
                    dst = (rho * ng + grp) * run
                    h_ref[dst:dst + run, :] = hp[rho * run:(rho + 1) * run, :]
        else:
            for r0 in range(0, tm, PROJ_ROWS):
                h_ref[r0:r0 + PROJ_ROWS, :] = _rms(x_ref[r0:r0 + PROJ_ROWS, :], g).astype(jnp.bfloat16)

    sec = (n * tn // D_MODEL) % 3

    def sub_blocks(epilogue, scale):
        w = w_ref[...].astype(jnp.bfloat16)
        for r0 in range(0, tm, PROJ_ROWS):
            rows = slice(r0, r0 + PROJ_ROWS)
            y = jnp.dot(h_ref[rows, :], w, preferred_element_type=jnp.float32)
            epilogue(rows, y, scale)

    def plain(rows, y, scale):
        o_ref[rows, :] = (y * scale).astype(o_ref.dtype)

    def roped(rows, y, scale):
        cos = cos_ref[rows, :]
        sin = sin_ref[rows, :]
        for c in range(tn // LANES):
            sl = slice(c * LANES, (c + 1) * LANES)
            o_ref[rows, sl] = (_rope128(y[:, sl], cos, sin) * scale).astype(o_ref.dtype)

    qk = roped if rope else plain

    @pl.when(n == 0)
    def _():
        normalise()
        sub_blocks(qk, QK_SCALE)

    @pl.when((n != 0) & (sec == 0))
    def _():
        sub_blocks(qk, QK_SCALE)

    @pl.when(sec == 1)
    def _():
        sub_blocks(qk, 1.0)

    @pl.when(sec == 2)
    def _():
        sub_blocks(plain, 1.0)


def _proj(x2, g, w, cos, sin, *, rope, perm=False, col0=0, ncols=None, tm=2048, tn=1024):
    T, D = x2.shape
    N = w.shape[1] - col0 if ncols is None else ncols
    S = cos.shape[0]
    nsb = S // tm
    cb0 = col0 // tn
    kern = functools.partial(_proj_kernel, rope=rope, perm=perm)
    return pl.pallas_call(
        kern,
        grid=(T // tm, N // tn),
        in_specs=[
            pl.BlockSpec((tm, D), lambda i, n: (i, 0)),
            pl.BlockSpec((1, D), lambda i, n: (0, 0)),
            pl.BlockSpec((D, tn), lambda i, n: (0, cb0 + n)),
            pl.BlockSpec((tm, LANES), lambda i, n: (i % nsb, 0)),
            pl.BlockSpec((tm, LANES), lambda i, n: (i % nsb, 0)),
            pl.BlockSpec((PERM_GROUP, PERM_GROUP), lambda i, n: (0, 0)),
        ],
        out_specs=pl.BlockSpec((tm, tn), lambda i, n: (i, n)),
        out_shape=jax.ShapeDtypeStruct((T, N), jnp.bfloat16),
        scratch_shapes=[pltpu.VMEM((tm, D), jnp.bfloat16)],
        compiler_params=_cparams(("parallel", "arbitrary")),
        name="proj_perm" if perm else "proj",
    )(x2, g, w, cos, sin, jnp.asarray(_group_perm_matrix().T, jnp.bfloat16))


def _proj_c_kernel(x_ref, g_ref, w_ref, cos_ref, sin_ref, qg_ref, kg_ref, bd_ref,
                   q_ref, k_ref, v_ref):
    bd = bd_ref[...]
    is_a = _lane_is_first_head()
    nq = D_MODEL // LANES
    nkv = C_KV_HEADS * HEAD_DIM // LANES

    for r0 in range(0, x_ref.shape[0], PROJ_ROWS):
        rows = slice(r0, r0 + PROJ_ROWS)
        h = _rms(x_ref[rows, :], g_ref[...]).astype(jnp.bfloat16)
        y = jnp.dot(h, w_ref[...], preferred_element_type=jnp.float32)
        cos = cos_ref[rows, :]
        sin = sin_ref[rows, :]

        def head_norm(c, gain):
            yc = y[:, c * LANES:(c + 1) * LANES]
            sq = yc * yc
            hi = sq.astype(jnp.bfloat16)
            lo = (sq - hi.astype(jnp.float32)).astype(jnp.bfloat16)
            ms = (jnp.dot(hi, bd, preferred_element_type=jnp.float32)
                  + jnp.dot(lo, bd, preferred_element_type=jnp.float32))
            return (yc * lax.rsqrt(ms + RMS_EPS)) * gain

        for c in range(nq):
            qn = head_norm(c, qg_ref[...])
            q_ref[rows, c * LANES:(c + 1) * LANES] = (_rope128(qn, cos, sin) * QK_SCALE).astype(q_ref.dtype)
        for c in range(nkv):
            kn = _rope128(head_norm(nq + c, kg_ref[...]), cos, sin)
            ksw = pltpu.roll(kn, HEAD_DIM, 1)
            k_ref[rows, (2 * c) * LANES:(2 * c + 1) * LANES] = jnp.where(is_a, kn, ksw).astype(k_ref.dtype)
            k_ref[rows, (2 * c + 1) * LANES:(2 * c + 2) * LANES] = jnp.where(is_a, ksw, kn).astype(k_ref.dtype)
            vc = y[:, (nq + nkv + c) * LANES:(nq + nkv + c + 1) * LANES]
            vsw = pltpu.roll(vc, HEAD_DIM, 1)
            v_ref[rows, (2 * c) * LANES:(2 * c + 1) * LANES] = jnp.where(is_a, vc, vsw).astype(v_ref.dtype)
            v_ref[rows, (2 * c + 1) * LANES:(2 * c + 2) * LANES] = jnp.where(is_a, vsw, vc).astype(v_ref.dtype)


def _proj_c(x2, g, w, cos, sin, qg, kg, bd, *, tm=1024):
    T, D = x2.shape
    N = w.shape[1]
    S = cos.shape[0]
    nsb = S // tm
    kvw = 2 * C_KV_HEADS * HEAD_DIM
    full = lambda shape: pl.BlockSpec(shape, lambda i: (0, 0))
    return pl.pallas_call(
        _proj_c_kernel,
        grid=(T // tm,),
        in_specs=[
            pl.BlockSpec((tm, D), lambda i: (i, 0)),
            full((1, D)),
            full((D, N)),
            pl.BlockSpec((tm, LANES), lambda i: (i % nsb, 0)),
            pl.BlockSpec((tm, LANES), lambda i: (i % nsb, 0)),
            full((1, LANES)),
            full((1, LANES)),
            full((LANES, LANES)),
        ],
        out_specs=[
            pl.BlockSpec((tm, D_MODEL), lambda i: (i, 0)),
            pl.BlockSpec((tm, kvw), lambda i: (i, 0)),
            pl.BlockSpec((tm, kvw), lambda i: (i, 0)),
        ],
        out_shape=[
            jax.ShapeDtypeStruct((T, D_MODEL), jnp.bfloat16),
            jax.ShapeDtypeStruct((T, kvw), jnp.bfloat16),
            jax.ShapeDtypeStruct((T, kvw), jnp.bfloat16),
        ],
        compiler_params=_cparams(("parallel",)),
        name="proj_c",
    )(x2, g, w, cos, sin, qg, kg, bd)


def _stack_heads(q2, is_a):
    zero = jnp.zeros_like(q2)
    return jnp.concatenate([jnp.where(is_a, q2, zero), jnp.where(is_a, zero, q2)], axis=0)


def _softmax_pv(s, vw, mxu_sums=True):
    m = jnp.max(s, axis=-1, keepdims=True)
    if not mxu_sums:
        p = jnp.exp2(s - m)
        den = jnp.sum(p, axis=-1, keepdims=True)
        return jnp.dot(p.astype(vw.dtype), vw, preferred_element_type=jnp.float32), m, den
    p = jnp.exp2((s - m).astype(jnp.bfloat16))
    vaug = jnp.concatenate([vw, jnp.ones_like(vw)], axis=1)
    r = jnp.dot(p, vaug, preferred_element_type=jnp.float32)
    return r[:, :LANES], m, r[:, LANES:]


def _unstack(is_a, x, n):
    return jnp.where(is_a, x[:n], x[n:2 * n])


A_HALF = 64
A_QB = 128
NPAIR = D_MODEL // LANES


def _band_pairs(load_q, load_k, load_v, valid, store_o, qb, mxu_sums=True):
    is_a = _lane_is_first_head()
    lane = lax.broadcasted_iota(jnp.int32, (1, LANES), 1)
    m_tile = jnp.zeros((qb, LANES), jnp.float32)
    den_tile = jnp.ones((qb, LANES), jnp.float32)
    for p in range(NPAIR):
        cols = slice(p * LANES, (p + 1) * LANES)
        qst = _stack_heads(load_q(cols), is_a)
        vw = load_v(cols)
        s = lax.dot_general(qst, load_k(cols), NT_DIMS, preferred_element_type=jnp.float32)
        s = jnp.where(valid, s, NEG_INF)
        pv, m, den = _softmax_pv(s, vw, mxu_sums)
        store_o(cols, _unstack(is_a, pv, qb) / _unstack(is_a, den, qb))
        in_a, in_b = lane == 2 * p, lane == 2 * p + 1
        m_tile = jnp.where(in_a, m[:qb], jnp.where(in_b, m[qb:], m_tile))
        den_tile = jnp.where(in_a, den[:qb], jnp.where(in_b, den[qb:], den_tile))
    return m_tile * LN2 + jnp.log(den_tile)


def _attn_a_kernel(q_ref, k_ref, v_ref, o_ref, lse_ref, mask_ref, *, L, permuted):
    qb, W = A_QB, 2 * A_QB
    nq = L // qb

    def pos(l):
        return (l // 64) * 64 + 4 * (l % 16) + (l % 64) // 16 if permuted else l

    rel = (pos(lax.broadcasted_iota(jnp.int32, (2 * qb, W), 0) % qb)
           - pos(lax.broadcasted_iota(jnp.int32, (2 * qb, W), 1)))
    for case, delta in enumerate((0, A_HALF, W - qb)):
        mask_ref[case] = (jnp.abs(rel + delta) <= A_HALF).astype(jnp.int32)

    def body(iq, carry):
        qs = pl.multiple_of(iq * qb, qb)
        ks = pl.multiple_of(jnp.clip(qs - A_HALF, 0, L - W), A_HALF)
        case = jnp.where(iq == 0, 0, jnp.where(iq == nq - 1, 2, 1))
        valid = mask_ref[case] != 0

        def store_o(cols, o):
            o_ref[0, pl.ds(qs, qb), cols] = o.astype(o_ref.dtype)

        lse_ref[0, pl.ds(qs, qb), :] = _band_pairs(
            lambda cols: q_ref[0, pl.ds(qs, qb), cols],
            lambda cols: k_ref[0, pl.ds(ks, W), cols],
            lambda cols: v_ref[0, pl.ds(ks, W), cols],
            valid, store_o, qb)
        return carry

    lax.fori_loop(0, nq, body, 0, unroll=min(nq, 8))


def _attn_a01(qkv, *, B, S, L, permuted, name):
    runs = S // L
    spec = lambda j: pl.BlockSpec((1, L, D_MODEL), lambda b, r: (b, r, j))
    kern = functools.partial(_attn_a_kernel, L=L, permuted=permuted)
    return pl.pallas_call(
        kern,
        grid=(B, runs),
        in_specs=[spec(0), spec(1), spec(2)],
        out_specs=[pl.BlockSpec((1, L, D_MODEL), lambda b, r: (b, r, 0)),
                   pl.BlockSpec((1, L, LANES), lambda b, r: (b, r, 0))],
        out_shape=[jax.ShapeDtypeStruct((B, S, D_MODEL), jnp.bfloat16),
                   jax.ShapeDtypeStruct((B, S, LANES), jnp.float32)],
        scratch_shapes=[pltpu.VMEM((3, 2 * A_QB, 2 * A_QB), jnp.int32)],
        compiler_params=_cparams(("parallel", "parallel")),
        name=name,
    )(qkv, qkv, qkv)


def _attn_a2_kernel(q_ref, k_ref, v_ref, o_ref, lse_ref):
    ng = q_ref.shape[2]
    L = ng * 16
    row = lax.broadcasted_iota(jnp.int32, (2 * L, L), 0) % L
    col = lax.broadcasted_iota(jnp.int32, (2 * L, L), 1)
    valid = jnp.abs(row - col) <= A_HALF

    for a in range(4):
        def load(ref, a=a):
            return lambda cols: ref[0, 0, :, a, :, cols].reshape(L, LANES)

        def store_o(cols, o, a=a):
            o_ref[0, 0, :, a, :, cols] = o.astype(o_ref.dtype).reshape(ng, 16, LANES)

        lse = _band_pairs(load(q_ref), load(k_ref), load(v_ref), valid, store_o, L, mxu_sums=False)
        lse_ref[0, 0, :, a, :, :] = lse.reshape(ng, 16, LANES)


def _attn_a2(qkv, *, B, S):
    ng = S // PERM_GROUP
    view = qkv.reshape(B, 4, ng, 4, 16, qkv.shape[2])
    blk = lambda c: (1, 1, ng, 4, 16, c)
    spec = lambda j: pl.BlockSpec(blk(D_MODEL), lambda b, r: (b, r, 0, 0, 0, 3 + j))
    o, lse = pl.pallas_call(
        _attn_a2_kernel,
        grid=(B, 4),
        in_specs=[spec(0), spec(1), spec(2)],
        out_specs=[pl.BlockSpec(blk(D_MODEL), lambda b, r: (b, r, 0, 0, 0, 0)),
                   pl.BlockSpec(blk(LANES), lambda b, r: (b, r, 0, 0, 0, 0))],
        out_shape=[jax.ShapeDtypeStruct((B, 4, ng, 4, 16, D_MODEL), jnp.bfloat16),
                   jax.ShapeDtypeStruct((B, 4, ng, 4, 16, LANES), jnp.float32)],
        compiler_params=_cparams(("parallel", "parallel")),
        name="attn_a2",
    )(view, view, view)
    return o.reshape(B, S, D_MODEL), lse.reshape(B, S, LANES)


def _split_bf16(v, parts):
    out = []
    for _ in range(parts - 1):
        hi = v.astype(jnp.bfloat16)
        out.append(hi)
        v = v - hi.astype(jnp.float32)
    out.append(v.astype(jnp.bfloat16))
    return out


def _merge_out_kernel(x_ref, o0_ref, l0_ref, o1_ref, l1_ref, o2_ref, l2_ref, pm_ref, ee_ref, w_ref, y_ref):
    pm = pm_ref[...]
    ee = ee_ref[...]
    f32 = jnp.float32

    def natural(o_ref, l_ref, g):
        lse = l_ref[0, :, g].reshape(PERM_GROUP, LANES)
        both = jnp.concatenate([o_ref[0, :, g].reshape(PERM_GROUP, D_MODEL)] + _split_bf16(lse, 3), axis=1)
        r = jnp.dot(pm, both, preferred_element_type=f32)
        d = D_MODEL
        return r[:, :d], r[:, d:d + LANES] + r[:, d + LANES:d + 2 * LANES] + r[:, d + 2 * LANES:]

    for g in range(x_ref.shape[1] // PERM_GROUP):
        rows = slice(g * PERM_GROUP, (g + 1) * PERM_GROUP)
        o0 = o0_ref[0, rows, :].astype(f32)
        l0 = l0_ref[0, rows, :]
        o1, l1 = natural(o1_ref, l1_ref, g)
        o2, l2 = natural(o2_ref, l2_ref, g)
        m = jnp.maximum(jnp.maximum(l0, l1), l2)
        e0, e1, e2 = jnp.exp(l0 - m), jnp.exp(l1 - m), jnp.exp(l2 - m)
        den = e0 + e1 + e2

        def spread(e):
            return jnp.dot(jnp.concatenate(_split_bf16(e / den, 2), axis=1), ee, preferred_element_type=f32)

        a = (o0 + spread(e1) * (o1 - o0) + spread(e2) * (o2 - o0)).astype(jnp.bfloat16)
        y_ref[0, rows, :] = x_ref[0, rows, :] + jnp.dot(a, w_ref[...], preferred_element_type=f32)


def _merge_out(x3, o0, l0, o1, l1, o2, l2, w, *, groups=4):
    B, S, D = x3.shape
    ng = S // PERM_GROUP
    tm = groups * PERM_GROUP
    pview = lambda a: a.reshape(B, 4, ng, PERM_GROUP // 4, a.shape[2])
    nat = lambda c: pl.BlockSpec((1, tm, c), lambda b, j: (b, j, 0))
    per = lambda c: pl.BlockSpec((1, 4, groups, PERM_GROUP // 4, c), lambda b, j: (b, 0, j, 0, 0))
    full = lambda shape: pl.BlockSpec(shape, lambda b, j: (0, 0))

    pm = _group_perm_matrix()
    ee = np.zeros((2 * LANES, D_MODEL), np.float32)
    for h in range(N_HEADS):
        ee[h, h * HEAD_DIM:(h + 1) * HEAD_DIM] = 1.0
        ee[LANES + h, h * HEAD_DIM:(h + 1) * HEAD_DIM] = 1.0

    return pl.pallas_call(
        _merge_out_kernel,
        grid=(B, ng // groups),
        in_specs=[nat(D), nat(D), nat(LANES), per(D), per(LANES), per(D), per(LANES),
                  full((PERM_GROUP, PERM_GROUP)), full((2 * LANES, D)), full((D, D))],
        out_specs=nat(D),
        out_shape=jax.ShapeDtypeStruct((B, S, D), jnp.float32),
        compiler_params=_cparams(("parallel", "parallel")),
        name="merge_out",
    )(x3, o0, l0, pview(o1), pview(l1), pview(o2), pview(l2),
      jnp.asarray(pm, jnp.bfloat16), jnp.asarray(ee, jnp.bfloat16), w)


def _out_kernel(x_ref, a_ref, w_ref, y_ref):
    y_ref[...] = x_ref[...] + jnp.dot(a_ref[...], w_ref[...], preferred_element_type=jnp.float32)


def _out_proj(x2, a, w, *, tm=1024):
    T, D = x2.shape
    row = pl.BlockSpec((tm, D), lambda i: (i, 0))
    return pl.pallas_call(
        _out_kernel,
        grid=(T // tm,),
        in_specs=[row, row, pl.BlockSpec((D, D), lambda i: (0, 0))],
        out_specs=row,
        out_shape=jax.ShapeDtypeStruct((T, D), jnp.float32),
        compiler_params=_cparams(("parallel",)),
        name="out_proj",
    )(x2, a, w)


B_QROWS = 4
B_KROWS = 12
NA_KH = 8
NA_KW = 16


def _b_slab_index(rows):
    units = rows // B_QROWS
    idx = np.zeros((3, B_QROWS, B_KROWS), np.int32)
    for geo, u in enumerate((0, 1, units - 1)):
        kr0 = min(max(u * B_QROWS - NA_KH // 2, 0), rows - B_KROWS)
        for a in range(B_QROWS):
            qr = u * B_QROWS + a
            rs = min(max(qr - NA_KH // 2, 0), rows - NA_KH)
            for c in range(B_KROWS):
                kr = kr0 + c
                idx[geo, a, c] = kr - qr + NA_KH - 1 if rs <= kr < rs + NA_KH else 2 * NA_KH - 1
    return idx


def _attn_b_kernel(q_ref, k_ref, v_ref, slab_ref, o_ref, tbl_ref, *, rows):
    is_a = _lane_is_first_head()
    nq = B_QROWS * GRID_W
    nk = B_KROWS * GRID_W
    units = rows // B_QROWS
    idx = _b_slab_index(rows)

    @pl.when(pl.program_id(1) == 0)
    def _():
        for h in range(2):
            for geo in range(3):
                for qr in range(B_QROWS):
                    for j in range(B_KROWS // 2):
                        even = slab_ref[h, int(idx[geo, qr, 2 * j])]
                        odd = slab_ref[h, int(idx[geo, qr, 2 * j + 1])]
                        tbl_ref[h, geo, qr * GRID_W:(qr + 1) * GRID_W, j * LANES:(j + 1) * LANES] = (
                            jnp.where(is_a, even, odd))

    for u in range(units):
        qs = u * nq
        kr0 = min(max(u * B_QROWS - NA_KH // 2, 0), rows - B_KROWS)
        ks = kr0 * GRID_W
        geo = 0 if u == 0 else (2 if u == units - 1 else 1)
        qst = _stack_heads(q_ref[0, qs:qs + nq, :], is_a)
        kw = k_ref[0, ks:ks + nk, :]
        vw = v_ref[0, ks:ks + nk, :]
        s = lax.dot_general(qst, kw, NT_DIMS, preferred_element_type=jnp.float32)
        tbl = tbl_ref[:, geo].reshape(2 * nq, nk)
        s = jnp.where(tbl > 0.5 * NEG_INF, s + tbl, NEG_INF)
        pv, _, den = _softmax_pv(s, vw)
        o_ref[0, qs:qs + nq, :] = (_unstack(is_a, pv, nq) / _unstack(is_a, den, nq)).astype(o_ref.dtype)


def _attn_b(qkv, slabs, *, B, S):
    rows = S // GRID_W
    npair = D_MODEL // LANES
    nq = B_QROWS * GRID_W
    nk = B_KROWS * GRID_W

    def in_spec(j):
        return pl.BlockSpec((1, S, LANES), lambda p, b: (b, 0, j * npair + p))

    kern = functools.partial(_attn_b_kernel, rows=rows)
    return pl.pallas_call(
        kern,
        grid=(npair, B),
        in_specs=[in_spec(0), in_spec(1), in_spec(2),
                  pl.BlockSpec((2, 2 * NA_KH, GRID_W, LANES), lambda p, b: (p, 0, 0, 0))],
        out_specs=pl.BlockSpec((1, S, LANES), lambda p, b: (b, 0, p)),
        out_shape=jax.ShapeDtypeStruct((B, S, D_MODEL), jnp.bfloat16),
        scratch_shapes=[pltpu.VMEM((2, 3, nq, nk), jnp.float32)],
        compiler_params=_cparams(("parallel", "arbitrary")),
        name="attn_b",
    )(qkv, qkv, qkv, slabs)


def _b_bias_slabs(rpb):
    H = rpb.shape[0]
    qc = np.arange(GRID_W)[:, None]
    kc = (np.arange(LANES) % GRID_W)[None, :]
    cs = np.clip(qc - NA_KW // 2, 0, GRID_W - NA_KW)
    col_ok = (kc >= cs) & (kc < cs + NA_KW)
    dc = np.clip(kc - qc + NA_KW - 1, 0, 2 * NA_KW - 2)
    onehot = ((dc[None] == np.arange(2 * NA_KW - 1)[:, None, None]) & col_ok[None]).astype(np.float32)
    cval = jnp.einsum("hac,cqk->haqk", rpb, jnp.asarray(onehot), precision=lax.Precision.HIGHEST)
    slabs = jnp.where(jnp.asarray(col_ok)[None, None], cval * LOG2E, NEG_INF)
    return jnp.concatenate([slabs, jnp.full((H, 1, GRID_W, LANES), NEG_INF, jnp.float32)], axis=1)


def _attn_c_kernel(q_ref, k_ref, v_ref, o_ref):
    is_a = _lane_is_first_head()
    tq = C_SUB_ROWS
    for r0 in range(0, q_ref.shape[1], tq):
        rows = slice(r0, r0 + tq)
        q = q_ref[0, rows, :]
        qst = jnp.concatenate([_stack_heads(q[:, :LANES], is_a), _stack_heads(q[:, LANES:], is_a)], axis=0)
        s = lax.dot_general(qst, k_ref[0], NT_DIMS, preferred_element_type=jnp.float32)
        pv, _, den = _softmax_pv(s, v_ref[0])
        o_ref[0, rows, :LANES] = (_unstack(is_a, pv, tq) / _unstack(is_a, den, tq)).astype(o_ref.dtype)
        o_ref[0, rows, LANES:] = (_unstack(is_a, pv[2 * tq:], tq) / _unstack(is_a, den[2 * tq:], tq)).astype(o_ref.dtype)


C_SUB_ROWS = 128


def _attn_c(q, k2, v2, *, B, S, tq=2048):
    gw = 2 * LANES
    return pl.pallas_call(
        _attn_c_kernel,
        grid=(B, C_KV_HEADS, S // tq),
        in_specs=[
            pl.BlockSpec((1, tq, gw), lambda b, g, i: (b, i, g)),
            pl.BlockSpec((1, S, LANES), lambda b, g, i: (b, 0, g)),
            pl.BlockSpec((1, S, LANES), lambda b, g, i: (b, 0, g)),
        ],
        out_specs=pl.BlockSpec((1, tq, gw), lambda b, g, i: (b, i, g)),
        out_shape=jax.ShapeDtypeStruct((B, S, D_MODEL), jnp.bfloat16),
        compiler_params=_cparams(("parallel", "parallel", "parallel")),
        name="attn_c",
    )(q, k2, v2)


MLP_ROWS = 256


def _mlp_kernel(x_ref, g_ref, wu_ref, wd_ref, gf_ref, y_ref, h_ref, acc_ref, *, final_norm):
    f = pl.program_id(1)
    last = pl.num_programs(1) - 1
    tm = x_ref.shape[0]

    def weights():
        return wu_ref[...].astype(jnp.bfloat16), wd_ref[...].astype(jnp.bfloat16)

    def partial_out(rows, wu, wd):
        u = jnp.dot(h_ref[rows, :], wu, preferred_element_type=jnp.float32)
        r = jnp.maximum(u, 0.0)
        return jnp.dot((r * r).astype(jnp.bfloat16), wd, preferred_element_type=jnp.float32)

    @pl.when(f == 0)
    def _():
        wu, wd = weights()
        for r0 in range(0, tm, MLP_ROWS):
            rows = slice(r0, r0 + MLP_ROWS)
            h_ref[rows, :] = _rms(x_ref[rows, :], g_ref[...]).astype(jnp.bfloat16)
            acc_ref[rows, :] = partial_out(rows, wu, wd)

    @pl.when((f != 0) & (f != last))
    def _():
        acc_ref[...] += partial_out(slice(None), *weights())

    @pl.when(f == last)
    def _():
        wu, wd = weights()
        for r0 in range(0, tm, MLP_ROWS):
            rows = slice(r0, r0 + MLP_ROWS)
            y = x_ref[rows, :] + (acc_ref[rows, :] + partial_out(rows, wu, wd))
            if final_norm:
                y = _rms(y, gf_ref[...])
            y_ref[rows, :] = y


def _mlp(x2, g, wu, wd, gf, *, final_norm, tm=1024, tf=1024):
    T, D = x2.shape
    F = wu.shape[1]
    assert F // tf >= 2, "the kernel treats the first and the last hidden chunk separately"
    kern = functools.partial(_mlp_kernel, final_norm=final_norm)
    return pl.pallas_call(
        kern,
        grid=(T // tm, F // tf),
        in_specs=[
            pl.BlockSpec((tm, D), lambda i, f: (i, 0)),
            pl.BlockSpec((1, D), lambda i, f: (0, 0)),
            pl.BlockSpec((D, tf), lambda i, f: (0, f)),
            pl.BlockSpec((tf, D), lambda i, f: (f, 0)),
            pl.BlockSpec((1, D), lambda i, f: (0, 0)),
        ],
        out_specs=pl.BlockSpec((tm, D), lambda i, f: (i, 0)),
        out_shape=jax.ShapeDtypeStruct((T, D), jnp.float32),
        scratch_shapes=[pltpu.VMEM((tm, D), jnp.bfloat16), pltpu.VMEM((tm, D), jnp.float32)],
        compiler_params=_cparams(("parallel", "arbitrary")),
        name="mlp",
    )(x2, g, wu, wd, gf)


def _rope_angles(pos, dim):
    inv = 1.0 / (ROPE_THETA ** (jnp.arange(0, dim, 2, dtype=jnp.float32) / dim))
    return pos.astype(jnp.float32)[:, None] * inv[None, :]


def _rope_tables(ang):
    cos, sin = jnp.cos(ang), jnp.sin(ang)
    return jnp.tile(cos, (1, 4)), jnp.tile(jnp.concatenate([-sin, sin], axis=-1), (1, 2))


def kernel(x, l0_attn_norm, l0_w_in, l0_w_out, l0_mlp_norm, l0_w_up, l0_w_down, l1_attn_norm, l1_w_in, l1_rpb, l1_w_out, l1_mlp_norm, l1_w_up, l1_w_down, l2_attn_norm, l2_w_in, l2_q_norm, l2_k_norm, l2_w_out, l2_mlp_norm, l2_w_up, l2_w_down, l3_attn_norm, l3_w_in, l3_w_out, l3_mlp_norm, l3_w_up, l3_w_down, final_norm):
    B, S, D = x.shape
    bf = lambda w: w.astype(jnp.bfloat16)
    row = lambda g: g.reshape(1, -1).astype(jnp.float32)

    t = jnp.arange(S, dtype=jnp.int32)
    cos_a, sin_a = _rope_tables(_rope_angles(t, HEAD_DIM))
    cos_c, sin_c = _rope_tables(jnp.concatenate(
        [_rope_angles(t // GRID_W, HALF), _rope_angles(t % GRID_W, HALF)], axis=-1))

    x2 = x.reshape(B * S, D)

    perm = _token_perm(S)
    cos_p, sin_p = cos_a[perm], sin_a[perm]

    def layer_a(x2, attn_norm, w_in, w_out):
        gw = 3 * D_MODEL
        g = row(attn_norm)
        qkv0 = _proj(x2, g, w_in, cos_a, sin_a, rope=True, ncols=gw).reshape(B, S, -1)
        qkv12 = _proj(x2, g, w_in, cos_p, sin_p, rope=True, perm=True, col0=gw, tm=S).reshape(B, S, -1)
        o0, l0 = _attn_a01(qkv0, B=B, S=S, L=S, permuted=False, name="attn_a0")
        o1, l1 = _attn_a01(qkv12, B=B, S=S, L=S // 4, permuted=True, name="attn_a1")
        o2, l2 = _attn_a2(qkv12, B=B, S=S)
        return _merge_out(x2.reshape(B, S, D), o0, l0, o1, l1, o2, l2, bf(w_out)).reshape(B * S, D)

    def layer_b(x2, attn_norm, w_in, rpb, w_out):
        qkv = _proj(x2, row(attn_norm), w_in, cos_a, sin_a, rope=False).reshape(B, S, -1)
        a = _attn_b(qkv, _b_bias_slabs(rpb.astype(jnp.float32)), B=B, S=S)
        return _out_proj(x2, a.reshape(B * S, D), bf(w_out))

    def layer_c(x2, attn_norm, w_in, q_norm, k_norm, w_out):
        gain2 = lambda g: jnp.tile(g.astype(jnp.float32), 2).reshape(1, LANES)
        bd = jnp.kron(jnp.eye(2, dtype=jnp.float32),
                      jnp.full((HEAD_DIM, HEAD_DIM), 1.0 / HEAD_DIM, jnp.float32)).astype(jnp.bfloat16)
        q, k2, v2 = _proj_c(x2, row(attn_norm), bf(w_in), cos_c, sin_c, gain2(q_norm), gain2(k_norm), bd)
        a = _attn_c(q.reshape(B, S, -1), k2.reshape(B, S, -1), v2.reshape(B, S, -1), B=B, S=S)
        return _out_proj(x2, a.reshape(B * S, D), bf(w_out))

    fn = row(final_norm)
    x2 = layer_a(x2, l0_attn_norm, l0_w_in, l0_w_out)
    x2 = _mlp(x2, row(l0_mlp_norm), l0_w_up, l0_w_down, fn, final_norm=False)
    x2 = layer_b(x2, l1_attn_norm, l1_w_in, l1_rpb, l1_w_out)
    x2 = _mlp(x2, row(l1_mlp_norm), l1_w_up, l1_w_down, fn, final_norm=False)
    x2 = layer_c(x2, l2_attn_norm, l2_w_in, l2_q_norm, l2_k_norm, l2_w_out)
    x2 = _mlp(x2, row(l2_mlp_norm), l2_w_up, l2_w_down, fn, final_norm=False)
    x2 = layer_a(x2, l3_attn_norm, l3_w_in, l3_w_out)
    x2 = _mlp(x2, row(l3_mlp_norm), l3_w_up, l3_w_down, fn, final_norm=True)
    return x2.reshape(B, S, D)
```

```python
import functools

import jax
import jax.numpy as jnp
import numpy as np
from jax import lax
from jax.experimental import pallas as pl
from jax.experimental.pallas import tpu as pltpu

D_MODEL = 1024
HEAD_DIM = 64
N_HEADS = 16
D_FF = 4 * D_MODEL
ROPE_THETA = 10000.0
RMS_EPS = 1e-6
NEG_INF = -1e30
GRID_W = 64
A_GROUPS = ((128, 1), (512, 4), (2048, 16))
C_KV_HEADS = 4
LOG2E = 1.4426950408889634
LN2 = 0.6931471805599453
QK_SCALE = HEAD_DIM ** -0.5 * LOG2E

LANES = 128
HALF = HEAD_DIM // 2
VMEM_LIMIT = 56 * 1024 * 1024

NT_DIMS = (((1,), (1,)), ((), ()))


def _cparams(sem):
    return pltpu.CompilerParams(dimension_semantics=sem, vmem_limit_bytes=VMEM_LIMIT)


def _rms(x, g):
    ms = jnp.mean(x * x, axis=-1, keepdims=True)
    return (x * lax.rsqrt(ms + RMS_EPS)) * g


def _lane_is_first_head():
    return lax.broadcasted_iota(jnp.int32, (1, LANES), 1) < HEAD_DIM


def _rope128(y, cos, sin_signed):
    lane = lax.broadcasted_iota(jnp.int32, (1, LANES), 1)
    first_half = (lane % HEAD_DIM) < HALF
    partner = jnp.where(first_half, pltpu.roll(y, LANES - HALF, 1), pltpu.roll(y, HALF, 1))
    return y * cos + partner * sin_signed


PERM_GROUP = 256
PROJ_ROWS = 256


def _token_perm(S):
    t = np.arange(S).reshape(S // PERM_GROUP, 16, 4, 4)
    return t.transpose(3, 0, 2, 1).reshape(S)


def _group_perm_matrix():
    token = np.arange(PERM_GROUP).reshape(16, 4, 4).transpose(2, 1, 0).reshape(-1)
    pm = np.zeros((PERM_GROUP, PERM_GROUP), np.float32)
    pm[token, np.arange(PERM_GROUP)] = 1.0
    return pm


def _proj_kernel(x_ref, g_ref, w_ref, cos_ref, sin_ref, pm_ref, o_ref, h_ref, *, rope, perm):
    n = pl.program_id(1)
    tm = x_ref.shape[0]
    tn = o_ref.shape[1]

    def normalise():
        g = g_ref[...]
        if perm:
            ng = tm // PERM_GROUP
            run = PERM_GROUP // 4
            for grp in range(ng):
                hn = _rms(x_ref[grp * PERM_GROUP:(grp + 1) * PERM_GROUP, :], g).astype(jnp.bfloat16)
                hp = jnp.dot(pm_ref[...], hn, preferred_element_type=jnp.float32).astype(jnp.bfloat16)
                for rho in range(4):
                    dst = (rho * ng + grp) * run
                    h_ref[dst:dst + run, :] = hp[rho * run:(rho + 1) * run, :]
        else:
            for r0 in range(0, tm, PROJ_ROWS):
                h_ref[r0:r0 + PROJ_ROWS, :] = _rms(x_ref[r0:r0 + PROJ_ROWS, :], g).astype(jnp.bfloat16)

    sec = (n * tn // D_MODEL) % 3

    def sub_blocks(epilogue, scale):
        w = w_ref[...].astype(jnp.bfloat16)
        for r0 in range(0, tm, PROJ_ROWS):
            rows = slice(r0, r0 + PROJ_ROWS)
            y = jnp.dot(h_ref[rows, :], w, preferred_element_type=jnp.float32)
            epilogue(rows, y, scale)

    def plain(rows, y, scale):
        o_ref[rows, :] = (y * scale).astype(o_ref.dtype)

    def roped(rows, y, scale):
        cos = cos_ref[rows, :]
        sin = sin_ref[rows, :]
        for c in range(tn // LANES):
            sl = slice(c * LANES, (c + 1) * LANES)
            o_ref[rows, sl] = (_rope128(y[:, sl], cos, sin) * scale).astype(o_ref.dtype)

    qk = roped if rope else plain

    @pl.when(n == 0)
    def _():
        normalise()
        sub_blocks(qk, QK_SCALE)

    @pl.when((n != 0) & (sec == 0))
    def _():
        sub_blocks(qk, QK_SCALE)

    @pl.when(sec == 1)
    def _():
        sub_blocks(qk, 1.0)

    @pl.when(sec == 2)
    def _():
        sub_blocks(plain, 1.0)


def _proj(x2, g, w, cos, sin, *, rope, perm=False, col0=0, ncols=None, tm=2048, tn=1024):
    T, D = x2.shape
    N = w.shape[1] - col0 if ncols is None else ncols
    S = cos.shape[0]
    nsb = S // tm
    cb0 = col0 // tn
    kern = functools.partial(_proj_kernel, rope=rope, perm=perm)
    return pl.pallas_call(
        kern,
        grid=(T // tm, N // tn),
        in_specs=[
            pl.BlockSpec((tm, D), lambda i, n: (i, 0)),
            pl.BlockSpec((1, D), lambda i, n: (0, 0)),
            pl.BlockSpec((D, tn), lambda i, n: (0, cb0 + n)),
            pl.BlockSpec((tm, LANES), lambda i, n: (i % nsb, 0)),
            pl.BlockSpec((tm, LANES), lambda i, n: (i % nsb, 0)),
            pl.BlockSpec((PERM_GROUP, PERM_GROUP), lambda i, n: (0, 0)),
        ],
        out_specs=pl.BlockSpec((tm, tn), lambda i, n: (i, n)),
        out_shape=jax.ShapeDtypeStruct((T, N), jnp.bfloat16),
        scratch_shapes=[pltpu.VMEM((tm, D), jnp.bfloat16)],
        compiler_params=_cparams(("parallel", "arbitrary")),
        name="proj_perm" if perm else "proj",
    )(x2, g, w, cos, sin, jnp.asarray(_group_perm_matrix().T, jnp.bfloat16))


def _proj_c_kernel(x_ref, g_ref, w_ref, cos_ref, sin_ref, qg_ref, kg_ref, bd_ref,
                   q_ref, k_ref, v_ref):
    bd = bd_ref[...]
    is_a = _lane_is_first_head()
    nq = D_MODEL // LANES
    nkv = C_KV_HEADS * HEAD_DIM // LANES

    for r0 in range(0, x_ref.shape[0], PROJ_ROWS):
        rows = slice(r0, r0 + PROJ_ROWS)
        h = _rms(x_ref[rows, :], g_ref[...]).astype(jnp.bfloat16)
        y = jnp.dot(h, w_ref[...], preferred_element_type=jnp.float32)
        cos = cos_ref[rows, :]
        sin = sin_ref[rows, :]

        def head_norm(c, gain):
            yc = y[:, c * LANES:(c + 1) * LANES]
            sq = yc * yc
            hi = sq.astype(jnp.bfloat16)
            lo = (sq - hi.astype(jnp.float32)).astype(jnp.bfloat16)
            ms = (jnp.dot(hi, bd, preferred_element_type=jnp.float32)
                  + jnp.dot(lo, bd, preferred_element_type=jnp.float32))
            return (yc * lax.rsqrt(ms + RMS_EPS)) * gain

        for c in range(nq):
            qn = head_norm(c, qg_ref[...])
            q_ref[rows, c * LANES:(c + 1) * LANES] = (_rope128(qn, cos, sin) * QK_SCALE).astype(q_ref.dtype)
        for c in range(nkv):
            kn = _rope128(head_norm(nq + c, kg_ref[...]), cos, sin)
            ksw = pltpu.roll(kn, HEAD_DIM, 1)
            k_ref[rows, (2 * c) * LANES:(2 * c + 1) * LANES] = jnp.where(is_a, kn, ksw).astype(k_ref.dtype)
            k_ref[rows, (2 * c + 1) * LANES:(2 * c + 2) * LANES] = jnp.where(is_a, ksw, kn).astype(k_ref.dtype)
            vc = y[:, (nq + nkv + c) * LANES:(nq + nkv + c + 1) * LANES]
            vsw = pltpu.roll(vc, HEAD_DIM, 1)
            v_ref[rows, (2 * c) * LANES:(2 * c + 1) * LANES] = jnp.where(is_a, vc, vsw).astype(v_ref.dtype)
            v_ref[rows, (2 * c + 1) * LANES:(2 * c + 2) * LANES] = jnp.where(is_a, vsw, vc).astype(v_ref.dtype)


def _proj_c(x2, g, w, cos, sin, qg, kg, bd, *, tm=1024):
    T, D = x2.shape
    N = w.shape[1]
    S = cos.shape[0]
    nsb = S // tm
    kvw = 2 * C_KV_HEADS * HEAD_DIM
    full = lambda shape: pl.BlockSpec(shape, lambda i: (0, 0))
    return pl.pallas_call(
        _proj_c_kernel,
        grid=(T // tm,),
        in_specs=[
            pl.BlockSpec((tm, D), lambda i: (i, 0)),
            full((1, D)),
            full((D, N)),
            pl.BlockSpec((tm, LANES), lambda i: (i % nsb, 0)),
            pl.BlockSpec((tm, LANES), lambda i: (i % nsb, 0)),
            full((1, LANES)),
            full((1, LANES)),
            full((LANES, LANES)),
        ],
        out_specs=[
            pl.BlockSpec((tm, D_MODEL), lambda i: (i, 0)),
            pl.BlockSpec((tm, kvw), lambda i: (i, 0)),
            pl.BlockSpec((tm, kvw), lambda i: (i, 0)),
        ],
        out_shape=[
            jax.ShapeDtypeStruct((T, D_MODEL), jnp.bfloat16),
            jax.ShapeDtypeStruct((T, kvw), jnp.bfloat16),
            jax.ShapeDtypeStruct((T, kvw), jnp.bfloat16),
        ],
        compiler_params=_cparams(("parallel",)),
        name="proj_c",
    )(x2, g, w, cos, sin, qg, kg, bd)


def _stack_heads(q2, is_a):
    zero = jnp.zeros_like(q2)
    return jnp.concatenate([jnp.where(is_a, q2, zero), jnp.where(is_a, zero, q2)], axis=0)


def _softmax_pv(s, vw, mxu_sums=True):
    m = jnp.max(s, axis=-1, keepdims=True)
    if not mxu_sums:
        p = jnp.exp2(s - m)
        den = jnp.sum(p, axis=-1, keepdims=True)
        return jnp.dot(p.astype(vw.dtype), vw, preferred_element_type=jnp.float32), m, den
    p = jnp.exp2((s - m).astype(jnp.bfloat16))
    vaug = jnp.concatenate([vw, jnp.ones_like(vw)], axis=1)
    r = jnp.dot(p, vaug, preferred_element_type=jnp.float32)
    return r[:, :LANES], m, r[:, LANES:]


def _unstack(is_a, x, n):
    return jnp.where(is_a, x[:n], x[n:2 * n])


A_HALF = 64
A_QB = 128
NPAIR = D_MODEL // LANES


def _band_pairs(load_q, load_k, load_v, valid, store_o, qb, mxu_sums=True):
    is_a = _lane_is_first_head()
    lane = lax.broadcasted_iota(jnp.int32, (1, LANES), 1)
    m_tile = jnp.zeros((qb, LANES), jnp.float32)
    den_tile = jnp.ones((qb, LANES), jnp.float32)
    for p in range(NPAIR):
        cols = slice(p * LANES, (p + 1) * LANES)
        qst = _stack_heads(load_q(cols), is_a)
        vw = load_v(cols)
        s = lax.dot_general(qst, load_k(cols), NT_DIMS, preferred_element_type=jnp.float32)
        s = jnp.where(valid, s, NEG_INF)
        pv, m, den = _softmax_pv(s, vw, mxu_sums)
        store_o(cols, _unstack(is_a, pv, qb) / _unstack(is_a, den, qb))
        in_a, in_b = lane == 2 * p, lane == 2 * p + 1
        m_tile = jnp.where(in_a, m[:qb], jnp.where(in_b, m[qb:], m_tile))
        den_tile = jnp.where(in_a, den[:qb], jnp.where(in_b, den[qb:], den_tile))
    return m_tile * LN2 + jnp.log(den_tile)


def _attn_a_kernel(q_ref, k_ref, v_ref, o_ref, lse_ref, mask_ref, *, L, permuted):
    qb, W = A_QB, 2 * A_QB
    nq = L // qb

    def pos(l):
        return (l // 64) * 64 + 4 * (l % 16) + (l % 64) // 16 if permuted else l

    rel = (pos(lax.broadcasted_iota(jnp.int32, (2 * qb, W), 0) % qb)
           - pos(lax.broadcasted_iota(jnp.int32, (2 * qb, W), 1)))
    for case, delta in enumerate((0, A_HALF, W - qb)):
        mask_ref[case] = (jnp.abs(rel + delta) <= A_HALF).astype(jnp.int32)

    def body(iq, carry):
        qs = pl.multiple_of(iq * qb, qb)
        ks = pl.multiple_of(jnp.clip(qs - A_HALF, 0, L - W), A_HALF)
        case = jnp.where(iq == 0, 0, jnp.where(iq == nq - 1, 2, 1))
        valid = mask_ref[case] != 0

        def store_o(cols, o):
            o_ref[0, pl.ds(qs, qb), cols] = o.astype(o_ref.dtype)

        lse_ref[0, pl.ds(qs, qb), :] = _band_pairs(
            lambda cols: q_ref[0, pl.ds(qs, qb), cols],
            lambda cols: k_ref[0, pl.ds(ks, W), cols],
            lambda cols: v_ref[0, pl.ds(ks, W), cols],
            valid, store_o, qb)
        return carry

    lax.fori_loop(0, nq, body, 0, unroll=min(nq, 8))


def _attn_a01(qkv, *, B, S, L, permuted, name):
    runs = S // L
    spec = lambda j: pl.BlockSpec((1, L, D_MODEL), lambda b, r: (b, r, j))
    kern = functools.partial(_attn_a_kernel, L=L, permuted=permuted)
    return pl.pallas_call(
        kern,
        grid=(B, runs),
        in_specs=[spec(0), spec(1), spec(2)],
        out_specs=[pl.BlockSpec((1, L, D_MODEL), lambda b, r: (b, r, 0)),
                   pl.BlockSpec((1, L, LANES), lambda b, r: (b, r, 0))],
        out_shape=[jax.ShapeDtypeStruct((B, S, D_MODEL), jnp.bfloat16),
                   jax.ShapeDtypeStruct((B, S, LANES), jnp.float32)],
        scratch_shapes=[pltpu.VMEM((3, 2 * A_QB, 2 * A_QB), jnp.int32)],
        compiler_params=_cparams(("parallel", "parallel")),
        name=name,
    )(qkv, qkv, qkv)


def _attn_a2_kernel(q_ref, k_ref, v_ref, o_ref, lse_ref):
    ng = q_ref.shape[2]
    L = ng * 16
    row = lax.broadcasted_iota(jnp.int32, (2 * L, L), 0) % L
    col = lax.broadcasted_iota(jnp.int32, (2 * L, L), 1)
    valid = jnp.abs(row - col) <= A_HALF

    for a in range(4):
        def load(ref, a=a):
            return lambda cols: ref[0, 0, :, a, :, cols].reshape(L, LANES)

        def store_o(cols, o, a=a):
            o_ref[0, 0, :, a, :, cols] = o.astype(o_ref.dtype).reshape(ng, 16, LANES)

        lse = _band_pairs(load(q_ref), load(k_ref), load(v_ref), valid, store_o, L, mxu_sums=False)
        lse_ref[0, 0, :, a, :, :] = lse.reshape(ng, 16, LANES)


def _attn_a2(qkv, *, B, S):
    ng = S // PERM_GROUP
    view = qkv.reshape(B, 4, ng, 4, 16, qkv.shape[2])
    blk = lambda c: (1, 1, ng, 4, 16, c)
    spec = lambda j: pl.BlockSpec(blk(D_MODEL), lambda b, r: (b, r, 0, 0, 0, 3 + j))
    o, lse = pl.pallas_call(
        _attn_a2_kernel,
        grid=(B, 4),
        in_specs=[spec(0), spec(1), spec(2)],
        out_specs=[pl.BlockSpec(blk(D_MODEL), lambda b, r: (b, r, 0, 0, 0, 0)),
                   pl.BlockSpec(blk(LANES), lambda b, r: (b, r, 0, 0, 0, 0))],
        out_shape=[jax.ShapeDtypeStruct((B, 4, ng, 4, 16, D_MODEL), jnp.bfloat16),
                   jax.ShapeDtypeStruct((B, 4, ng, 4, 16, LANES), jnp.float32)],
        compiler_params=_cparams(("parallel", "parallel")),
        name="attn_a2",
    )(view, view, view)
    return o.reshape(B, S, D_MODEL), lse.reshape(B, S, LANES)


def _split_bf16(v, parts):
    out = []
    for _ in range(parts - 1):
        hi = v.astype(jnp.bfloat16)
        out.append(hi)
        v = v - hi.astype(jnp.float32)
    out.append(v.astype(jnp.bfloat16))
    return out


def _merge_out_kernel(x_ref, o0_ref, l0_ref, o1_ref, l1_ref, o2_ref, l2_ref, pm_ref, ee_ref, w_ref, y_ref):
    pm = pm_ref[...]
    ee = ee_ref[...]
    f32 = jnp.float32

    def natural(o_ref, l_ref, g):
        lse = l_ref[0, :, g].reshape(PERM_GROUP, LANES)
        both = jnp.concatenate([o_ref[0, :, g].reshape(PERM_GROUP, D_MODEL)] + _split_bf16(lse, 3), axis=1)
        r = jnp.dot(pm, both, preferred_element_type=f32)
        d = D_MODEL
        return r[:, :d], r[:, d:d + LANES] + r[:, d + LANES:d + 2 * LANES] + r[:, d + 2 * LANES:]

    for g in range(x_ref.shape[1] // PERM_GROUP):
        rows = slice(g * PERM_GROUP, (g + 1) * PERM_GROUP)
        o0 = o0_ref[0, rows, :].astype(f32)
        l0 = l0_ref[0, rows, :]
        o1, l1 = natural(o1_ref, l1_ref, g)
        o2, l2 = natural(o2_ref, l2_ref, g)
        m = jnp.maximum(jnp.maximum(l0, l1), l2)
        e0, e1, e2 = jnp.exp(l0 - m), jnp.exp(l1 - m), jnp.exp(l2 - m)
        den = e0 + e1 + e2

        def spread(e):
            return jnp.dot(jnp.concatenate(_split_bf16(e / den, 2), axis=1), ee, preferred_element_type=f32)

        a = (o0 + spread(e1) * (o1 - o0) + spread(e2) * (o2 - o0)).astype(jnp.bfloat16)
        y_ref[0, rows, :] = x_ref[0, rows, :] + jnp.dot(a, w_ref[...], preferred_element_type=f32)


def _merge_out(x3, o0, l0, o1, l1, o2, l2, w, *, groups=4):
    B, S, D = x3.shape
    ng = S // PERM_GROUP
    tm = groups * PERM_GROUP
    pview = lambda a: a.reshape(B, 4, ng, PERM_GROUP // 4, a.shape[2])
    nat = lambda c: pl.BlockSpec((1, tm, c), lambda b, j: (b, j, 0))
    per = lambda c: pl.BlockSpec((1, 4, groups, PERM_GROUP // 4, c), lambda b, j: (b, 0, j, 0, 0))
    full = lambda shape: pl.BlockSpec(shape, lambda b, j: (0, 0))

    pm = _group_perm_matrix()
    ee = np.zeros((2 * LANES, D_MODEL), np.float32)
    for h in range(N_HEADS):
        ee[h, h * HEAD_DIM:(h + 1) * HEAD_DIM] = 1.0
        ee[LANES + h, h * HEAD_DIM:(h + 1) * HEAD_DIM] = 1.0

    return pl.pallas_call(
        _merge_out_kernel,
        grid=(B, ng // groups),
        in_specs=[nat(D), nat(D), nat(LANES), per(D), per(LANES), per(D), per(LANES),
                  full((PERM_GROUP, PERM_GROUP)), full((2 * LANES, D)), full((D, D))],
        out_specs=nat(D),
        out_shape=jax.ShapeDtypeStruct((B, S, D), jnp.float32),
        compiler_params=_cparams(("parallel", "parallel")),
        name="merge_out",
    )(x3, o0, l0, pview(o1), pview(l1), pview(o2), pview(l2),
      jnp.asarray(pm, jnp.bfloat16), jnp.asarray(ee, jnp.bfloat16), w)


def _out_kernel(x_ref, a_ref, w_ref, y_ref):
    y_ref[...] = x_ref[...] + jnp.dot(a_ref[...], w_ref[...], preferred_element_type=jnp.float32)


def _out_proj(x2, a, w, *, tm=1024):
    T, D = x2.shape
    row = pl.BlockSpec((tm, D), lambda i: (i, 0))
    return pl.pallas_call(
        _out_kernel,
        grid=(T // tm,),
        in_specs=[row, row, pl.BlockSpec((D, D), lambda i: (0, 0))],
        out_specs=row,
        out_shape=jax.ShapeDtypeStruct((T, D), jnp.float32),
        compiler_params=_cparams(("parallel",)),
        name="out_proj",
    )(x2, a, w)


B_QROWS = 4
B_KROWS = 12
B_SUB = 128
NA_KH = 8
NA_KW = 16


def _b_slab_index(rows):
    units = rows // B_QROWS
    idx = np.zeros((3, B_QROWS, B_KROWS), np.int32)
    for geo, u in enumerate((0, 1, units - 1)):
        kr0 = min(max(u * B_QROWS - NA_KH // 2, 0), rows - B_KROWS)
        for a in range(B_QROWS):
            qr = u * B_QROWS + a
            rs = min(max(qr - NA_KH // 2, 0), rows - NA_KH)
            for c in range(B_KROWS):
                kr = kr0 + c
                idx[geo, a, c] = kr - qr + NA_KH - 1 if rs <= kr < rs + NA_KH else 2 * NA_KH - 1
    return idx


def _attn_b_kernel(q_ref, k_ref, v_ref, slab_ref, o_ref, tbl_ref, *, rows):
    is_a = _lane_is_first_head()
    nq = B_QROWS * GRID_W
    nk = B_KROWS * GRID_W
    units = rows // B_QROWS
    idx = _b_slab_index(rows)

    @pl.when(pl.program_id(1) == 0)
    def _():
        for h in range(2):
            for geo in range(3):
                for qr in range(B_QROWS):
                    for j in range(B_KROWS // 2):
                        even = slab_ref[h, int(idx[geo, qr, 2 * j])]
                        odd = slab_ref[h, int(idx[geo, qr, 2 * j + 1])]
                        tbl_ref[h, geo, qr * GRID_W:(qr + 1) * GRID_W, j * LANES:(j + 1) * LANES] = (
                            jnp.where(is_a, even, odd))

    for u in range(units):
        qs = u * nq
        kr0 = min(max(u * B_QROWS - NA_KH // 2, 0), rows - B_KROWS)
        ks = kr0 * GRID_W
        geo = 0 if u == 0 else (2 if u == units - 1 else 1)
        kw = k_ref[0, ks:ks + nk, :]
        vw = v_ref[0, ks:ks + nk, :]
        for r0 in range(0, nq, B_SUB):
            qst = _stack_heads(q_ref[0, qs + r0:qs + r0 + B_SUB, :], is_a)
            s = lax.dot_general(qst, kw, NT_DIMS, preferred_element_type=jnp.float32)
            tbl = tbl_ref[:, geo, r0:r0 + B_SUB, :].reshape(2 * B_SUB, nk)
            s = jnp.where(tbl > 0.5 * NEG_INF, s + tbl, NEG_INF)
            pv, _, den = _softmax_pv(s, vw)
            o_ref[0, qs + r0:qs + r0 + B_SUB, :] = (
                _unstack(is_a, pv, B_SUB) / _unstack(is_a, den, B_SUB)).astype(o_ref.dtype)


def _attn_b(qkv, slabs, *, B, S):
    rows = S // GRID_W
    npair = D_MODEL // LANES
    nq = B_QROWS * GRID_W
    nk = B_KROWS * GRID_W

    def in_spec(j):
        return pl.BlockSpec((1, S, LANES), lambda p, b: (b, 0, j * npair + p))

    kern = functools.partial(_attn_b_kernel, rows=rows)
    return pl.pallas_call(
        kern,
        grid=(npair, B),
        in_specs=[in_spec(0), in_spec(1), in_spec(2),
                  pl.BlockSpec((2, 2 * NA_KH, GRID_W, LANES), lambda p, b: (p, 0, 0, 0))],
        out_specs=pl.BlockSpec((1, S, LANES), lambda p, b: (b, 0, p)),
        out_shape=jax.ShapeDtypeStruct((B, S, D_MODEL), jnp.bfloat16),
        scratch_shapes=[pltpu.VMEM((2, 3, nq, nk), jnp.float32)],
        compiler_params=_cparams(("parallel", "arbitrary")),
        name="attn_b",
    )(qkv, qkv, qkv, slabs)


def _b_bias_slabs(rpb):
    H = rpb.shape[0]
    qc = np.arange(GRID_W)[:, None]
    kc = (np.arange(LANES) % GRID_W)[None, :]
    cs = np.clip(qc - NA_KW // 2, 0, GRID_W - NA_KW)
    col_ok = (kc >= cs) & (kc < cs + NA_KW)
    dc = np.clip(kc - qc + NA_KW - 1, 0, 2 * NA_KW - 2)
    onehot = ((dc[None] == np.arange(2 * NA_KW - 1)[:, None, None]) & col_ok[None]).astype(np.float32)
    cval = jnp.einsum("hac,cqk->haqk", rpb, jnp.asarray(onehot), precision=lax.Precision.HIGHEST)
    slabs = jnp.where(jnp.asarray(col_ok)[None, None], cval * LOG2E, NEG_INF)
    return jnp.concatenate([slabs, jnp.full((H, 1, GRID_W, LANES), NEG_INF, jnp.float32)], axis=1)


def _attn_c_kernel(q_ref, k_ref, v_ref, o_ref):
    is_a = _lane_is_first_head()
    tq = C_SUB_ROWS
    for r0 in range(0, q_ref.shape[1], tq):
        rows = slice(r0, r0 + tq)
        q = q_ref[0, rows, :]
        qst = jnp.concatenate([_stack_heads(q[:, :LANES], is_a), _stack_heads(q[:, LANES:], is_a)], axis=0)
        s = lax.dot_general(qst, k_ref[0], NT_DIMS, preferred_element_type=jnp.float32)
        pv, _, den = _softmax_pv(s, v_ref[0])
        o_ref[0, rows, :LANES] = (_unstack(is_a, pv, tq) / _unstack(is_a, den, tq)).astype(o_ref.dtype)
        o_ref[0, rows, LANES:] = (_unstack(is_a, pv[2 * tq:], tq) / _unstack(is_a, den[2 * tq:], tq)).astype(o_ref.dtype)


C_SUB_ROWS = 64


def _attn_c(q, k2, v2, *, B, S, tq=2048):
    gw = 2 * LANES
    return pl.pallas_call(
        _attn_c_kernel,
        grid=(B, C_KV_HEADS, S // tq),
        in_specs=[
            pl.BlockSpec((1, tq, gw), lambda b, g, i: (b, i, g)),
            pl.BlockSpec((1, S, LANES), lambda b, g, i: (b, 0, g)),
            pl.BlockSpec((1, S, LANES), lambda b, g, i: (b, 0, g)),
        ],
        out_specs=pl.BlockSpec((1, tq, gw), lambda b, g, i: (b, i, g)),
        out_shape=jax.ShapeDtypeStruct((B, S, D_MODEL), jnp.bfloat16),
        compiler_params=_cparams(("parallel", "parallel", "parallel")),
        name="attn_c",
    )(q, k2, v2)


MLP_ROWS = 256


def _mlp_kernel(x_ref, g_ref, wu_ref, wd_ref, gf_ref, y_ref, h_ref, acc_ref, *, final_norm):
    f = pl.program_id(1)
    last = pl.num_programs(1) - 1
    tm = x_ref.shape[0]

    def weights():
        return wu_ref[...].astype(jnp.bfloat16), wd_ref[...].astype(jnp.bfloat16)

    def partial_out(rows, wu, wd):
        u = jnp.dot(h_ref[rows, :], wu, preferred_element_type=jnp.float32)
        r = jnp.maximum(u, 0.0)
        return jnp.dot((r * r).astype(jnp.bfloat16), wd, preferred_element_type=jnp.float32)

    @pl.when(f == 0)
    def _():
        wu, wd = weights()
        for r0 in range(0, tm, MLP_ROWS):
            rows = slice(r0, r0 + MLP_ROWS)
            h_ref[rows, :] = _rms(x_ref[rows, :], g_ref[...]).astype(jnp.bfloat16)
            acc_ref[rows, :] = partial_out(rows, wu, wd)

    @pl.when((f != 0) & (f != last))
    def _():
        acc_ref[...] += partial_out(slice(None), *weights())

    @pl.when(f == last)
    def _():
        wu, wd = weights()
        for r0 in range(0, tm, MLP_ROWS):
            rows = slice(r0, r0 + MLP_ROWS)
            y = x_ref[rows, :] + (acc_ref[rows, :] + partial_out(rows, wu, wd))
            if final_norm:
                y = _rms(y, gf_ref[...])
            y_ref[rows, :] = y


def _mlp(x2, g, wu, wd, gf, *, final_norm, tm=1024, tf=1024):
    T, D = x2.shape
    F = wu.shape[1]
    assert F // tf >= 2, "the kernel treats the first and the last hidden chunk separately"
    kern = functools.partial(_mlp_kernel, final_norm=final_norm)
    return pl.pallas_call(
        kern,
        grid=(T // tm, F // tf),
        in_specs=[
            pl.BlockSpec((tm, D), lambda i, f: (i, 0)),
            pl.BlockSpec((1, D), lambda i, f: (0, 0)),
            pl.BlockSpec((D, tf), lambda i, f: (0, f)),
            pl.BlockSpec((tf, D), lambda i, f: (f, 0)),
            pl.BlockSpec((1, D), lambda i, f: (0, 0)),
        ],
        out_specs=pl.BlockSpec((tm, D), lambda i, f: (i, 0)),
        out_shape=jax.ShapeDtypeStruct((T, D), jnp.float32),
        scratch_shapes=[pltpu.VMEM((tm, D), jnp.bfloat16), pltpu.VMEM((tm, D), jnp.float32)],
        compiler_params=_cparams(("parallel", "arbitrary")),
        name="mlp",
    )(x2, g, wu, wd, gf)


def _rope_angles(pos, dim):
    inv = 1.0 / (ROPE_THETA ** (jnp.arange(0, dim, 2, dtype=jnp.float32) / dim))
    return pos.astype(jnp.float32)[:, None] * inv[None, :]


def _rope_tables(ang):
    cos, sin = jnp.cos(ang), jnp.sin(ang)
    return jnp.tile(cos, (1, 4)), jnp.tile(jnp.concatenate([-sin, sin], axis=-1), (1, 2))


def kernel(x, l0_attn_norm, l0_w_in, l0_w_out, l0_mlp_norm, l0_w_up, l0_w_down, l1_attn_norm, l1_w_in, l1_rpb, l1_w_out, l1_mlp_norm, l1_w_up, l1_w_down, l2_attn_norm, l2_w_in, l2_q_norm, l2_k_norm, l2_w_out, l2_mlp_norm, l2_w_up, l2_w_down, l3_attn_norm, l3_w_in, l3_w_out, l3_mlp_norm, l3_w_up, l3_w_down, final_norm):
    B, S, D = x.shape
    bf = lambda w: w.astype(jnp.bfloat16)
    row = lambda g: g.reshape(1, -1).astype(jnp.float32)

    t = jnp.arange(S, dtype=jnp.int32)
    cos_a, sin_a = _rope_tables(_rope_angles(t, HEAD_DIM))
    cos_c, sin_c = _rope_tables(jnp.concatenate(
        [_rope_angles(t // GRID_W, HALF), _rope_angles(t % GRID_W, HALF)], axis=-1))

    x2 = x.reshape(B * S, D)

    perm = _token_perm(S)
    cos_p, sin_p = cos_a[perm], sin_a[perm]

    def layer_a(x2, attn_norm, w_in, w_out):
        gw = 3 * D_MODEL
        g = row(attn_norm)
        qkv0 = _proj(x2, g, w_in, cos_a, sin_a, rope=True, ncols=gw).reshape(B, S, -1)
        qkv12 = _proj(x2, g, w_in, cos_p, sin_p, rope=True, perm=True, col0=gw, tm=S).reshape(B, S, -1)
        o0, l0 = _attn_a01(qkv0, B=B, S=S, L=S, permuted=False, name="attn_a0")
        o1, l1 = _attn_a01(qkv12, B=B, S=S, L=S // 4, permuted=True, name="attn_a1")
        o2, l2 = _attn_a2(qkv12, B=B, S=S)
        return _merge_out(x2.reshape(B, S, D), o0, l0, o1, l1, o2, l2, bf(w_out)).reshape(B * S, D)

    def layer_b(x2, attn_norm, w_in, rpb, w_out):
        qkv = _proj(x2, row(attn_norm), w_in, cos_a, sin_a, rope=False).reshape(B, S, -1)
        a = _attn_b(qkv, _b_bias_slabs(rpb.astype(jnp.float32)), B=B, S=S)
        return _out_proj(x2, a.reshape(B * S, D), bf(w_out))

    def layer_c(x2, attn_norm, w_in, q_norm, k_norm, w_out):
        gain2 = lambda g: jnp.tile(g.astype(jnp.float32), 2).reshape(1, LANES)
        bd = jnp.kron(jnp.eye(2, dtype=jnp.float32),
                      jnp.full((HEAD_DIM, HEAD_DIM), 1.0 / HEAD_DIM, jnp.float32)).astype(jnp.bfloat16)
        q, k2, v2 = _proj_c(x2, row(attn_norm), bf(w_in), cos_c, sin_c, gain2(q_norm), gain2(k_norm), bd)
        a = _attn_c(q.reshape(B, S, -1), k2.reshape(B, S, -1), v2.reshape(B, S, -1), B=B, S=S)
        return _out_proj(x2, a.reshape(B * S, D), bf(w_out))

    fn = row(final_norm)
    x2 = layer_a(x2, l0_attn_norm, l0_w_in, l0_w_out)
    x2 = _mlp(x2, row(l0_mlp_norm), l0_w_up, l0_w_down, fn, final_norm=False)
    x2 = layer_b(x2, l1_attn_norm, l1_w_in, l1_rpb, l1_w_out)
    x2 = _mlp(x2, row(l1_mlp_norm), l1_w_up, l1_w_down, fn, final_norm=False)
    x2 = layer_c(x2, l2_attn_norm, l2_w_in, l2_q_norm, l2_k_norm, l2_w_out)
    x2 = _mlp(x2, row(l2_mlp_norm), l2_w_up, l2_w_down, fn, final_norm=False)
    x2 = layer_a(x2, l3_attn_norm, l3_w_in, l3_w_out)
    x2 = _mlp(x2, row(l3_mlp_norm), l3_w_up, l3_w_down, fn, final_norm=True)
    return x2.reshape(B, S, D)
```

```python
import functools

import jax
import jax.numpy as jnp
import numpy as np
from jax import lax
from jax.experimental import pallas as pl
from jax.experimental.pallas import tpu as pltpu

D_MODEL = 1024
HEAD_DIM = 64
N_HEADS = 16
D_FF = 4 * D_MODEL
ROPE_THETA = 10000.0
RMS_EPS = 1e-6
NEG_INF = -1e30
GRID_W = 64
A_GROUPS = ((128, 1), (512, 4), (2048, 16))
C_KV_HEADS = 4
LOG2E = 1.4426950408889634
LN2 = 0.6931471805599453
QK_SCALE = HEAD_DIM ** -0.5 * LOG2E

LANES = 128
HALF = HEAD_DIM // 2
VMEM_LIMIT = 56 * 1024 * 1024

NT_DIMS = (((1,), (1,)), ((), ()))


def _cparams(sem):
    return pltpu.CompilerParams(dimension_semantics=sem, vmem_limit_bytes=VMEM_LIMIT)


def _rms(x, g):
    ms = jnp.mean(x * x, axis=-1, keepdims=True)
    return (x * lax.rsqrt(ms + RMS_EPS)) * g


def _lane_is_first_head():
    return lax.broadcasted_iota(jnp.int32, (1, LANES), 1) < HEAD_DIM


def _rope128(y, cos, sin_signed):
    lane = lax.broadcasted_iota(jnp.int32, (1, LANES), 1)
    first_half = (lane % HEAD_DIM) < HALF
    partner = jnp.where(first_half, pltpu.roll(y, LANES - HALF, 1), pltpu.roll(y, HALF, 1))
    return y * cos + partner * sin_signed


PERM_GROUP = 256
PROJ_ROWS = 256


def _token_perm(S):
    t = np.arange(S).reshape(S // PERM_GROUP, 16, 4, 4)
    return t.transpose(3, 0, 2, 1).reshape(S)


def _group_perm_matrix():
    token = np.arange(PERM_GROUP).reshape(16, 4, 4).transpose(2, 1, 0).reshape(-1)
    pm = np.zeros((PERM_GROUP, PERM_GROUP), np.float32)
    pm[token, np.arange(PERM_GROUP)] = 1.0
    return pm


def _proj_kernel(x_ref, g_ref, w_ref, cos_ref, sin_ref, pm_ref, o_ref, h_ref, *, rope, perm):
    n = pl.program_id(1)
    tm = x_ref.shape[0]
    tn = o_ref.shape[1]

    def normalise():
        g = g_ref[...]
        if perm:
            ng = tm // PERM_GROUP
            run = PERM_GROUP // 4
            for grp in range(ng):
                hn = _rms(x_ref[grp * PERM_GROUP:(grp + 1) * PERM_GROUP, :], g).astype(jnp.bfloat16)
                hp = jnp.dot(pm_ref[...], hn, preferred_element_type=jnp.float32).astype(jnp.bfloat16)
                for rho in range(4):
                    dst = (rho * ng + grp) * run
                    h_ref[dst:dst + run, :] = hp[rho * run:(rho + 1) * run, :]
        else:
            for r0 in range(0, tm, PROJ_ROWS):
                h_ref[r0:r0 + PROJ_ROWS, :] = _rms(x_ref[r0:r0 + PROJ_ROWS, :], g).astype(jnp.bfloat16)

    sec = (n * tn // D_MODEL) % 3

    def sub_blocks(epilogue, scale):
        w = w_ref[...].astype(jnp.bfloat16)
        for r0 in range(0, tm, PROJ_ROWS):
            rows = slice(r0, r0 + PROJ_ROWS)
            y = jnp.dot(h_ref[rows, :], w, preferred_element_type=jnp.float32)
            epilogue(rows, y, scale)

    def plain(rows, y, scale):
        o_ref[rows, :] = (y * scale).astype(o_ref.dtype)

    def roped(rows, y, scale):
        cos = cos_ref[rows, :]
        sin = sin_ref[rows, :]
        for c in range(tn // LANES):
            sl = slice(c * LANES, (c + 1) * LANES)
            o_ref[rows, sl] = (_rope128(y[:, sl], cos, sin) * scale).astype(o_ref.dtype)

    qk = roped if rope else plain

    @pl.when(n == 0)
    def _():
        normalise()
        sub_blocks(qk, QK_SCALE)

    @pl.when((n != 0) & (sec == 0))
    def _():
        sub_blocks(qk, QK_SCALE)

    @pl.when(sec == 1)
    def _():
        sub_blocks(qk, 1.0)

    @pl.when(sec == 2)
    def _():
        sub_blocks(plain, 1.0)


def _proj(x2, g, w, cos, sin, *, rope, perm=False, col0=0, ncols=None, tm=2048, tn=1024):
    T, D = x2.shape
    N = w.shape[1] - col0 if ncols is None else ncols
    S = cos.shape[0]
    nsb = S // tm
    cb0 = col0 // tn
    kern = functools.partial(_proj_kernel, rope=rope, perm=perm)
    return pl.pallas_call(
        kern,
        grid=(T // tm, N // tn),
        in_specs=[
            pl.BlockSpec((tm, D), lambda i, n: (i, 0)),
            pl.BlockSpec((1, D), lambda i, n: (0, 0)),
            pl.BlockSpec((D, tn), lambda i, n: (0, cb0 + n)),
            pl.BlockSpec((tm, LANES), lambda i, n: (i % nsb, 0)),
            pl.BlockSpec((tm, LANES), lambda i, n: (i % nsb, 0)),
            pl.BlockSpec((PERM_GROUP, PERM_GROUP), lambda i, n: (0, 0)),
        ],
        out_specs=pl.BlockSpec((tm, tn), lambda i, n: (i, n)),
        out_shape=jax.ShapeDtypeStruct((T, N), jnp.bfloat16),
        scratch_shapes=[pltpu.VMEM((tm, D), jnp.bfloat16)],
        compiler_params=_cparams(("parallel", "arbitrary")),
        name="proj_perm" if perm else "proj",
    )(x2, g, w, cos, sin, jnp.asarray(_group_perm_matrix().T, jnp.bfloat16))


def _proj_c_kernel(x_ref, g_ref, w_ref, cos_ref, sin_ref, qg_ref, kg_ref, bd_ref,
                   q_ref, k_ref, v_ref):
    bd = bd_ref[...]
    is_a = _lane_is_first_head()
    nq = D_MODEL // LANES
    nkv = C_KV_HEADS * HEAD_DIM // LANES

    for r0 in range(0, x_ref.shape[0], PROJ_ROWS):
        rows = slice(r0, r0 + PROJ_ROWS)
        h = _rms(x_ref[rows, :], g_ref[...]).astype(jnp.bfloat16)
        y = jnp.dot(h, w_ref[...], preferred_element_type=jnp.float32)
        cos = cos_ref[rows, :]
        sin = sin_ref[rows, :]

        def head_norm(c, gain):
            yc = y[:, c * LANES:(c + 1) * LANES]
            sq = yc * yc
            hi = sq.astype(jnp.bfloat16)
            lo = (sq - hi.astype(jnp.float32)).astype(jnp.bfloat16)
            ms = (jnp.dot(hi, bd, preferred_element_type=jnp.float32)
                  + jnp.dot(lo, bd, preferred_element_type=jnp.float32))
            return (yc * lax.rsqrt(ms + RMS_EPS)) * gain

        for c in range(nq):
            qn = head_norm(c, qg_ref[...])
            q_ref[rows, c * LANES:(c + 1) * LANES] = (_rope128(qn, cos, sin) * QK_SCALE).astype(q_ref.dtype)
        for c in range(nkv):
            kn = _rope128(head_norm(nq + c, kg_ref[...]), cos, sin)
            ksw = pltpu.roll(kn, HEAD_DIM, 1)
            k_ref[rows, (2 * c) * LANES:(2 * c + 1) * LANES] = jnp.where(is_a, kn, ksw).astype(k_ref.dtype)
            k_ref[rows, (2 * c + 1) * LANES:(2 * c + 2) * LANES] = jnp.where(is_a, ksw, kn).astype(k_ref.dtype)
            vc = y[:, (nq + nkv + c) * LANES:(nq + nkv + c + 1) * LANES]
            vsw = pltpu.roll(vc, HEAD_DIM, 1)
            v_ref[rows, (2 * c) * LANES:(2 * c + 1) * LANES] = jnp.where(is_a, vc, vsw).astype(v_ref.dtype)
            v_ref[rows, (2 * c + 1) * LANES:(2 * c + 2) * LANES] = jnp.where(is_a, vsw, vc).astype(v_ref.dtype)


def _proj_c(x2, g, w, cos, sin, qg, kg, bd, *, tm=1024):
    T, D = x2.shape
    N = w.shape[1]
    S = cos.shape[0]
    nsb = S // tm
    kvw = 2 * C_KV_HEADS * HEAD_DIM
    full = lambda shape: pl.BlockSpec(shape, lambda i: (0, 0))
    return pl.pallas_call(
        _proj_c_kernel,
        grid=(T // tm,),
        in_specs=[
            pl.BlockSpec((tm, D), lambda i: (i, 0)),
            full((1, D)),
            full((D, N)),
            pl.BlockSpec((tm, LANES), lambda i: (i % nsb, 0)),
            pl.BlockSpec((tm, LANES), lambda i: (i % nsb, 0)),
            full((1, LANES)),
            full((1, LANES)),
            full((LANES, LANES)),
        ],
        out_specs=[
            pl.BlockSpec((tm, D_MODEL), lambda i: (i, 0)),
            pl.BlockSpec((tm, kvw), lambda i: (i, 0)),
            pl.BlockSpec((tm, kvw), lambda i: (i, 0)),
        ],
        out_shape=[
            jax.ShapeDtypeStruct((T, D_MODEL), jnp.bfloat16),
            jax.ShapeDtypeStruct((T, kvw), jnp.bfloat16),
            jax.ShapeDtypeStruct((T, kvw), jnp.bfloat16),
        ],
        compiler_params=_cparams(("parallel",)),
        name="proj_c",
    )(x2, g, w, cos, sin, qg, kg, bd)


def _stack_heads(q2, is_a):
    zero = jnp.zeros_like(q2)
    return jnp.concatenate([jnp.where(is_a, q2, zero), jnp.where(is_a, zero, q2)], axis=0)


def _softmax_pv(s, vw, mxu_sums=True):
    m = jnp.max(s, axis=-1, keepdims=True)
    if not mxu_sums:
        p = jnp.exp2(s - m)
        den = jnp.sum(p, axis=-1, keepdims=True)
        return jnp.dot(p.astype(vw.dtype), vw, preferred_element_type=jnp.float32), m, den
    p = jnp.exp2((s - m).astype(jnp.bfloat16))
    vaug = jnp.concatenate([vw, jnp.ones_like(vw)], axis=1)
    r = jnp.dot(p, vaug, preferred_element_type=jnp.float32)
    return r[:, :LANES], m, r[:, LANES:]


def _unstack(is_a, x, n):
    return jnp.where(is_a, x[:n], x[n:2 * n])


A_HALF = 64
A_QB = 128
NPAIR = D_MODEL // LANES


def _band_pairs(load_q, load_k, load_v, valid, store_o, qb, mxu_sums=True):
    is_a = _lane_is_first_head()
    lane = lax.broadcasted_iota(jnp.int32, (1, LANES), 1)
    m_tile = jnp.zeros((qb, LANES), jnp.float32)
    den_tile = jnp.ones((qb, LANES), jnp.float32)
    for p in range(NPAIR):
        cols = slice(p * LANES, (p + 1) * LANES)
        qst = _stack_heads(load_q(cols), is_a)
        vw = load_v(cols)
        s = lax.dot_general(qst, load_k(cols), NT_DIMS, preferred_element_type=jnp.float32)
        s = jnp.where(valid, s, NEG_INF)
        pv, m, den = _softmax_pv(s, vw, mxu_sums)
        store_o(cols, _unstack(is_a, pv, qb) / _unstack(is_a, den, qb))
        in_a, in_b = lane == 2 * p, lane == 2 * p + 1
        m_tile = jnp.where(in_a, m[:qb], jnp.where(in_b, m[qb:], m_tile))
        den_tile = jnp.where(in_a, den[:qb], jnp.where(in_b, den[qb:], den_tile))
    return m_tile * LN2 + jnp.log(den_tile)


def _attn_a_kernel(q_ref, k_ref, v_ref, o_ref, lse_ref, mask_ref, *, L, permuted):
    qb, W = A_QB, 2 * A_QB
    nq = L // qb

    def pos(l):
        return (l // 64) * 64 + 4 * (l % 16) + (l % 64) // 16 if permuted else l

    rel = (pos(lax.broadcasted_iota(jnp.int32, (2 * qb, W), 0) % qb)
           - pos(lax.broadcasted_iota(jnp.int32, (2 * qb, W), 1)))
    for case, delta in enumerate((0, A_HALF, W - qb)):
        mask_ref[case] = (jnp.abs(rel + delta) <= A_HALF).astype(jnp.int32)

    def body(iq, carry):
        qs = pl.multiple_of(iq * qb, qb)
        ks = pl.multiple_of(jnp.clip(qs - A_HALF, 0, L - W), A_HALF)
        case = jnp.where(iq == 0, 0, jnp.where(iq == nq - 1, 2, 1))
        valid = mask_ref[case] != 0

        def store_o(cols, o):
            o_ref[0, pl.ds(qs, qb), cols] = o.astype(o_ref.dtype)

        lse_ref[0, pl.ds(qs, qb), :] = _band_pairs(
            lambda cols: q_ref[0, pl.ds(qs, qb), cols],
            lambda cols: k_ref[0, pl.ds(ks, W), cols],
            lambda cols: v_ref[0, pl.ds(ks, W), cols],
            valid, store_o, qb)
        return carry

    lax.fori_loop(0, nq, body, 0, unroll=min(nq, 8))


def _attn_a01(qkv, *, B, S, L, permuted, name):
    runs = S // L
    spec = lambda j: pl.BlockSpec((1, L, D_MODEL), lambda b, r: (b, r, j))
    kern = functools.partial(_attn_a_kernel, L=L, permuted=permuted)
    return pl.pallas_call(
        kern,
        grid=(B, runs),
        in_specs=[spec(0), spec(1), spec(2)],
        out_specs=[pl.BlockSpec((1, L, D_MODEL), lambda b, r: (b, r, 0)),
                   pl.BlockSpec((1, L, LANES), lambda b, r: (b, r, 0))],
        out_shape=[jax.ShapeDtypeStruct((B, S, D_MODEL), jnp.bfloat16),
                   jax.ShapeDtypeStruct((B, S, LANES), jnp.float32)],
        scratch_shapes=[pltpu.VMEM((3, 2 * A_QB, 2 * A_QB), jnp.int32)],
        compiler_params=_cparams(("parallel", "parallel")),
        name=name,
    )(qkv, qkv, qkv)


def _attn_a2_kernel(q_ref, k_ref, v_ref, o_ref, lse_ref):
    ng = q_ref.shape[2]
    L = ng * 16
    row = lax.broadcasted_iota(jnp.int32, (2 * L, L), 0) % L
    col = lax.broadcasted_iota(jnp.int32, (2 * L, L), 1)
    valid = jnp.abs(row - col) <= A_HALF

    for a in range(4):
        def load(ref, a=a):
            return lambda cols: ref[0, 0, :, a, :, cols].reshape(L, LANES)

        def store_o(cols, o, a=a):
            o_ref[0, 0, :, a, :, cols] = o.astype(o_ref.dtype).reshape(ng, 16, LANES)

        lse = _band_pairs(load(q_ref), load(k_ref), load(v_ref), valid, store_o, L, mxu_sums=False)
        lse_ref[0, 0, :, a, :, :] = lse.reshape(ng, 16, LANES)


def _attn_a2(qkv, *, B, S):
    ng = S // PERM_GROUP
    view = qkv.reshape(B, 4, ng, 4, 16, qkv.shape[2])
    blk = lambda c: (1, 1, ng, 4, 16, c)
    spec = lambda j: pl.BlockSpec(blk(D_MODEL), lambda b, r: (b, r, 0, 0, 0, 3 + j))
    o, lse = pl.pallas_call(
        _attn_a2_kernel,
        grid=(B, 4),
        in_specs=[spec(0), spec(1), spec(2)],
        out_specs=[pl.BlockSpec(blk(D_MODEL), lambda b, r: (b, r, 0, 0, 0, 0)),
                   pl.BlockSpec(blk(LANES), lambda b, r: (b, r, 0, 0, 0, 0))],
        out_shape=[jax.ShapeDtypeStruct((B, 4, ng, 4, 16, D_MODEL), jnp.bfloat16),
                   jax.ShapeDtypeStruct((B, 4, ng, 4, 16, LANES), jnp.float32)],
        compiler_params=_cparams(("parallel", "parallel")),
        name="attn_a2",
    )(view, view, view)
    return o.reshape(B, S, D_MODEL), lse.reshape(B, S, LANES)


def _split_bf16(v, parts):
    out = []
    for _ in range(parts - 1):
        hi = v.astype(jnp.bfloat16)
        out.append(hi)
        v = v - hi.astype(jnp.float32)
    out.append(v.astype(jnp.bfloat16))
    return out


def _merge_out_kernel(x_ref, o0_ref, l0_ref, o1_ref, l1_ref, o2_ref, l2_ref, pm_ref, ee_ref, w_ref, y_ref):
    pm = pm_ref[...]
    ee = ee_ref[...]
    f32 = jnp.float32

    def natural(o_ref, l_ref, g):
        lse = l_ref[0, :, g].reshape(PERM_GROUP, LANES)
        both = jnp.concatenate([o_ref[0, :, g].reshape(PERM_GROUP, D_MODEL)] + _split_bf16(lse, 3), axis=1)
        r = jnp.dot(pm, both, preferred_element_type=f32)
        d = D_MODEL
        return r[:, :d], r[:, d:d + LANES] + r[:, d + LANES:d + 2 * LANES] + r[:, d + 2 * LANES:]

    for g in range(x_ref.shape[1] // PERM_GROUP):
        rows = slice(g * PERM_GROUP, (g + 1) * PERM_GROUP)
        o0 = o0_ref[0, rows, :].astype(f32)
        l0 = l0_ref[0, rows, :]
        o1, l1 = natural(o1_ref, l1_ref, g)
        o2, l2 = natural(o2_ref, l2_ref, g)
        m = jnp.maximum(jnp.maximum(l0, l1), l2)
        e0, e1, e2 = jnp.exp(l0 - m), jnp.exp(l1 - m), jnp.exp(l2 - m)
        den = e0 + e1 + e2

        def spread(e):
            return jnp.dot(jnp.concatenate(_split_bf16(e / den, 2), axis=1), ee, preferred_element_type=f32)

        a = (o0 + spread(e1) * (o1 - o0) + spread(e2) * (o2 - o0)).astype(jnp.bfloat16)
        y_ref[0, rows, :] = x_ref[0, rows, :] + jnp.dot(a, w_ref[...], preferred_element_type=f32)


def _merge_out(x3, o0, l0, o1, l1, o2, l2, w, *, groups=4):
    B, S, D = x3.shape
    ng = S // PERM_GROUP
    tm = groups * PERM_GROUP
    pview = lambda a: a.reshape(B, 4, ng, PERM_GROUP // 4, a.shape[2])
    nat = lambda c: pl.BlockSpec((1, tm, c), lambda b, j: (b, j, 0))
    per = lambda c: pl.BlockSpec((1, 4, groups, PERM_GROUP // 4, c), lambda b, j: (b, 0, j, 0, 0))
    full = lambda shape: pl.BlockSpec(shape, lambda b, j: (0, 0))

    pm = _group_perm_matrix()
    ee = np.zeros((2 * LANES, D_MODEL), np.float32)
    for h in range(N_HEADS):
        ee[h, h * HEAD_DIM:(h + 1) * HEAD_DIM] = 1.0
        ee[LANES + h, h * HEAD_DIM:(h + 1) * HEAD_DIM] = 1.0

    return pl.pallas_call(
        _merge_out_kernel,
        grid=(B, ng // groups),
        in_specs=[nat(D), nat(D), nat(LANES), per(D), per(LANES), per(D), per(LANES),
                  full((PERM_GROUP, PERM_GROUP)), full((2 * LANES, D)), full((D, D))],
        out_specs=nat(D),
        out_shape=jax.ShapeDtypeStruct((B, S, D), jnp.float32),
        compiler_params=_cparams(("parallel", "parallel")),
        name="merge_out",
    )(x3, o0, l0, pview(o1), pview(l1), pview(o2), pview(l2),
      jnp.asarray(pm, jnp.bfloat16), jnp.asarray(ee, jnp.bfloat16), w)


def _out_kernel(x_ref, a_ref, w_ref, y_ref):
    y_ref[...] = x_ref[...] + jnp.dot(a_ref[...], w_ref[...], preferred_element_type=jnp.float32)


def _out_proj(x2, a, w, *, tm=1024):
    T, D = x2.shape
    row = pl.BlockSpec((tm, D), lambda i: (i, 0))
    return pl.pallas_call(
        _out_kernel,
        grid=(T // tm,),
        in_specs=[row, row, pl.BlockSpec((D, D), lambda i: (0, 0))],
        out_specs=row,
        out_shape=jax.ShapeDtypeStruct((T, D), jnp.float32),
        compiler_params=_cparams(("parallel",)),
        name="out_proj",
    )(x2, a, w)


B_QROWS = 4
B_KROWS = 12
B_SUB = 128
NA_KH = 8
NA_KW = 16


def _b_slab_index(rows):
    units = rows // B_QROWS
    idx = np.zeros((3, B_QROWS, B_KROWS), np.int32)
    for geo, u in enumerate((0, 1, units - 1)):
        kr0 = min(max(u * B_QROWS - NA_KH // 2, 0), rows - B_KROWS)
        for a in range(B_QROWS):
            qr = u * B_QROWS + a
            rs = min(max(qr - NA_KH // 2, 0), rows - NA_KH)
            for c in range(B_KROWS):
                kr = kr0 + c
                idx[geo, a, c] = kr - qr + NA_KH - 1 if rs <= kr < rs + NA_KH else 2 * NA_KH - 1
    return idx


def _attn_b_kernel(q_ref, k_ref, v_ref, slab_ref, o_ref, tbl_ref, *, rows):
    is_a = _lane_is_first_head()
    nq = B_QROWS * GRID_W
    nk = B_KROWS * GRID_W
    units = rows // B_QROWS
    idx = _b_slab_index(rows)

    @pl.when(pl.program_id(1) == 0)
    def _():
        for h in range(2):
            for geo in range(3):
                for qr in range(B_QROWS):
                    for j in range(B_KROWS // 2):
                        even = slab_ref[h, int(idx[geo, qr, 2 * j])]
                        odd = slab_ref[h, int(idx[geo, qr, 2 * j + 1])]
                        tbl_ref[h, geo, qr * GRID_W:(qr + 1) * GRID_W, j * LANES:(j + 1) * LANES] = (
                            jnp.where(is_a, even, odd))

    for u in range(units):
        qs = u * nq
        kr0 = min(max(u * B_QROWS - NA_KH // 2, 0), rows - B_KROWS)
        ks = kr0 * GRID_W
        geo = 0 if u == 0 else (2 if u == units - 1 else 1)
        kw = k_ref[0, ks:ks + nk, :]
        vw = v_ref[0, ks:ks + nk, :]
        for r0 in range(0, nq, B_SUB):
            qst = _stack_heads(q_ref[0, qs + r0:qs + r0 + B_SUB, :], is_a)
            s = lax.dot_general(qst, kw, NT_DIMS, preferred_element_type=jnp.float32)
            tbl = tbl_ref[:, geo, r0:r0 + B_SUB, :].reshape(2 * B_SUB, nk)
            s = jnp.where(tbl > 0.5 * NEG_INF, s + tbl, NEG_INF)
            pv, _, den = _softmax_pv(s, vw)
            o_ref[0, qs + r0:qs + r0 + B_SUB, :] = (
                _unstack(is_a, pv, B_SUB) / _unstack(is_a, den, B_SUB)).astype(o_ref.dtype)


def _attn_b(qkv, slabs, *, B, S):
    rows = S // GRID_W
    npair = D_MODEL // LANES
    nq = B_QROWS * GRID_W
    nk = B_KROWS * GRID_W

    def in_spec(j):
        return pl.BlockSpec((1, S, LANES), lambda p, b: (b, 0, j * npair + p))

    kern = functools.partial(_attn_b_kernel, rows=rows)
    return pl.pallas_call(
        kern,
        grid=(npair, B),
        in_specs=[in_spec(0), in_spec(1), in_spec(2),
                  pl.BlockSpec((2, 2 * NA_KH, GRID_W, LANES), lambda p, b: (p, 0, 0, 0))],
        out_specs=pl.BlockSpec((1, S, LANES), lambda p, b: (b, 0, p)),
        out_shape=jax.ShapeDtypeStruct((B, S, D_MODEL), jnp.bfloat16),
        scratch_shapes=[pltpu.VMEM((2, 3, nq, nk), jnp.float32)],
        compiler_params=_cparams(("parallel", "arbitrary")),
        name="attn_b",
    )(qkv, qkv, qkv, slabs)


def _b_bias_slabs(rpb):
    H = rpb.shape[0]
    qc = np.arange(GRID_W)[:, None]
    kc = (np.arange(LANES) % GRID_W)[None, :]
    cs = np.clip(qc - NA_KW // 2, 0, GRID_W - NA_KW)
    col_ok = (kc >= cs) & (kc < cs + NA_KW)
    dc = np.clip(kc - qc + NA_KW - 1, 0, 2 * NA_KW - 2)
    onehot = ((dc[None] == np.arange(2 * NA_KW - 1)[:, None, None]) & col_ok[None]).astype(np.float32)
    cval = jnp.einsum("hac,cqk->haqk", rpb, jnp.asarray(onehot), precision=lax.Precision.HIGHEST)
    slabs = jnp.where(jnp.asarray(col_ok)[None, None], cval * LOG2E, NEG_INF)
    return jnp.concatenate([slabs, jnp.full((H, 1, GRID_W, LANES), NEG_INF, jnp.float32)], axis=1)


def _attn_c_kernel(q_ref, k_ref, v_ref, o_ref):
    is_a = _lane_is_first_head()
    tq = C_SUB_ROWS
    for r0 in range(0, q_ref.shape[1], tq):
        rows = slice(r0, r0 + tq)
        q = q_ref[0, rows, :]
        qst = jnp.concatenate([_stack_heads(q[:, :LANES], is_a), _stack_heads(q[:, LANES:], is_a)], axis=0)
        s = lax.dot_general(qst, k_ref[0], NT_DIMS, preferred_element_type=jnp.float32)
        pv, _, den = _softmax_pv(s, v_ref[0])
        o_ref[0, rows, :LANES] = (_unstack(is_a, pv, tq) / _unstack(is_a, den, tq)).astype(o_ref.dtype)
        o_ref[0, rows, LANES:] = (_unstack(is_a, pv[2 * tq:], tq) / _unstack(is_a, den[2 * tq:], tq)).astype(o_ref.dtype)


C_SUB_ROWS = 64


def _attn_c(q, k2, v2, *, B, S, tq=2048):
    gw = 2 * LANES
    return pl.pallas_call(
        _attn_c_kernel,
        grid=(B, C_KV_HEADS, S // tq),
        in_specs=[
            pl.BlockSpec((1, tq, gw), lambda b, g, i: (b, i, g)),
            pl.BlockSpec((1, S, LANES), lambda b, g, i: (b, 0, g)),
            pl.BlockSpec((1, S, LANES), lambda b, g, i: (b, 0, g)),
        ],
        out_specs=pl.BlockSpec((1, tq, gw), lambda b, g, i: (b, i, g)),
        out_shape=jax.ShapeDtypeStruct((B, S, D_MODEL), jnp.bfloat16),
        compiler_params=_cparams(("parallel", "parallel", "parallel")),
        name="attn_c",
    )(q, k2, v2)


MLP_ROWS = 256


def _mlp_kernel(x_ref, g_ref, wu_ref, wd_ref, gf_ref, y_ref, h_ref, acc_ref, *, final_norm):
    f = pl.program_id(1)
    last = pl.num_programs(1) - 1
    tm = x_ref.shape[0]

    def weights():
        return wu_ref[...].astype(jnp.bfloat16), wd_ref[...].astype(jnp.bfloat16)

    def partial_out(rows, wu, wd):
        u = jnp.dot(h_ref[rows, :], wu, preferred_element_type=jnp.float32)
        r = jnp.maximum(u, 0.0)
        return jnp.dot((r * r).astype(jnp.bfloat16), wd, preferred_element_type=jnp.float32)

    @pl.when(f == 0)
    def _():
        wu, wd = weights()
        for r0 in range(0, tm, MLP_ROWS):
            rows = slice(r0, r0 + MLP_ROWS)
            h_ref[rows, :] = _rms(x_ref[rows, :], g_ref[...]).astype(jnp.bfloat16)
            acc_ref[rows, :] = partial_out(rows, wu, wd)

    @pl.when((f != 0) & (f != last))
    def _():
        acc_ref[...] += partial_out(slice(None), *weights())

    @pl.when(f == last)
    def _():
        wu, wd = weights()
        for r0 in range(0, tm, MLP_ROWS):
            rows = slice(r0, r0 + MLP_ROWS)
            y = x_ref[rows, :] + (acc_ref[rows, :] + partial_out(rows, wu, wd))
            if final_norm:
                y = _rms(y, gf_ref[...])
            y_ref[rows, :] = y


def _mlp(x2, g, wu, wd, gf, *, final_norm, tm=1024, tf=1024):
    T, D = x2.shape
    F = wu.shape[1]
    assert F // tf >= 2, "the kernel treats the first and the last hidden chunk separately"
    kern = functools.partial(_mlp_kernel, final_norm=final_norm)
    return pl.pallas_call(
        kern,
        grid=(T // tm, F // tf),
        in_specs=[
            pl.BlockSpec((tm, D), lambda i, f: (i, 0)),
            pl.BlockSpec((1, D), lambda i, f: (0, 0)),
            pl.BlockSpec((D, tf), lambda i, f: (0, f)),
            pl.BlockSpec((tf, D), lambda i, f: (f, 0)),
            pl.BlockSpec((1, D), lambda i, f: (0, 0)),
        ],
        out_specs=pl.BlockSpec((tm, D), lambda i, f: (i, 0)),
        out_shape=jax.ShapeDtypeStruct((T, D), jnp.float32),
        scratch_shapes=[pltpu.VMEM((tm, D), jnp.bfloat16), pltpu.VMEM((tm, D), jnp.float32)],
        compiler_params=_cparams(("parallel", "arbitrary")),
        name="mlp",
    )(x2, g, wu, wd, gf)


def _rope_angles(pos, dim):
    inv = 1.0 / (ROPE_THETA ** (np.arange(0, dim, 2, dtype=np.float64) / dim))
    return pos.astype(np.float64)[:, None] * inv[None, :]


def _rope_tables(ang):
    cos, sin = np.cos(ang).astype(np.float32), np.sin(ang).astype(np.float32)
    return np.tile(cos, (1, 4)), np.tile(np.concatenate([-sin, sin], axis=-1), (1, 2))


def kernel(x, l0_attn_norm, l0_w_in, l0_w_out, l0_mlp_norm, l0_w_up, l0_w_down, l1_attn_norm, l1_w_in, l1_rpb, l1_w_out, l1_mlp_norm, l1_w_up, l1_w_down, l2_attn_norm, l2_w_in, l2_q_norm, l2_k_norm, l2_w_out, l2_mlp_norm, l2_w_up, l2_w_down, l3_attn_norm, l3_w_in, l3_w_out, l3_mlp_norm, l3_w_up, l3_w_down, final_norm):
    B, S, D = x.shape
    bf = lambda w: w.astype(jnp.bfloat16)
    row = lambda g: g.reshape(1, -1).astype(jnp.float32)

    t = np.arange(S, dtype=np.int32)
    perm = _token_perm(S)
    cos_a, sin_a = _rope_tables(_rope_angles(t, HEAD_DIM))
    cos_p, sin_p = jnp.asarray(cos_a[perm]), jnp.asarray(sin_a[perm])
    cos_a, sin_a = jnp.asarray(cos_a), jnp.asarray(sin_a)
    cos_c, sin_c = map(jnp.asarray, _rope_tables(np.concatenate(
        [_rope_angles(t // GRID_W, HALF), _rope_angles(t % GRID_W, HALF)], axis=-1)))

    x2 = x.reshape(B * S, D)

    def layer_a(x2, attn_norm, w_in, w_out):
        gw = 3 * D_MODEL
        g = row(attn_norm)
        qkv0 = _proj(x2, g, w_in, cos_a, sin_a, rope=True, ncols=gw).reshape(B, S, -1)
        qkv12 = _proj(x2, g, w_in, cos_p, sin_p, rope=True, perm=True, col0=gw, tm=S).reshape(B, S, -1)
        o0, l0 = _attn_a01(qkv0, B=B, S=S, L=S, permuted=False, name="attn_a0")
        o1, l1 = _attn_a01(qkv12, B=B, S=S, L=S // 4, permuted=True, name="attn_a1")
        o2, l2 = _attn_a2(qkv12, B=B, S=S)
        return _merge_out(x2.reshape(B, S, D), o0, l0, o1, l1, o2, l2, bf(w_out)).reshape(B * S, D)

    def layer_b(x2, attn_norm, w_in, rpb, w_out):
        qkv = _proj(x2, row(attn_norm), w_in, cos_a, sin_a, rope=False).reshape(B, S, -1)
        a = _attn_b(qkv, _b_bias_slabs(rpb.astype(jnp.float32)), B=B, S=S)
        return _out_proj(x2, a.reshape(B * S, D), bf(w_out))

    def layer_c(x2, attn_norm, w_in, q_norm, k_norm, w_out):
        gain2 = lambda g: jnp.tile(g.astype(jnp.float32), 2).reshape(1, LANES)
        bd = jnp.kron(jnp.eye(2, dtype=jnp.float32),
                      jnp.full((HEAD_DIM, HEAD_DIM), 1.0 / HEAD_DIM, jnp.float32)).astype(jnp.bfloat16)
        q, k2, v2 = _proj_c(x2, row(attn_norm), bf(w_in), cos_c, sin_c, gain2(q_norm), gain2(k_norm), bd)
        a = _attn_c(q.reshape(B, S, -1), k2.reshape(B, S, -1), v2.reshape(B, S, -1), B=B, S=S)
        return _out_proj(x2, a.reshape(B * S, D), bf(w_out))

    fn = row(final_norm)
    x2 = layer_a(x2, l0_attn_norm, l0_w_in, l0_w_out)
    x2 = _mlp(x2, row(l0_mlp_norm), l0_w_up, l0_w_down, fn, final_norm=False)
    x2 = layer_b(x2, l1_attn_norm, l1_w_in, l1_rpb, l1_w_out)
    x2 = _mlp(x2, row(l1_mlp_norm), l1_w_up, l1_w_down, fn, final_norm=False)
    x2 = layer_c(x2, l2_attn_norm, l2_w_in, l2_q_norm, l2_k_norm, l2_w_out)
    x2 = _mlp(x2, row(l2_mlp_norm), l2_w_up, l2_w_down, fn, final_norm=False)
    x2 = layer_a(x2, l3_attn_norm, l3_w_in, l3_w_out)
    x2 = _mlp(x2, row(l3_mlp_norm), l3_w_up, l3_w_down, fn, final_norm=True)
    return x2.reshape(B, S, D)
```

```python
import functools

import jax
import jax.numpy as jnp
import numpy as np
from jax import lax
from jax.experimental import pallas as pl
from jax.experimental.pallas import tpu as pltpu

D_MODEL = 1024
HEAD_DIM = 64
N_HEADS = 16
D_FF = 4 * D_MODEL
ROPE_THETA = 10000.0
RMS_EPS = 1e-6
NEG_INF = -1e30
GRID_W = 64
A_GROUPS = ((128, 1), (512, 4), (2048, 16))
C_KV_HEADS = 4
LOG2E = 1.4426950408889634
LN2 = 0.6931471805599453
QK_SCALE = HEAD_DIM ** -0.5 * LOG2E

LANES = 128
HALF = HEAD_DIM // 2
VMEM_LIMIT = 56 * 1024 * 1024

NT_DIMS = (((1,), (1,)), ((), ()))


def _cparams(sem):
    return pltpu.CompilerParams(dimension_semantics=sem, vmem_limit_bytes=VMEM_LIMIT)


def _rms(x, g):
    ms = jnp.mean(x * x, axis=-1, keepdims=True)
    return (x * lax.rsqrt(ms + RMS_EPS)) * g


def _lane_is_first_head():
    return lax.broadcasted_iota(jnp.int32, (1, LANES), 1) < HEAD_DIM


def _rope128(y, cos, sin_signed):
    lane = lax.broadcasted_iota(jnp.int32, (1, LANES), 1)
    first_half = (lane % HEAD_DIM) < HALF
    partner = jnp.where(first_half, pltpu.roll(y, LANES - HALF, 1), pltpu.roll(y, HALF, 1))
    return y * cos + partner * sin_signed


PERM_GROUP = 256
PROJ_ROWS = 256


def _token_perm(S):
    t = np.arange(S).reshape(S // PERM_GROUP, 16, 4, 4)
    return t.transpose(3, 0, 2, 1).reshape(S)


def _group_perm_matrix():
    token = np.arange(PERM_GROUP).reshape(16, 4, 4).transpose(2, 1, 0).reshape(-1)
    pm = np.zeros((PERM_GROUP, PERM_GROUP), np.float32)
    pm[token, np.arange(PERM_GROUP)] = 1.0
    return pm


def _proj_kernel(x_ref, g_ref, w_ref, cos_ref, sin_ref, pm_ref, o_ref, h_ref, *, rope, perm):
    n = pl.program_id(1)
    tm = x_ref.shape[0]
    tn = o_ref.shape[1]

    def normalise():
        g = g_ref[...]
        if perm:
            ng = tm // PERM_GROUP
            run = PERM_GROUP // 4
            for grp in range(ng):
                hn = _rms(x_ref[grp * PERM_GROUP:(grp + 1) * PERM_GROUP, :], g).astype(jnp.bfloat16)
                hp = jnp.dot(pm_ref[...], hn, preferred_element_type=jnp.float32).astype(jnp.bfloat16)
                for rho in range(4):
                    dst = (rho * ng + grp) * run
                    h_ref[dst:dst + run, :] = hp[rho * run:(rho + 1) * run, :]
        else:
            for r0 in range(0, tm, PROJ_ROWS):
                h_ref[r0:r0 + PROJ_ROWS, :] = _rms(x_ref[r0:r0 + PROJ_ROWS, :], g).astype(jnp.bfloat16)

    sec = (n * tn // D_MODEL) % 3

    def sub_blocks(epilogue, scale):
        w = w_ref[...].astype(jnp.bfloat16)
        for r0 in range(0, tm, PROJ_ROWS):
            rows = slice(r0, r0 + PROJ_ROWS)
            y = jnp.dot(h_ref[rows, :], w, preferred_element_type=jnp.float32)
            epilogue(rows, y, scale)

    def plain(rows, y, scale):
        o_ref[rows, :] = (y * scale).astype(o_ref.dtype)

    def roped(rows, y, scale):
        cos = cos_ref[rows, :]
        sin = sin_ref[rows, :]
        for c in range(tn // LANES):
            sl = slice(c * LANES, (c + 1) * LANES)
            o_ref[rows, sl] = (_rope128(y[:, sl], cos, sin) * scale).astype(o_ref.dtype)

    qk = roped if rope else plain

    @pl.when(n == 0)
    def _():
        normalise()
        sub_blocks(qk, QK_SCALE)

    @pl.when((n != 0) & (sec == 0))
    def _():
        sub_blocks(qk, QK_SCALE)

    @pl.when(sec == 1)
    def _():
        sub_blocks(qk, 1.0)

    @pl.when(sec == 2)
    def _():
        sub_blocks(plain, 1.0)


def _proj(x2, g, w, cos, sin, *, rope, perm=False, col0=0, ncols=None, tm=2048, tn=1024):
    T, D = x2.shape
    N = w.shape[1] - col0 if ncols is None else ncols
    S = cos.shape[0]
    nsb = S // tm
    cb0 = col0 // tn
    kern = functools.partial(_proj_kernel, rope=rope, perm=perm)
    return pl.pallas_call(
        kern,
        grid=(T // tm, N // tn),
        in_specs=[
            pl.BlockSpec((tm, D), lambda i, n: (i, 0)),
            pl.BlockSpec((1, D), lambda i, n: (0, 0)),
            pl.BlockSpec((D, tn), lambda i, n: (0, cb0 + n)),
            pl.BlockSpec((tm, LANES), lambda i, n: (i % nsb, 0)),
            pl.BlockSpec((tm, LANES), lambda i, n: (i % nsb, 0)),
            pl.BlockSpec((PERM_GROUP, PERM_GROUP), lambda i, n: (0, 0)),
        ],
        out_specs=pl.BlockSpec((tm, tn), lambda i, n: (i, n)),
        out_shape=jax.ShapeDtypeStruct((T, N), jnp.bfloat16),
        scratch_shapes=[pltpu.VMEM((tm, D), jnp.bfloat16)],
        compiler_params=_cparams(("parallel", "arbitrary")),
        name="proj_perm" if perm else "proj",
    )(x2, g, w, cos, sin, jnp.asarray(_group_perm_matrix().T, jnp.bfloat16))


def _proj_c_kernel(x_ref, g_ref, w_ref, cos_ref, sin_ref, qg_ref, kg_ref, bd_ref,
                   q_ref, k_ref, v_ref):
    bd = bd_ref[...]
    is_a = _lane_is_first_head()
    nq = D_MODEL // LANES
    nkv = C_KV_HEADS * HEAD_DIM // LANES

    for r0 in range(0, x_ref.shape[0], PROJ_ROWS):
        rows = slice(r0, r0 + PROJ_ROWS)
        h = _rms(x_ref[rows, :], g_ref[...]).astype(jnp.bfloat16)
        y = jnp.dot(h, w_ref[...], preferred_element_type=jnp.float32)
        cos = cos_ref[rows, :]
        sin = sin_ref[rows, :]

        def head_norm(c, gain):
            yc = y[:, c * LANES:(c + 1) * LANES]
            sq = yc * yc
            hi = sq.astype(jnp.bfloat16)
            lo = (sq - hi.astype(jnp.float32)).astype(jnp.bfloat16)
            ms = (jnp.dot(hi, bd, preferred_element_type=jnp.float32)
                  + jnp.dot(lo, bd, preferred_element_type=jnp.float32))
            return (yc * lax.rsqrt(ms + RMS_EPS)) * gain

        for c in range(nq):
            qn = head_norm(c, qg_ref[...])
            q_ref[rows, c * LANES:(c + 1) * LANES] = (_rope128(qn, cos, sin) * QK_SCALE).astype(q_ref.dtype)
        for c in range(nkv):
            kn = _rope128(head_norm(nq + c, kg_ref[...]), cos, sin)
            ksw = pltpu.roll(kn, HEAD_DIM, 1)
            k_ref[rows, (2 * c) * LANES:(2 * c + 1) * LANES] = jnp.where(is_a, kn, ksw).astype(k_ref.dtype)
            k_ref[rows, (2 * c + 1) * LANES:(2 * c + 2) * LANES] = jnp.where(is_a, ksw, kn).astype(k_ref.dtype)
            vc = y[:, (nq + nkv + c) * LANES:(nq + nkv + c + 1) * LANES]
            vsw = pltpu.roll(vc, HEAD_DIM, 1)
            v_ref[rows, (2 * c) * LANES:(2 * c + 1) * LANES] = jnp.where(is_a, vc, vsw).astype(v_ref.dtype)
            v_ref[rows, (2 * c + 1) * LANES:(2 * c + 2) * LANES] = jnp.where(is_a, vsw, vc).astype(v_ref.dtype)


def _proj_c(x2, g, w, cos, sin, qg, kg, bd, *, tm=1024):
    T, D = x2.shape
    N = w.shape[1]
    S = cos.shape[0]
    nsb = S // tm
    kvw = 2 * C_KV_HEADS * HEAD_DIM
    full = lambda shape: pl.BlockSpec(shape, lambda i: (0, 0))
    return pl.pallas_call(
        _proj_c_kernel,
        grid=(T // tm,),
        in_specs=[
            pl.BlockSpec((tm, D), lambda i: (i, 0)),
            full((1, D)),
            full((D, N)),
            pl.BlockSpec((tm, LANES), lambda i: (i % nsb, 0)),
            pl.BlockSpec((tm, LANES), lambda i: (i % nsb, 0)),
            full((1, LANES)),
            full((1, LANES)),
            full((LANES, LANES)),
        ],
        out_specs=[
            pl.BlockSpec((tm, D_MODEL), lambda i: (i, 0)),
            pl.BlockSpec((tm, kvw), lambda i: (i, 0)),
            pl.BlockSpec((tm, kvw), lambda i: (i, 0)),
        ],
        out_shape=[
            jax.ShapeDtypeStruct((T, D_MODEL), jnp.bfloat16),
            jax.ShapeDtypeStruct((T, kvw), jnp.bfloat16),
            jax.ShapeDtypeStruct((T, kvw), jnp.bfloat16),
        ],
        compiler_params=_cparams(("parallel",)),
        name="proj_c",
    )(x2, g, w, cos, sin, qg, kg, bd)


def _stack_heads(q2, is_a):
    zero = jnp.zeros_like(q2)
    return jnp.concatenate([jnp.where(is_a, q2, zero), jnp.where(is_a, zero, q2)], axis=0)


def _softmax_pv(s, vw, mxu_sums=True):
    m = jnp.max(s, axis=-1, keepdims=True)
    if not mxu_sums:
        p = jnp.exp2(s - m)
        den = jnp.sum(p, axis=-1, keepdims=True)
        return jnp.dot(p.astype(vw.dtype), vw, preferred_element_type=jnp.float32), m, den
    p = jnp.exp2((s - m).astype(jnp.bfloat16))
    vaug = jnp.concatenate([vw, jnp.ones_like(vw)], axis=1)
    r = jnp.dot(p, vaug, preferred_element_type=jnp.float32)
    return r[:, :LANES], m, r[:, LANES:]


def _unstack(is_a, x, n):
    return jnp.where(is_a, x[:n], x[n:2 * n])


A_HALF = 64
A_QB = 128
NPAIR = D_MODEL // LANES


def _band_pairs(load_q, load_k, load_v, valid, store_o, qb, mxu_sums=True):
    is_a = _lane_is_first_head()
    lane = lax.broadcasted_iota(jnp.int32, (1, LANES), 1)
    m_tile = jnp.zeros((qb, LANES), jnp.float32)
    den_tile = jnp.ones((qb, LANES), jnp.float32)
    for p in range(NPAIR):
        cols = slice(p * LANES, (p + 1) * LANES)
        qst = _stack_heads(load_q(cols), is_a)
        vw = load_v(cols)
        s = lax.dot_general(qst, load_k(cols), NT_DIMS, preferred_element_type=jnp.float32)
        s = jnp.where(valid, s, NEG_INF)
        pv, m, den = _softmax_pv(s, vw, mxu_sums)
        store_o(cols, _unstack(is_a, pv, qb) / _unstack(is_a, den, qb))
        in_a, in_b = lane == 2 * p, lane == 2 * p + 1
        m_tile = jnp.where(in_a, m[:qb], jnp.where(in_b, m[qb:], m_tile))
        den_tile = jnp.where(in_a, den[:qb], jnp.where(in_b, den[qb:], den_tile))
    return m_tile * LN2 + jnp.log(den_tile)


def _attn_a_kernel(q_ref, k_ref, v_ref, o_ref, lse_ref, mask_ref, *, L, permuted):
    qb, W = A_QB, 2 * A_QB
    nq = L // qb

    def pos(l):
        return (l // 64) * 64 + 4 * (l % 16) + (l % 64) // 16 if permuted else l

    rel = (pos(lax.broadcasted_iota(jnp.int32, (2 * qb, W), 0) % qb)
           - pos(lax.broadcasted_iota(jnp.int32, (2 * qb, W), 1)))
    for case, delta in enumerate((0, A_HALF, W - qb)):
        mask_ref[case] = (jnp.abs(rel + delta) <= A_HALF).astype(jnp.int32)

    def body(iq, carry):
        qs = pl.multiple_of(iq * qb, qb)
        ks = pl.multiple_of(jnp.clip(qs - A_HALF, 0, L - W), A_HALF)
        case = jnp.where(iq == 0, 0, jnp.where(iq == nq - 1, 2, 1))
        valid = mask_ref[case] != 0

        def store_o(cols, o):
            o_ref[0, pl.ds(qs, qb), cols] = o.astype(o_ref.dtype)

        lse_ref[0, pl.ds(qs, qb), :] = _band_pairs(
            lambda cols: q_ref[0, pl.ds(qs, qb), cols],
            lambda cols: k_ref[0, pl.ds(ks, W), cols],
            lambda cols: v_ref[0, pl.ds(ks, W), cols],
            valid, store_o, qb)
        return carry

    lax.fori_loop(0, nq, body, 0, unroll=min(nq, 8))


def _attn_a01(qkv, *, B, S, L, permuted, name):
    runs = S // L
    spec = lambda j: pl.BlockSpec((1, L, D_MODEL), lambda b, r: (b, r, j))
    kern = functools.partial(_attn_a_kernel, L=L, permuted=permuted)
    return pl.pallas_call(
        kern,
        grid=(B, runs),
        in_specs=[spec(0), spec(1), spec(2)],
        out_specs=[pl.BlockSpec((1, L, D_MODEL), lambda b, r: (b, r, 0)),
                   pl.BlockSpec((1, L, LANES), lambda b, r: (b, r, 0))],
        out_shape=[jax.ShapeDtypeStruct((B, S, D_MODEL), jnp.bfloat16),
                   jax.ShapeDtypeStruct((B, S, LANES), jnp.float32)],
        scratch_shapes=[pltpu.VMEM((3, 2 * A_QB, 2 * A_QB), jnp.int32)],
        compiler_params=_cparams(("parallel", "parallel")),
        name=name,
    )(qkv, qkv, qkv)


def _attn_a2_kernel(q_ref, k_ref, v_ref, o_ref, lse_ref):
    ng = q_ref.shape[2]
    L = ng * 16
    row = lax.broadcasted_iota(jnp.int32, (2 * L, L), 0) % L
    col = lax.broadcasted_iota(jnp.int32, (2 * L, L), 1)
    valid = jnp.abs(row - col) <= A_HALF

    for a in range(4):
        def load(ref, a=a):
            return lambda cols: ref[0, 0, :, a, :, cols].reshape(L, LANES)

        def store_o(cols, o, a=a):
            o_ref[0, 0, :, a, :, cols] = o.astype(o_ref.dtype).reshape(ng, 16, LANES)

        lse = _band_pairs(load(q_ref), load(k_ref), load(v_ref), valid, store_o, L, mxu_sums=False)
        lse_ref[0, 0, :, a, :, :] = lse.reshape(ng, 16, LANES)


def _attn_a2(qkv, *, B, S):
    ng = S // PERM_GROUP
    view = qkv.reshape(B, 4, ng, 4, 16, qkv.shape[2])
    blk = lambda c: (1, 1, ng, 4, 16, c)
    spec = lambda j: pl.BlockSpec(blk(D_MODEL), lambda b, r: (b, r, 0, 0, 0, 3 + j))
    o, lse = pl.pallas_call(
        _attn_a2_kernel,
        grid=(B, 4),
        in_specs=[spec(0), spec(1), spec(2)],
        out_specs=[pl.BlockSpec(blk(D_MODEL), lambda b, r: (b, r, 0, 0, 0, 0)),
                   pl.BlockSpec(blk(LANES), lambda b, r: (b, r, 0, 0, 0, 0))],
        out_shape=[jax.ShapeDtypeStruct((B, 4, ng, 4, 16, D_MODEL), jnp.bfloat16),
                   jax.ShapeDtypeStruct((B, 4, ng, 4, 16, LANES), jnp.float32)],
        compiler_params=_cparams(("parallel", "parallel")),
        name="attn_a2",
    )(view, view, view)
    return o.reshape(B, S, D_MODEL), lse.reshape(B, S, LANES)


def _split_bf16(v, parts):
    out = []
    for _ in range(parts - 1):
        hi = v.astype(jnp.bfloat16)
        out.append(hi)
        v = v - hi.astype(jnp.float32)
    out.append(v.astype(jnp.bfloat16))
    return out


def _merge_out_kernel(x_ref, o0_ref, l0_ref, o1_ref, l1_ref, o2_ref, l2_ref, pm_ref, ee_ref, w_ref, y_ref):
    pm = pm_ref[...]
    ee = ee_ref[...]
    f32 = jnp.float32

    def natural(o_ref, l_ref, g):
        lse = l_ref[0, :, g].reshape(PERM_GROUP, LANES)
        both = jnp.concatenate([o_ref[0, :, g].reshape(PERM_GROUP, D_MODEL)] + _split_bf16(lse, 3), axis=1)
        r = jnp.dot(pm, both, preferred_element_type=f32)
        d = D_MODEL
        return r[:, :d], r[:, d:d + LANES] + r[:, d + LANES:d + 2 * LANES] + r[:, d + 2 * LANES:]

    for g in range(x_ref.shape[1] // PERM_GROUP):
        rows = slice(g * PERM_GROUP, (g + 1) * PERM_GROUP)
        o0 = o0_ref[0, rows, :].astype(f32)
        l0 = l0_ref[0, rows, :]
        o1, l1 = natural(o1_ref, l1_ref, g)
        o2, l2 = natural(o2_ref, l2_ref, g)
        m = jnp.maximum(jnp.maximum(l0, l1), l2)
        e0, e1, e2 = jnp.exp(l0 - m), jnp.exp(l1 - m), jnp.exp(l2 - m)
        den = e0 + e1 + e2

        def spread(e):
            return jnp.dot(jnp.concatenate(_split_bf16(e / den, 2), axis=1), ee, preferred_element_type=f32)

        a = (o0 + spread(e1) * (o1 - o0) + spread(e2) * (o2 - o0)).astype(jnp.bfloat16)
        y_ref[0, rows, :] = x_ref[0, rows, :] + jnp.dot(a, w_ref[...], preferred_element_type=f32)


def _merge_out(x3, o0, l0, o1, l1, o2, l2, w, *, groups=4):
    B, S, D = x3.shape
    ng = S // PERM_GROUP
    tm = groups * PERM_GROUP
    pview = lambda a: a.reshape(B, 4, ng, PERM_GROUP // 4, a.shape[2])
    nat = lambda c: pl.BlockSpec((1, tm, c), lambda b, j: (b, j, 0))
    per = lambda c: pl.BlockSpec((1, 4, groups, PERM_GROUP // 4, c), lambda b, j: (b, 0, j, 0, 0))
    full = lambda shape: pl.BlockSpec(shape, lambda b, j: (0, 0))

    pm = _group_perm_matrix()
    ee = np.zeros((2 * LANES, D_MODEL), np.float32)
    for h in range(N_HEADS):
        ee[h, h * HEAD_DIM:(h + 1) * HEAD_DIM] = 1.0
        ee[LANES + h, h * HEAD_DIM:(h + 1) * HEAD_DIM] = 1.0

    return pl.pallas_call(
        _merge_out_kernel,
        grid=(B, ng // groups),
        in_specs=[nat(D), nat(D), nat(LANES), per(D), per(LANES), per(D), per(LANES),
                  full((PERM_GROUP, PERM_GROUP)), full((2 * LANES, D)), full((D, D))],
        out_specs=nat(D),
        out_shape=jax.ShapeDtypeStruct((B, S, D), jnp.float32),
        compiler_params=_cparams(("parallel", "parallel")),
        name="merge_out",
    )(x3, o0, l0, pview(o1), pview(l1), pview(o2), pview(l2),
      jnp.asarray(pm, jnp.bfloat16), jnp.asarray(ee, jnp.bfloat16), w)


def _out_kernel(x_ref, a_ref, w_ref, y_ref):
    y_ref[...] = x_ref[...] + jnp.dot(a_ref[...], w_ref[...], preferred_element_type=jnp.float32)


def _out_proj(x2, a, w, *, tm=1024):
    T, D = x2.shape
    row = pl.BlockSpec((tm, D), lambda i: (i, 0))
    return pl.pallas_call(
        _out_kernel,
        grid=(T // tm,),
        in_specs=[row, row, pl.BlockSpec((D, D), lambda i: (0, 0))],
        out_specs=row,
        out_shape=jax.ShapeDtypeStruct((T, D), jnp.float32),
        compiler_params=_cparams(("parallel",)),
        name="out_proj",
    )(x2, a, w)


B_QROWS = 4
B_KROWS = 12
B_SUB = 128
NA_KH = 8
NA_KW = 16


def _b_slab_index(rows):
    units = rows // B_QROWS
    idx = np.zeros((3, B_QROWS, B_KROWS), np.int32)
    for geo, u in enumerate((0, 1, units - 1)):
        kr0 = min(max(u * B_QROWS - NA_KH // 2, 0), rows - B_KROWS)
        for a in range(B_QROWS):
            qr = u * B_QROWS + a
            rs = min(max(qr - NA_KH // 2, 0), rows - NA_KH)
            for c in range(B_KROWS):
                kr = kr0 + c
                idx[geo, a, c] = kr - qr + NA_KH - 1 if rs <= kr < rs + NA_KH else 2 * NA_KH - 1
    return idx


def _attn_b_kernel(q_ref, k_ref, v_ref, slab_ref, o_ref, tbl_ref, *, rows):
    is_a = _lane_is_first_head()
    nq = B_QROWS * GRID_W
    nk = B_KROWS * GRID_W
    units = rows // B_QROWS
    idx = _b_slab_index(rows)

    @pl.when(pl.program_id(1) == 0)
    def _():
        for h in range(2):
            for geo in range(3):
                for qr in range(B_QROWS):
                    for j in range(B_KROWS // 2):
                        even = slab_ref[h, int(idx[geo, qr, 2 * j])]
                        odd = slab_ref[h, int(idx[geo, qr, 2 * j + 1])]
                        tbl_ref[h, geo, qr * GRID_W:(qr + 1) * GRID_W, j * LANES:(j + 1) * LANES] = (
                            jnp.where(is_a, even, odd))

    for u in range(units):
        qs = u * nq
        kr0 = min(max(u * B_QROWS - NA_KH // 2, 0), rows - B_KROWS)
        ks = kr0 * GRID_W
        geo = 0 if u == 0 else (2 if u == units - 1 else 1)
        kw = k_ref[0, ks:ks + nk, :]
        vw = v_ref[0, ks:ks + nk, :]
        for r0 in range(0, nq, B_SUB):
            qst = _stack_heads(q_ref[0, qs + r0:qs + r0 + B_SUB, :], is_a)
            s = lax.dot_general(qst, kw, NT_DIMS, preferred_element_type=jnp.float32)
            tbl = tbl_ref[:, geo, r0:r0 + B_SUB, :].reshape(2 * B_SUB, nk)
            s = jnp.where(tbl > 0.5 * NEG_INF, s + tbl, NEG_INF)
            pv, _, den = _softmax_pv(s, vw)
            o_ref[0, qs + r0:qs + r0 + B_SUB, :] = (
                _unstack(is_a, pv, B_SUB) / _unstack(is_a, den, B_SUB)).astype(o_ref.dtype)


def _attn_b(qkv, slabs, *, B, S):
    rows = S // GRID_W
    npair = D_MODEL // LANES
    nq = B_QROWS * GRID_W
    nk = B_KROWS * GRID_W

    def in_spec(j):
        return pl.BlockSpec((1, S, LANES), lambda p, b: (b, 0, j * npair + p))

    kern = functools.partial(_attn_b_kernel, rows=rows)
    return pl.pallas_call(
        kern,
        grid=(npair, B),
        in_specs=[in_spec(0), in_spec(1), in_spec(2),
                  pl.BlockSpec((2, 2 * NA_KH, GRID_W, LANES), lambda p, b: (p, 0, 0, 0))],
        out_specs=pl.BlockSpec((1, S, LANES), lambda p, b: (b, 0, p)),
        out_shape=jax.ShapeDtypeStruct((B, S, D_MODEL), jnp.bfloat16),
        scratch_shapes=[pltpu.VMEM((2, 3, nq, nk), jnp.float32)],
        compiler_params=_cparams(("parallel", "arbitrary")),
        name="attn_b",
    )(qkv, qkv, qkv, slabs)


def _b_bias_slabs(rpb):
    H = rpb.shape[0]
    qc = np.arange(GRID_W)[:, None]
    kc = (np.arange(LANES) % GRID_W)[None, :]
    cs = np.clip(qc - NA_KW // 2, 0, GRID_W - NA_KW)
    col_ok = (kc >= cs) & (kc < cs + NA_KW)
    dc = np.clip(kc - qc + NA_KW - 1, 0, 2 * NA_KW - 2)
    onehot = ((dc[None] == np.arange(2 * NA_KW - 1)[:, None, None]) & col_ok[None]).astype(np.float32)
    cval = jnp.einsum("hac,cqk->haqk", rpb, jnp.asarray(onehot), precision=lax.Precision.HIGHEST)
    slabs = jnp.where(jnp.asarray(col_ok)[None, None], cval * LOG2E, NEG_INF)
    return jnp.concatenate([slabs, jnp.full((H, 1, GRID_W, LANES), NEG_INF, jnp.float32)], axis=1)


def _attn_c_kernel(q_ref, k_ref, v_ref, o_ref):
    is_a = _lane_is_first_head()
    tq = C_SUB_ROWS
    for r0 in range(0, q_ref.shape[1], tq):
        rows = slice(r0, r0 + tq)
        q = q_ref[0, rows, :]
        qst = jnp.concatenate([_stack_heads(q[:, :LANES], is_a), _stack_heads(q[:, LANES:], is_a)], axis=0)
        s = lax.dot_general(qst, k_ref[0], NT_DIMS, preferred_element_type=jnp.float32)
        pv, _, den = _softmax_pv(s, v_ref[0])
        o_ref[0, rows, :LANES] = (_unstack(is_a, pv, tq) / _unstack(is_a, den, tq)).astype(o_ref.dtype)
        o_ref[0, rows, LANES:] = (_unstack(is_a, pv[2 * tq:], tq) / _unstack(is_a, den[2 * tq:], tq)).astype(o_ref.dtype)


C_SUB_ROWS = 64


def _attn_c(q, k2, v2, *, B, S, tq=2048):
    gw = 2 * LANES
    return pl.pallas_call(
        _attn_c_kernel,
        grid=(B, C_KV_HEADS, S // tq),
        in_specs=[
            pl.BlockSpec((1, tq, gw), lambda b, g, i: (b, i, g)),
            pl.BlockSpec((1, S, LANES), lambda b, g, i: (b, 0, g)),
            pl.BlockSpec((1, S, LANES), lambda b, g, i: (b, 0, g)),
        ],
        out_specs=pl.BlockSpec((1, tq, gw), lambda b, g, i: (b, i, g)),
        out_shape=jax.ShapeDtypeStruct((B, S, D_MODEL), jnp.bfloat16),
        compiler_params=_cparams(("parallel", "parallel", "parallel")),
        name="attn_c",
    )(q, k2, v2)


MLP_ROWS = 256


def _mlp_kernel(x_ref, g_ref, wu_ref, wd_ref, gf_ref, y_ref, h_ref, acc_ref, *, final_norm):
    f = pl.program_id(1)
    last = pl.num_programs(1) - 1
    tm = x_ref.shape[0]

    def weights():
        return wu_ref[...].astype(jnp.bfloat16), wd_ref[...].astype(jnp.bfloat16)

    def partial_out(rows, wu, wd):
        u = jnp.dot(h_ref[rows, :], wu, preferred_element_type=jnp.float32)
        r = jnp.maximum(u, 0.0)
        return jnp.dot((r * r).astype(jnp.bfloat16), wd, preferred_element_type=jnp.float32)

    @pl.when(f == 0)
    def _():
        wu, wd = weights()
        for r0 in range(0, tm, MLP_ROWS):
            rows = slice(r0, r0 + MLP_ROWS)
            h_ref[rows, :] = _rms(x_ref[rows, :], g_ref[...]).astype(jnp.bfloat16)
            acc_ref[rows, :] = partial_out(rows, wu, wd)

    @pl.when((f != 0) & (f != last))
    def _():
        acc_ref[...] += partial_out(slice(None), *weights())

    @pl.when(f == last)
    def _():
        wu, wd = weights()
        for r0 in range(0, tm, MLP_ROWS):
            rows = slice(r0, r0 + MLP_ROWS)
            y = x_ref[rows, :] + (acc_ref[rows, :] + partial_out(rows, wu, wd))
            if final_norm:
                y = _rms(y, gf_ref[...])
            y_ref[rows, :] = y


def _mlp(x2, g, wu, wd, gf, *, final_norm, tm=1024, tf=1024):
    T, D = x2.shape
    F = wu.shape[1]
    assert F // tf >= 2, "the kernel treats the first and the last hidden chunk separately"
    kern = functools.partial(_mlp_kernel, final_norm=final_norm)
    return pl.pallas_call(
        kern,
        grid=(T // tm, F // tf),
        in_specs=[
            pl.BlockSpec((tm, D), lambda i, f: (i, 0)),
            pl.BlockSpec((1, D), lambda i, f: (0, 0)),
            pl.BlockSpec((D, tf), lambda i, f: (0, f)),
            pl.BlockSpec((tf, D), lambda i, f: (f, 0)),
            pl.BlockSpec((1, D), lambda i, f: (0, 0)),
        ],
        out_specs=pl.BlockSpec((tm, D), lambda i, f: (i, 0)),
        out_shape=jax.ShapeDtypeStruct((T, D), jnp.float32),
        scratch_shapes=[pltpu.VMEM((tm, D), jnp.bfloat16), pltpu.VMEM((tm, D), jnp.float32)],
        compiler_params=_cparams(("parallel", "arbitrary")),
        name="mlp",
    )(x2, g, wu, wd, gf)


def _out_mlp_kernel(x_ref, a_ref, wo_ref, g_ref, wu_ref, wd_ref, y_ref, h_ref, acc_ref):
    f = pl.program_id(1)
    last = pl.num_programs(1) - 1
    tm = x_ref.shape[0]

    def weights():
        return wu_ref[...].astype(jnp.bfloat16), wd_ref[...].astype(jnp.bfloat16)

    def partial_out(rows, wu, wd):
        u = jnp.dot(h_ref[rows, :], wu, preferred_element_type=jnp.float32)
        r = jnp.maximum(u, 0.0)
        return jnp.dot((r * r).astype(jnp.bfloat16), wd, preferred_element_type=jnp.float32)

    @pl.when(f == 0)
    def _():
        wu, wd = weights()
        for r0 in range(0, tm, MLP_ROWS):
            rows = slice(r0, r0 + MLP_ROWS)
            x1 = x_ref[rows, :] + jnp.dot(a_ref[rows, :], wo_ref[...], preferred_element_type=jnp.float32)
            h_ref[rows, :] = _rms(x1, g_ref[...]).astype(jnp.bfloat16)
            acc_ref[rows, :] = x1 + partial_out(rows, wu, wd)

    @pl.when((f != 0) & (f != last))
    def _():
        acc_ref[...] += partial_out(slice(None), *weights())

    @pl.when(f == last)
    def _():
        wu, wd = weights()
        for r0 in range(0, tm, MLP_ROWS):
            rows = slice(r0, r0 + MLP_ROWS)
            y_ref[rows, :] = acc_ref[rows, :] + partial_out(rows, wu, wd)


def _out_mlp(x2, a, wo, g, wu, wd, *, tm=1024, tf=1024):
    T, D = x2.shape
    F = wu.shape[1]
    assert F // tf >= 2, "the kernel treats the first and the last hidden chunk separately"
    row = pl.BlockSpec((tm, D), lambda i, f: (i, 0))
    return pl.pallas_call(
        _out_mlp_kernel,
        grid=(T // tm, F // tf),
        in_specs=[
            row,
            row,
            pl.BlockSpec((D, D), lambda i, f: (0, 0), pipeline_mode=pl.Buffered(1)),
            pl.BlockSpec((1, D), lambda i, f: (0, 0)),
            pl.BlockSpec((D, tf), lambda i, f: (0, f)),
            pl.BlockSpec((tf, D), lambda i, f: (f, 0)),
        ],
        out_specs=row,
        out_shape=jax.ShapeDtypeStruct((T, D), jnp.float32),
        scratch_shapes=[pltpu.VMEM((tm, D), jnp.bfloat16), pltpu.VMEM((tm, D), jnp.float32)],
        compiler_params=_cparams(("parallel", "arbitrary")),
        name="out_mlp",
    )(x2, a, wo, g, wu, wd)


def _rope_angles(pos, dim):
    inv = 1.0 / (ROPE_THETA ** (np.arange(0, dim, 2, dtype=np.float64) / dim))
    return pos.astype(np.float64)[:, None] * inv[None, :]


def _rope_tables(ang):
    cos, sin = np.cos(ang).astype(np.float32), np.sin(ang).astype(np.float32)
    return np.tile(cos, (1, 4)), np.tile(np.concatenate([-sin, sin], axis=-1), (1, 2))


def kernel(x, l0_attn_norm, l0_w_in, l0_w_out, l0_mlp_norm, l0_w_up, l0_w_down, l1_attn_norm, l1_w_in, l1_rpb, l1_w_out, l1_mlp_norm, l1_w_up, l1_w_down, l2_attn_norm, l2_w_in, l2_q_norm, l2_k_norm, l2_w_out, l2_mlp_norm, l2_w_up, l2_w_down, l3_attn_norm, l3_w_in, l3_w_out, l3_mlp_norm, l3_w_up, l3_w_down, final_norm):
    B, S, D = x.shape
    bf = lambda w: w.astype(jnp.bfloat16)
    row = lambda g: g.reshape(1, -1).astype(jnp.float32)

    t = np.arange(S, dtype=np.int32)
    perm = _token_perm(S)
    cos_a, sin_a = _rope_tables(_rope_angles(t, HEAD_DIM))
    cos_p, sin_p = jnp.asarray(cos_a[perm]), jnp.asarray(sin_a[perm])
    cos_a, sin_a = jnp.asarray(cos_a), jnp.asarray(sin_a)
    cos_c, sin_c = map(jnp.asarray, _rope_tables(np.concatenate(
        [_rope_angles(t // GRID_W, HALF), _rope_angles(t % GRID_W, HALF)], axis=-1)))

    x2 = x.reshape(B * S, D)

    def layer_a(x2, attn_norm, w_in, w_out):
        gw = 3 * D_MODEL
        g = row(attn_norm)
        qkv0 = _proj(x2, g, w_in, cos_a, sin_a, rope=True, ncols=gw).reshape(B, S, -1)
        qkv12 = _proj(x2, g, w_in, cos_p, sin_p, rope=True, perm=True, col0=gw, tm=S).reshape(B, S, -1)
        o0, l0 = _attn_a01(qkv0, B=B, S=S, L=S, permuted=False, name="attn_a0")
        o1, l1 = _attn_a01(qkv12, B=B, S=S, L=S // 4, permuted=True, name="attn_a1")
        o2, l2 = _attn_a2(qkv12, B=B, S=S)
        return _merge_out(x2.reshape(B, S, D), o0, l0, o1, l1, o2, l2, bf(w_out)).reshape(B * S, D)

    def layer_b(x2, attn_norm, w_in, rpb):
        qkv = _proj(x2, row(attn_norm), w_in, cos_a, sin_a, rope=False).reshape(B, S, -1)
        return _attn_b(qkv, _b_bias_slabs(rpb.astype(jnp.float32)), B=B, S=S).reshape(B * S, D)

    def layer_c(x2, attn_norm, w_in, q_norm, k_norm):
        gain2 = lambda g: jnp.tile(g.astype(jnp.float32), 2).reshape(1, LANES)
        bd = jnp.kron(jnp.eye(2, dtype=jnp.float32),
                      jnp.full((HEAD_DIM, HEAD_DIM), 1.0 / HEAD_DIM, jnp.float32)).astype(jnp.bfloat16)
        q, k2, v2 = _proj_c(x2, row(attn_norm), bf(w_in), cos_c, sin_c, gain2(q_norm), gain2(k_norm), bd)
        return _attn_c(q.reshape(B, S, -1), k2.reshape(B, S, -1), v2.reshape(B, S, -1), B=B, S=S).reshape(B * S, D)

    fn = row(final_norm)
    x2 = layer_a(x2, l0_attn_norm, l0_w_in, l0_w_out)
    x2 = _mlp(x2, row(l0_mlp_norm), l0_w_up, l0_w_down, fn, final_norm=False)
    a = layer_b(x2, l1_attn_norm, l1_w_in, l1_rpb)
    x2 = _out_mlp(x2, a, bf(l1_w_out), row(l1_mlp_norm), l1_w_up, l1_w_down)
    a = layer_c(x2, l2_attn_norm, l2_w_in, l2_q_norm, l2_k_norm)
    x2 = _out_mlp(x2, a, bf(l2_w_out), row(l2_mlp_norm), l2_w_up, l2_w_down)
    x2 = layer_a(x2, l3_attn_norm, l3_w_in, l3_w_out)
    x2 = _mlp(x2, row(l3_mlp_norm), l3_w_up, l3_w_down, fn, final_norm=True)
    return x2.reshape(B, S, D)
```

```python
import functools

import jax
import jax.numpy as jnp
import numpy as np
from jax import lax
from jax.experimental import pallas as pl
from jax.experimental.pallas import tpu as pltpu

D_MODEL = 1024
HEAD_DIM = 64
N_HEADS = 16
D_FF = 4 * D_MODEL
ROPE_THETA = 10000.0
RMS_EPS = 1e-6
NEG_INF = -1e30
GRID_W = 64
A_GROUPS = ((128, 1), (512, 4), (2048, 16))
C_KV_HEADS = 4
LOG2E = 1.4426950408889634
LN2 = 0.6931471805599453
QK_SCALE = HEAD_DIM ** -0.5 * LOG2E

LANES = 128
HALF = HEAD_DIM // 2
VMEM_LIMIT = 56 * 1024 * 1024

NT_DIMS = (((1,), (1,)), ((), ()))


def _cparams(sem):
    return pltpu.CompilerParams(dimension_semantics=sem, vmem_limit_bytes=VMEM_LIMIT)


def _rms(x, g):
    ms = jnp.mean(x * x, axis=-1, keepdims=True)
    return (x * lax.rsqrt(ms + RMS_EPS)) * g


def _lane_is_first_head():
    return lax.broadcasted_iota(jnp.int32, (1, LANES), 1) < HEAD_DIM


def _rope128(y, cos, sin_signed):
    lane = lax.broadcasted_iota(jnp.int32, (1, LANES), 1)
    first_half = (lane % HEAD_DIM) < HALF
    partner = jnp.where(first_half, pltpu.roll(y, LANES - HALF, 1), pltpu.roll(y, HALF, 1))
    return y * cos + partner * sin_signed


PERM_GROUP = 256
PROJ_ROWS = 256


def _token_perm(S):
    t = np.arange(S).reshape(S // PERM_GROUP, 16, 4, 4)
    return t.transpose(3, 0, 2, 1).reshape(S)


def _group_perm_matrix():
    token = np.arange(PERM_GROUP).reshape(16, 4, 4).transpose(2, 1, 0).reshape(-1)
    pm = np.zeros((PERM_GROUP, PERM_GROUP), np.float32)
    pm[token, np.arange(PERM_GROUP)] = 1.0
    return pm


def _proj_kernel(x_ref, g_ref, w_ref, cos_ref, sin_ref, pm_ref, o_ref, h_ref, *, rope, perm):
    n = pl.program_id(1)
    tm = x_ref.shape[0]
    tn = o_ref.shape[1]

    def normalise():
        g = g_ref[...]
        if perm:
            ng = tm // PERM_GROUP
            run = PERM_GROUP // 4
            for grp in range(ng):
                hn = _rms(x_ref[grp * PERM_GROUP:(grp + 1) * PERM_GROUP, :], g).astype(jnp.bfloat16)
                hp = jnp.dot(pm_ref[...], hn, preferred_element_type=jnp.float32).astype(jnp.bfloat16)
                for rho in range(4):
                    dst = (rho * ng + grp) * run
                    h_ref[dst:dst + run, :] = hp[rho * run:(rho + 1) * run, :]
        else:
            for r0 in range(0, tm, PROJ_ROWS):
                h_ref[r0:r0 + PROJ_ROWS, :] = _rms(x_ref[r0:r0 + PROJ_ROWS, :], g).astype(jnp.bfloat16)

    sec = (n * tn // D_MODEL) % 3

    def sub_blocks(epilogue, scale):
        w = w_ref[...].astype(jnp.bfloat16)
        for r0 in range(0, tm, PROJ_ROWS):
            rows = slice(r0, r0 + PROJ_ROWS)
            y = jnp.dot(h_ref[rows, :], w, preferred_element_type=jnp.float32)
            epilogue(rows, y, scale)

    def plain(rows, y, scale):
        o_ref[rows, :] = (y * scale).astype(o_ref.dtype)

    def roped(rows, y, scale):
        cos = cos_ref[rows, :]
        sin = sin_ref[rows, :]
        for c in range(tn // LANES):
            sl = slice(c * LANES, (c + 1) * LANES)
            o_ref[rows, sl] = (_rope128(y[:, sl], cos, sin) * scale).astype(o_ref.dtype)

    qk = roped if rope else plain

    @pl.when(n == 0)
    def _():
        normalise()
        sub_blocks(qk, QK_SCALE)

    @pl.when((n != 0) & (sec == 0))
    def _():
        sub_blocks(qk, QK_SCALE)

    @pl.when(sec == 1)
    def _():
        sub_blocks(qk, 1.0)

    @pl.when(sec == 2)
    def _():
        sub_blocks(plain, 1.0)


def _proj(x2, g, w, cos, sin, *, rope, perm=False, col0=0, ncols=None, tm=2048, tn=1024):
    T, D = x2.shape
    N = w.shape[1] - col0 if ncols is None else ncols
    S = cos.shape[0]
    nsb = S // tm
    cb0 = col0 // tn
    kern = functools.partial(_proj_kernel, rope=rope, perm=perm)
    return pl.pallas_call(
        kern,
        grid=(T // tm, N // tn),
        in_specs=[
            pl.BlockSpec((tm, D), lambda i, n: (i, 0)),
            pl.BlockSpec((1, D), lambda i, n: (0, 0)),
            pl.BlockSpec((D, tn), lambda i, n: (0, cb0 + n)),
            pl.BlockSpec((tm, LANES), lambda i, n: (i % nsb, 0)),
            pl.BlockSpec((tm, LANES), lambda i, n: (i % nsb, 0)),
            pl.BlockSpec((PERM_GROUP, PERM_GROUP), lambda i, n: (0, 0)),
        ],
        out_specs=pl.BlockSpec((tm, tn), lambda i, n: (i, n)),
        out_shape=jax.ShapeDtypeStruct((T, N), jnp.bfloat16),
        scratch_shapes=[pltpu.VMEM((tm, D), jnp.bfloat16)],
        compiler_params=_cparams(("parallel", "arbitrary")),
        name="proj_perm" if perm else "proj",
    )(x2, g, w, cos, sin, jnp.asarray(_group_perm_matrix().T, jnp.bfloat16))


def _proj_c_kernel(x_ref, g_ref, w_ref, cos_ref, sin_ref, qg_ref, kg_ref, bd_ref,
                   q_ref, k_ref, v_ref):
    bd = bd_ref[...]
    is_a = _lane_is_first_head()
    nq = D_MODEL // LANES
    nkv = C_KV_HEADS * HEAD_DIM // LANES

    for r0 in range(0, x_ref.shape[0], PROJ_ROWS):
        rows = slice(r0, r0 + PROJ_ROWS)
        h = _rms(x_ref[rows, :], g_ref[...]).astype(jnp.bfloat16)
        y = jnp.dot(h, w_ref[...], preferred_element_type=jnp.float32)
        cos = cos_ref[rows, :]
        sin = sin_ref[rows, :]

        def head_norm(c, gain):
            yc = y[:, c * LANES:(c + 1) * LANES]
            sq = yc * yc
            hi = sq.astype(jnp.bfloat16)
            lo = (sq - hi.astype(jnp.float32)).astype(jnp.bfloat16)
            ms = (jnp.dot(hi, bd, preferred_element_type=jnp.float32)
                  + jnp.dot(lo, bd, preferred_element_type=jnp.float32))
            return (yc * lax.rsqrt(ms + RMS_EPS)) * gain

        for c in range(nq):
            qn = head_norm(c, qg_ref[...])
            q_ref[rows, c * LANES:(c + 1) * LANES] = (_rope128(qn, cos, sin) * QK_SCALE).astype(q_ref.dtype)
        for c in range(nkv):
            kn = _rope128(head_norm(nq + c, kg_ref[...]), cos, sin)
            ksw = pltpu.roll(kn, HEAD_DIM, 1)
            k_ref[rows, (2 * c) * LANES:(2 * c + 1) * LANES] = jnp.where(is_a, kn, ksw).astype(k_ref.dtype)
            k_ref[rows, (2 * c + 1) * LANES:(2 * c + 2) * LANES] = jnp.where(is_a, ksw, kn).astype(k_ref.dtype)
            vc = y[:, (nq + nkv + c) * LANES:(nq + nkv + c + 1) * LANES]
            vsw = pltpu.roll(vc, HEAD_DIM, 1)
            v_ref[rows, (2 * c) * LANES:(2 * c + 1) * LANES] = jnp.where(is_a, vc, vsw).astype(v_ref.dtype)
            v_ref[rows, (2 * c + 1) * LANES:(2 * c + 2) * LANES] = jnp.where(is_a, vsw, vc).astype(v_ref.dtype)


def _proj_c(x2, g, w, cos, sin, qg, kg, bd, *, tm=1024):
    T, D = x2.shape
    N = w.shape[1]
    S = cos.shape[0]
    nsb = S // tm
    kvw = 2 * C_KV_HEADS * HEAD_DIM
    full = lambda shape: pl.BlockSpec(shape, lambda i: (0, 0))
    return pl.pallas_call(
        _proj_c_kernel,
        grid=(T // tm,),
        in_specs=[
            pl.BlockSpec((tm, D), lambda i: (i, 0)),
            full((1, D)),
            full((D, N)),
            pl.BlockSpec((tm, LANES), lambda i: (i % nsb, 0)),
            pl.BlockSpec((tm, LANES), lambda i: (i % nsb, 0)),
            full((1, LANES)),
            full((1, LANES)),
            full((LANES, LANES)),
        ],
        out_specs=[
            pl.BlockSpec((tm, D_MODEL), lambda i: (i, 0)),
            pl.BlockSpec((tm, kvw), lambda i: (i, 0)),
            pl.BlockSpec((tm, kvw), lambda i: (i, 0)),
        ],
        out_shape=[
            jax.ShapeDtypeStruct((T, D_MODEL), jnp.bfloat16),
            jax.ShapeDtypeStruct((T, kvw), jnp.bfloat16),
            jax.ShapeDtypeStruct((T, kvw), jnp.bfloat16),
        ],
        compiler_params=_cparams(("parallel",)),
        name="proj_c",
    )(x2, g, w, cos, sin, qg, kg, bd)


def _stack_heads(q2, is_a):
    zero = jnp.zeros_like(q2)
    return jnp.concatenate([jnp.where(is_a, q2, zero), jnp.where(is_a, zero, q2)], axis=0)


def _softmax_pv(s, vw, mxu_sums=True):
    m = jnp.max(s, axis=-1, keepdims=True)
    if not mxu_sums:
        p = jnp.exp2(s - m)
        den = jnp.sum(p, axis=-1, keepdims=True)
        return jnp.dot(p.astype(vw.dtype), vw, preferred_element_type=jnp.float32), m, den
    p = jnp.exp2((s - m).astype(jnp.bfloat16))
    vaug = jnp.concatenate([vw, jnp.ones_like(vw)], axis=1)
    r = jnp.dot(p, vaug, preferred_element_type=jnp.float32)
    return r[:, :LANES], m, r[:, LANES:]


def _unstack(is_a, x, n):
    return jnp.where(is_a, x[:n], x[n:2 * n])


A_HALF = 64
A_QB = 128
NPAIR = D_MODEL // LANES


def _band_pairs(load_q, load_k, load_v, valid, store_o, qb, mxu_sums=True):
    is_a = _lane_is_first_head()
    lane = lax.broadcasted_iota(jnp.int32, (1, LANES), 1)
    m_tile = jnp.zeros((qb, LANES), jnp.float32)
    den_tile = jnp.ones((qb, LANES), jnp.float32)
    for p in range(NPAIR):
        cols = slice(p * LANES, (p + 1) * LANES)
        qst = _stack_heads(load_q(cols), is_a)
        vw = load_v(cols)
        s = lax.dot_general(qst, load_k(cols), NT_DIMS, preferred_element_type=jnp.float32)
        s = jnp.where(valid, s, NEG_INF)
        pv, m, den = _softmax_pv(s, vw, mxu_sums)
        store_o(cols, _unstack(is_a, pv, qb) / _unstack(is_a, den, qb))
        in_a, in_b = lane == 2 * p, lane == 2 * p + 1
        m_tile = jnp.where(in_a, m[:qb], jnp.where(in_b, m[qb:], m_tile))
        den_tile = jnp.where(in_a, den[:qb], jnp.where(in_b, den[qb:], den_tile))
    return m_tile * LN2 + jnp.log(den_tile)


def _attn_a_kernel(q_ref, k_ref, v_ref, o_ref, lse_ref, mask_ref, *, L, permuted):
    qb, W = A_QB, 2 * A_QB
    nq = L // qb

    def pos(l):
        return (l // 64) * 64 + 4 * (l % 16) + (l % 64) // 16 if permuted else l

    rel = (pos(lax.broadcasted_iota(jnp.int32, (2 * qb, W), 0) % qb)
           - pos(lax.broadcasted_iota(jnp.int32, (2 * qb, W), 1)))
    for case, delta in enumerate((0, A_HALF, W - qb)):
        mask_ref[case] = (jnp.abs(rel + delta) <= A_HALF).astype(jnp.int32)

    def body(iq, carry):
        qs = pl.multiple_of(iq * qb, qb)
        ks = pl.multiple_of(jnp.clip(qs - A_HALF, 0, L - W), A_HALF)
        case = jnp.where(iq == 0, 0, jnp.where(iq == nq - 1, 2, 1))
        valid = mask_ref[case] != 0

        def store_o(cols, o):
            o_ref[0, pl.ds(qs, qb), cols] = o.astype(o_ref.dtype)

        lse_ref[0, pl.ds(qs, qb), :] = _band_pairs(
            lambda cols: q_ref[0, pl.ds(qs, qb), cols],
            lambda cols: k_ref[0, pl.ds(ks, W), cols],
            lambda cols: v_ref[0, pl.ds(ks, W), cols],
            valid, store_o, qb)
        return carry

    lax.fori_loop(0, nq, body, 0, unroll=min(nq, 8))


def _attn_a01(qkv, *, B, S, L, permuted, name):
    runs = S // L
    spec = lambda j: pl.BlockSpec((1, L, D_MODEL), lambda b, r: (b, r, j))
    kern = functools.partial(_attn_a_kernel, L=L, permuted=permuted)
    return pl.pallas_call(
        kern,
        grid=(B, runs),
        in_specs=[spec(0), spec(1), spec(2)],
        out_specs=[pl.BlockSpec((1, L, D_MODEL), lambda b, r: (b, r, 0)),
                   pl.BlockSpec((1, L, LANES), lambda b, r: (b, r, 0))],
        out_shape=[jax.ShapeDtypeStruct((B, S, D_MODEL), jnp.bfloat16),
                   jax.ShapeDtypeStruct((B, S, LANES), jnp.float32)],
        scratch_shapes=[pltpu.VMEM((3, 2 * A_QB, 2 * A_QB), jnp.int32)],
        compiler_params=_cparams(("parallel", "parallel")),
        name=name,
    )(qkv, qkv, qkv)


def _attn_a2_kernel(q_ref, k_ref, v_ref, o_ref, lse_ref):
    ng = q_ref.shape[2]
    L = ng * 16
    row = lax.broadcasted_iota(jnp.int32, (2 * L, L), 0) % L
    col = lax.broadcasted_iota(jnp.int32, (2 * L, L), 1)
    valid = jnp.abs(row - col) <= A_HALF

    for a in range(4):
        def load(ref, a=a):
            return lambda cols: ref[0, 0, :, a, :, cols].reshape(L, LANES)

        def store_o(cols, o, a=a):
            o_ref[0, 0, :, a, :, cols] = o.astype(o_ref.dtype).reshape(ng, 16, LANES)

        lse = _band_pairs(load(q_ref), load(k_ref), load(v_ref), valid, store_o, L, mxu_sums=False)
        lse_ref[0, 0, :, a, :, :] = lse.reshape(ng, 16, LANES)


def _attn_a2(qkv, *, B, S):
    ng = S // PERM_GROUP
    view = qkv.reshape(B, 4, ng, 4, 16, qkv.shape[2])
    blk = lambda c: (1, 1, ng, 4, 16, c)
    spec = lambda j: pl.BlockSpec(blk(D_MODEL), lambda b, r: (b, r, 0, 0, 0, 3 + j))
    o, lse = pl.pallas_call(
        _attn_a2_kernel,
        grid=(B, 4),
        in_specs=[spec(0), spec(1), spec(2)],
        out_specs=[pl.BlockSpec(blk(D_MODEL), lambda b, r: (b, r, 0, 0, 0, 0)),
                   pl.BlockSpec(blk(LANES), lambda b, r: (b, r, 0, 0, 0, 0))],
        out_shape=[jax.ShapeDtypeStruct((B, 4, ng, 4, 16, D_MODEL), jnp.bfloat16),
                   jax.ShapeDtypeStruct((B, 4, ng, 4, 16, LANES), jnp.float32)],
        compiler_params=_cparams(("parallel", "parallel")),
        name="attn_a2",
    )(view, view, view)
    return o.reshape(B, S, D_MODEL), lse.reshape(B, S, LANES)


def _split_bf16(v, parts):
    out = []
    for _ in range(parts - 1):
        hi = v.astype(jnp.bfloat16)
        out.append(hi)
        v = v - hi.astype(jnp.float32)
    out.append(v.astype(jnp.bfloat16))
    return out


def _merge_out_kernel(x_ref, o0_ref, l0_ref, o1_ref, l1_ref, o2_ref, l2_ref, pm_ref, ee_ref, w_ref, y_ref):
    pm = pm_ref[...]
    ee = ee_ref[...]
    f32 = jnp.float32

    def natural(o_ref, l_ref, g):
        lse = l_ref[0, :, g].reshape(PERM_GROUP, LANES)
        both = jnp.concatenate([o_ref[0, :, g].reshape(PERM_GROUP, D_MODEL)] + _split_bf16(lse, 3), axis=1)
        r = jnp.dot(pm, both, preferred_element_type=f32)
        d = D_MODEL
        return r[:, :d], r[:, d:d + LANES] + r[:, d + LANES:d + 2 * LANES] + r[:, d + 2 * LANES:]

    for g in range(x_ref.shape[1] // PERM_GROUP):
        rows = slice(g * PERM_GROUP, (g + 1) * PERM_GROUP)
        o0 = o0_ref[0, rows, :].astype(f32)
        l0 = l0_ref[0, rows, :]
        o1, l1 = natural(o1_ref, l1_ref, g)
        o2, l2 = natural(o2_ref, l2_ref, g)
        m = jnp.maximum(jnp.maximum(l0, l1), l2)
        e0, e1, e2 = jnp.exp(l0 - m), jnp.exp(l1 - m), jnp.exp(l2 - m)
        den = e0 + e1 + e2

        def spread(e):
            return jnp.dot(jnp.concatenate(_split_bf16(e / den, 2), axis=1), ee, preferred_element_type=f32)

        a = (o0 + spread(e1) * (o1 - o0) + spread(e2) * (o2 - o0)).astype(jnp.bfloat16)
        y_ref[0, rows, :] = x_ref[0, rows, :] + jnp.dot(a, w_ref[...], preferred_element_type=f32)


def _merge_out(x3, o0, l0, o1, l1, o2, l2, w, *, groups=4):
    B, S, D = x3.shape
    ng = S // PERM_GROUP
    tm = groups * PERM_GROUP
    pview = lambda a: a.reshape(B, 4, ng, PERM_GROUP // 4, a.shape[2])
    nat = lambda c: pl.BlockSpec((1, tm, c), lambda b, j: (b, j, 0))
    per = lambda c: pl.BlockSpec((1, 4, groups, PERM_GROUP // 4, c), lambda b, j: (b, 0, j, 0, 0))
    full = lambda shape: pl.BlockSpec(shape, lambda b, j: (0, 0))

    pm = _group_perm_matrix()
    ee = np.zeros((2 * LANES, D_MODEL), np.float32)
    for h in range(N_HEADS):
        ee[h, h * HEAD_DIM:(h + 1) * HEAD_DIM] = 1.0
        ee[LANES + h, h * HEAD_DIM:(h + 1) * HEAD_DIM] = 1.0

    return pl.pallas_call(
        _merge_out_kernel,
        grid=(B, ng // groups),
        in_specs=[nat(D), nat(D), nat(LANES), per(D), per(LANES), per(D), per(LANES),
                  full((PERM_GROUP, PERM_GROUP)), full((2 * LANES, D)), full((D, D))],
        out_specs=nat(D),
        out_shape=jax.ShapeDtypeStruct((B, S, D), jnp.float32),
        compiler_params=_cparams(("parallel", "parallel")),
        name="merge_out",
    )(x3, o0, l0, pview(o1), pview(l1), pview(o2), pview(l2),
      jnp.asarray(pm, jnp.bfloat16), jnp.asarray(ee, jnp.bfloat16), w)


def _out_kernel(x_ref, a_ref, w_ref, y_ref):
    y_ref[...] = x_ref[...] + jnp.dot(a_ref[...], w_ref[...], preferred_element_type=jnp.float32)


def _out_proj(x2, a, w, *, tm=1024):
    T, D = x2.shape
    row = pl.BlockSpec((tm, D), lambda i: (i, 0))
    return pl.pallas_call(
        _out_kernel,
        grid=(T // tm,),
        in_specs=[row, row, pl.BlockSpec((D, D), lambda i: (0, 0))],
        out_specs=row,
        out_shape=jax.ShapeDtypeStruct((T, D), jnp.float32),
        compiler_params=_cparams(("parallel",)),
        name="out_proj",
    )(x2, a, w)


B_QROWS = 4
B_KROWS = 12
B_SUB = 128
NA_KH = 8
NA_KW = 16


def _b_slab_index(rows):
    units = rows // B_QROWS
    idx = np.zeros((3, B_QROWS, B_KROWS), np.int32)
    for geo, u in enumerate((0, 1, units - 1)):
        kr0 = min(max(u * B_QROWS - NA_KH // 2, 0), rows - B_KROWS)
        for a in range(B_QROWS):
            qr = u * B_QROWS + a
            rs = min(max(qr - NA_KH // 2, 0), rows - NA_KH)
            for c in range(B_KROWS):
                kr = kr0 + c
                idx[geo, a, c] = kr - qr + NA_KH - 1 if rs <= kr < rs + NA_KH else 2 * NA_KH - 1
    return idx


def _attn_b_kernel(q_ref, k_ref, v_ref, slab_ref, o_ref, tbl_ref, *, rows):
    is_a = _lane_is_first_head()
    nq = B_QROWS * GRID_W
    nk = B_KROWS * GRID_W
    units = rows // B_QROWS
    idx = _b_slab_index(rows)

    @pl.when(pl.program_id(1) == 0)
    def _():
        for h in range(2):
            for geo in range(3):
                for qr in range(B_QROWS):
                    for j in range(B_KROWS // 2):
                        even = slab_ref[h, int(idx[geo, qr, 2 * j])]
                        odd = slab_ref[h, int(idx[geo, qr, 2 * j + 1])]
                        tbl_ref[h, geo, qr * GRID_W:(qr + 1) * GRID_W, j * LANES:(j + 1) * LANES] = (
                            jnp.where(is_a, even, odd))

    for u in range(units):
        qs = u * nq
        kr0 = min(max(u * B_QROWS - NA_KH // 2, 0), rows - B_KROWS)
        ks = kr0 * GRID_W
        geo = 0 if u == 0 else (2 if u == units - 1 else 1)
        kw = k_ref[0, ks:ks + nk, :]
        vw = v_ref[0, ks:ks + nk, :]
        for r0 in range(0, nq, B_SUB):
            qst = _stack_heads(q_ref[0, qs + r0:qs + r0 + B_SUB, :], is_a)
            s = lax.dot_general(qst, kw, NT_DIMS, preferred_element_type=jnp.float32)
            tbl = tbl_ref[:, geo, r0:r0 + B_SUB, :].reshape(2 * B_SUB, nk)
            s = jnp.where(tbl > 0.5 * NEG_INF, s + tbl, NEG_INF)
            pv, _, den = _softmax_pv(s, vw)
            o_ref[0, qs + r0:qs + r0 + B_SUB, :] = (
                _unstack(is_a, pv, B_SUB) / _unstack(is_a, den, B_SUB)).astype(o_ref.dtype)


def _attn_b(qkv, slabs, *, B, S):
    rows = S // GRID_W
    npair = D_MODEL // LANES
    nq = B_QROWS * GRID_W
    nk = B_KROWS * GRID_W

    def in_spec(j):
        return pl.BlockSpec((1, S, LANES), lambda p, b: (b, 0, j * npair + p))

    kern = functools.partial(_attn_b_kernel, rows=rows)
    return pl.pallas_call(
        kern,
        grid=(npair, B),
        in_specs=[in_spec(0), in_spec(1), in_spec(2),
                  pl.BlockSpec((2, 2 * NA_KH, GRID_W, LANES), lambda p, b: (p, 0, 0, 0))],
        out_specs=pl.BlockSpec((1, S, LANES), lambda p, b: (b, 0, p)),
        out_shape=jax.ShapeDtypeStruct((B, S, D_MODEL), jnp.bfloat16),
        scratch_shapes=[pltpu.VMEM((2, 3, nq, nk), jnp.float32)],
        compiler_params=_cparams(("parallel", "arbitrary")),
        name="attn_b",
    )(qkv, qkv, qkv, slabs)


def _b_bias_slabs(rpb):
    H = rpb.shape[0]
    qc = np.arange(GRID_W)[:, None]
    kc = (np.arange(LANES) % GRID_W)[None, :]
    cs = np.clip(qc - NA_KW // 2, 0, GRID_W - NA_KW)
    col_ok = (kc >= cs) & (kc < cs + NA_KW)
    dc = np.clip(kc - qc + NA_KW - 1, 0, 2 * NA_KW - 2)
    onehot = ((dc[None] == np.arange(2 * NA_KW - 1)[:, None, None]) & col_ok[None]).astype(np.float32)
    cval = jnp.einsum("hac,cqk->haqk", rpb, jnp.asarray(onehot), precision=lax.Precision.HIGHEST)
    slabs = jnp.where(jnp.asarray(col_ok)[None, None], cval * LOG2E, NEG_INF)
    return jnp.concatenate([slabs, jnp.full((H, 1, GRID_W, LANES), NEG_INF, jnp.float32)], axis=1)


def _attn_c_kernel(q_ref, k_ref, v_ref, o_ref):
    is_a = _lane_is_first_head()
    tq = C_SUB_ROWS
    for r0 in range(0, q_ref.shape[1], tq):
        rows = slice(r0, r0 + tq)
        q = q_ref[0, rows, :]
        qst = jnp.concatenate([_stack_heads(q[:, :LANES], is_a), _stack_heads(q[:, LANES:], is_a)], axis=0)
        s = lax.dot_general(qst, k_ref[0], NT_DIMS, preferred_element_type=jnp.float32)
        pv, _, den = _softmax_pv(s, v_ref[0])
        o_ref[0, rows, :LANES] = (_unstack(is_a, pv, tq) / _unstack(is_a, den, tq)).astype(o_ref.dtype)
        o_ref[0, rows, LANES:] = (_unstack(is_a, pv[2 * tq:], tq) / _unstack(is_a, den[2 * tq:], tq)).astype(o_ref.dtype)


C_SUB_ROWS = 64


def _attn_c(q, k2, v2, *, B, S, tq=2048):
    gw = 2 * LANES
    return pl.pallas_call(
        _attn_c_kernel,
        grid=(B, C_KV_HEADS, S // tq),
        in_specs=[
            pl.BlockSpec((1, tq, gw), lambda b, g, i: (b, i, g)),
            pl.BlockSpec((1, S, LANES), lambda b, g, i: (b, 0, g)),
            pl.BlockSpec((1, S, LANES), lambda b, g, i: (b, 0, g)),
        ],
        out_specs=pl.BlockSpec((1, tq, gw), lambda b, g, i: (b, i, g)),
        out_shape=jax.ShapeDtypeStruct((B, S, D_MODEL), jnp.bfloat16),
        compiler_params=_cparams(("parallel", "parallel", "parallel")),
        name="attn_c",
    )(q, k2, v2)


MLP_ROWS = 256


def _mlp_kernel(x_ref, g_ref, wu_ref, wd_ref, gf_ref, y_ref, h_ref, acc_ref, *, final_norm):
    f = pl.program_id(1)
    last = pl.num_programs(1) - 1
    tm = x_ref.shape[0]

    def weights():
        return wu_ref[...].astype(jnp.bfloat16), wd_ref[...].astype(jnp.bfloat16)

    def partial_out(rows, wu, wd):
        u = jnp.dot(h_ref[rows, :], wu, preferred_element_type=jnp.float32)
        r = jnp.maximum(u, 0.0)
        return jnp.dot((r * r).astype(jnp.bfloat16), wd, preferred_element_type=jnp.float32)

    @pl.when(f == 0)
    def _():
        wu, wd = weights()
        for r0 in range(0, tm, MLP_ROWS):
            rows = slice(r0, r0 + MLP_ROWS)
            h_ref[rows, :] = _rms(x_ref[rows, :], g_ref[...]).astype(jnp.bfloat16)
            acc_ref[rows, :] = partial_out(rows, wu, wd)

    @pl.when((f != 0) & (f != last))
    def _():
        acc_ref[...] += partial_out(slice(None), *weights())

    @pl.when(f == last)
    def _():
        wu, wd = weights()
        for r0 in range(0, tm, MLP_ROWS):
            rows = slice(r0, r0 + MLP_ROWS)
            y = x_ref[rows, :] + (acc_ref[rows, :] + partial_out(rows, wu, wd))
            if final_norm:
                y = _rms(y, gf_ref[...])
            y_ref[rows, :] = y


def _mlp(x2, g, wu, wd, gf, *, final_norm, tm=1024, tf=1024):
    T, D = x2.shape
    F = wu.shape[1]
    assert F // tf >= 2, "the kernel treats the first and the last hidden chunk separately"
    kern = functools.partial(_mlp_kernel, final_norm=final_norm)
    return pl.pallas_call(
        kern,
        grid=(T // tm, F // tf),
        in_specs=[
            pl.BlockSpec((tm, D), lambda i, f: (i, 0)),
            pl.BlockSpec((1, D), lambda i, f: (0, 0)),
            pl.BlockSpec((D, tf), lambda i, f: (0, f)),
            pl.BlockSpec((tf, D), lambda i, f: (f, 0)),
            pl.BlockSpec((1, D), lambda i, f: (0, 0)),
        ],
        out_specs=pl.BlockSpec((tm, D), lambda i, f: (i, 0)),
        out_shape=jax.ShapeDtypeStruct((T, D), jnp.float32),
        scratch_shapes=[pltpu.VMEM((tm, D), jnp.bfloat16), pltpu.VMEM((tm, D), jnp.float32)],
        compiler_params=_cparams(("parallel", "arbitrary")),
        name="mlp",
    )(x2, g, wu, wd, gf)


OUT_ROWS = 512


def _out_mlp_kernel(x_ref, a_ref, wo_ref, g_ref, wu_ref, wd_ref, y_ref, h_ref, acc_ref):
    f = pl.program_id(1)
    last = pl.num_programs(1) - 1
    tm = x_ref.shape[0]

    def weights():
        return wu_ref[...].astype(jnp.bfloat16), wd_ref[...].astype(jnp.bfloat16)

    def partial_out(rows, wu, wd):
        u = jnp.dot(h_ref[rows, :], wu, preferred_element_type=jnp.float32)
        r = jnp.maximum(u, 0.0)
        return jnp.dot((r * r).astype(jnp.bfloat16), wd, preferred_element_type=jnp.float32)

    @pl.when(f == 0)
    def _():
        wu, wd = weights()
        for r0 in range(0, tm, OUT_ROWS):
            rows = slice(r0, r0 + OUT_ROWS)
            acc_ref[rows, :] = x_ref[rows, :] + jnp.dot(a_ref[rows, :], wo_ref[...], preferred_element_type=jnp.float32)
        for r0 in range(0, tm, MLP_ROWS):
            rows = slice(r0, r0 + MLP_ROWS)
            x1 = acc_ref[rows, :]
            h_ref[rows, :] = _rms(x1, g_ref[...]).astype(jnp.bfloat16)
            acc_ref[rows, :] = x1 + partial_out(rows, wu, wd)

    @pl.when((f != 0) & (f != last))
    def _():
        acc_ref[...] += partial_out(slice(None), *weights())

    @pl.when(f == last)
    def _():
        wu, wd = weights()
        for r0 in range(0, tm, MLP_ROWS):
            rows = slice(r0, r0 + MLP_ROWS)
            y_ref[rows, :] = acc_ref[rows, :] + partial_out(rows, wu, wd)


def _out_mlp(x2, a, wo, g, wu, wd, *, tm=1024, tf=1024):
    T, D = x2.shape
    F = wu.shape[1]
    assert F // tf >= 2, "the kernel treats the first and the last hidden chunk separately"
    row = pl.BlockSpec((tm, D), lambda i, f: (i, 0))
    return pl.pallas_call(
        _out_mlp_kernel,
        grid=(T // tm, F // tf),
        in_specs=[
            row,
            row,
            pl.BlockSpec((D, D), lambda i, f: (0, 0), pipeline_mode=pl.Buffered(1)),
            pl.BlockSpec((1, D), lambda i, f: (0, 0)),
            pl.BlockSpec((D, tf), lambda i, f: (0, f)),
            pl.BlockSpec((tf, D), lambda i, f: (f, 0)),
        ],
        out_specs=row,
        out_shape=jax.ShapeDtypeStruct((T, D), jnp.float32),
        scratch_shapes=[pltpu.VMEM((tm, D), jnp.bfloat16), pltpu.VMEM((tm, D), jnp.float32)],
        compiler_params=_cparams(("parallel", "arbitrary")),
        name="out_mlp",
    )(x2, a, wo, g, wu, wd)


def _rope_angles(pos, dim):
    inv = 1.0 / (ROPE_THETA ** (np.arange(0, dim, 2, dtype=np.float64) / dim))
    return pos.astype(np.float64)[:, None] * inv[None, :]


def _rope_tables(ang):
    cos, sin = np.cos(ang).astype(np.float32), np.sin(ang).astype(np.float32)
    return np.tile(cos, (1, 4)), np.tile(np.concatenate([-sin, sin], axis=-1), (1, 2))


def kernel(x, l0_attn_norm, l0_w_in, l0_w_out, l0_mlp_norm, l0_w_up, l0_w_down, l1_attn_norm, l1_w_in, l1_rpb, l1_w_out, l1_mlp_norm, l1_w_up, l1_w_down, l2_attn_norm, l2_w_in, l2_q_norm, l2_k_norm, l2_w_out, l2_mlp_norm, l2_w_up, l2_w_down, l3_attn_norm, l3_w_in, l3_w_out, l3_mlp_norm, l3_w_up, l3_w_down, final_norm):
    B, S, D = x.shape
    bf = lambda w: w.astype(jnp.bfloat16)
    row = lambda g: g.reshape(1, -1).astype(jnp.float32)

    t = np.arange(S, dtype=np.int32)
    perm = _token_perm(S)
    cos_a, sin_a = _rope_tables(_rope_angles(t, HEAD_DIM))
    cos_p, sin_p = jnp.asarray(cos_a[perm]), jnp.asarray(sin_a[perm])
    cos_a, sin_a = jnp.asarray(cos_a), jnp.asarray(sin_a)
    cos_c, sin_c = map(jnp.asarray, _rope_tables(np.concatenate(
        [_rope_angles(t // GRID_W, HALF), _rope_angles(t % GRID_W, HALF)], axis=-1)))

    x2 = x.reshape(B * S, D)

    def layer_a(x2, attn_norm, w_in, w_out):
        gw = 3 * D_MODEL
        g = row(attn_norm)
        qkv0 = _proj(x2, g, w_in, cos_a, sin_a, rope=True, ncols=gw).reshape(B, S, -1)
        qkv12 = _proj(x2, g, w_in, cos_p, sin_p, rope=True, perm=True, col0=gw, tm=S).reshape(B, S, -1)
        o0, l0 = _attn_a01(qkv0, B=B, S=S, L=S, permuted=False, name="attn_a0")
        o1, l1 = _attn_a01(qkv12, B=B, S=S, L=S // 4, permuted=True, name="attn_a1")
        o2, l2 = _attn_a2(qkv12, B=B, S=S)
        return _merge_out(x2.reshape(B, S, D), o0, l0, o1, l1, o2, l2, bf(w_out)).reshape(B * S, D)

    def layer_b(x2, attn_norm, w_in, rpb):
        qkv = _proj(x2, row(attn_norm), w_in, cos_a, sin_a, rope=False).reshape(B, S, -1)
        return _attn_b(qkv, _b_bias_slabs(rpb.astype(jnp.float32)), B=B, S=S).reshape(B * S, D)

    def layer_c(x2, attn_norm, w_in, q_norm, k_norm):
        gain2 = lambda g: jnp.tile(g.astype(jnp.float32), 2).reshape(1, LANES)
        bd = jnp.kron(jnp.eye(2, dtype=jnp.float32),
                      jnp.full((HEAD_DIM, HEAD_DIM), 1.0 / HEAD_DIM, jnp.float32)).astype(jnp.bfloat16)
        q, k2, v2 = _proj_c(x2, row(attn_norm), bf(w_in), cos_c, sin_c, gain2(q_norm), gain2(k_norm), bd)
        return _attn_c(q.reshape(B, S, -1), k2.reshape(B, S, -1), v2.reshape(B, S, -1), B=B, S=S).reshape(B * S, D)

    fn = row(final_norm)
    x2 = layer_a(x2, l0_attn_norm, l0_w_in, l0_w_out)
    x2 = _mlp(x2, row(l0_mlp_norm), l0_w_up, l0_w_down, fn, final_norm=False)
    a = layer_b(x2, l1_attn_norm, l1_w_in, l1_rpb)
    x2 = _out_mlp(x2, a, bf(l1_w_out), row(l1_mlp_norm), l1_w_up, l1_w_down)
    a = layer_c(x2, l2_attn_norm, l2_w_in, l2_q_norm, l2_k_norm)
    x2 = _out_mlp(x2, a, bf(l2_w_out), row(l2_mlp_norm), l2_w_up, l2_w_down)
    x2 = layer_a(x2, l3_attn_norm, l3_w_in, l3_w_out)
    x2 = _mlp(x2, row(l3_mlp_norm), l3_w_up, l3_w_down, fn, final_norm=True)
    return x2.reshape(B, S, D)
```
